```python
import jax, jax.numpy as jnp
from jax import lax
import numpy as np

D_MODEL = 1024
BATCH = 2
SEQ = 8192
DEPTH = 2
DEC_BATCH = 128
DEC_SEQ = 1
PAST_LEN = 8192
PAGE_SIZE = 128

D_CONV = D_MODEL // 2
CONV_W = 3
HEAD_DIM = 64
N_HEADS = (D_MODEL // 2) // HEAD_DIM
N_KV_HEADS = 2
GROUP = N_HEADS // N_KV_HEADS
WINDOW = 128
BLK = WINDOW
D_ATTN = N_HEADS * HEAD_DIM
D_KV = N_KV_HEADS * HEAD_DIM
MIX = D_CONV + D_ATTN
IN_COLS = 3 * D_CONV + D_ATTN + 2 * D_KV
MEM_LEN = 256
MEM_HEADS = 4
MEM_HEAD_DIM = D_MODEL // MEM_HEADS
MEM_INNER = MEM_HEADS * MEM_HEAD_DIM
D_FF = 2816
ALPHA = (2.0 * DEPTH) ** 0.25
BETA = (8.0 * DEPTH) ** -0.25
LN_EPS = 1e-5

kernel_name = 'hymba_conv_swa_sink_macaron_deepnorm_step'


def _layer_norm(x, g, b):
    xf = x.astype(jnp.float32)
    mu = xf.mean(-1, keepdims=True)
    var = jnp.square(xf - mu).mean(-1, keepdims=True)
    y = (xf - mu) * lax.rsqrt(var + LN_EPS) * g.astype(jnp.float32) + b.astype(jnp.float32)
    return y.astype(x.dtype)


def _deepnorm(x, sub, g, b):
    return _layer_norm(ALPHA * x + sub, g, b)


def _swiglu_half(x, w_gu, w_down):
    g, u = jnp.split(x @ w_gu, 2, axis=-1)
    return 0.5 * ((jax.nn.silu(g) * u) @ w_down)


def _split_in(p):
    cuts = [D_CONV, 2 * D_CONV, 3 * D_CONV, 3 * D_CONV + D_ATTN, 3 * D_CONV + D_ATTN + D_KV]
    return jnp.split(p, cuts, axis=-1)


def _dwconv(u_pad, w):
    T = u_pad.shape[1] - (CONV_W - 1)
    y = w[0] * u_pad[:, 0:T]
    for tap in range(1, CONV_W):
        y = y + w[tap] * u_pad[:, tap:tap + T]
    return y


def _sink_attend(q, k, v, mask, sinks):
    s = jnp.einsum('...qkgd,...ckd->...kgqc', q.astype(jnp.float32), k.astype(jnp.float32)) * (HEAD_DIM ** -0.5)
    s = jnp.where(mask, s, -jnp.inf)
    sk = sinks.astype(jnp.float32).reshape(N_KV_HEADS, GROUP, 1, 1)
    m = jnp.maximum(s.max(-1, keepdims=True), sk)
    p = jnp.exp(s - m)
    denom = p.sum(-1, keepdims=True) + jnp.exp(sk - m)
    o = jnp.einsum('...kgqc,...ckd->...qkgd', p / denom, v.astype(jnp.float32))
    return o.astype(v.dtype)


def _swa_prompt(q, k, v, sinks):
    Bn, S = q.shape[0], q.shape[1]
    nb = S // BLK
    qb = q.reshape(Bn, nb, BLK, N_KV_HEADS, GROUP, HEAD_DIM)
    kb = k.reshape(Bn, nb, BLK, N_KV_HEADS, HEAD_DIM)
    vb = v.reshape(Bn, nb, BLK, N_KV_HEADS, HEAD_DIM)
    pad = jnp.zeros_like(kb[:, :1])
    kk = jnp.concatenate([jnp.concatenate([pad, kb[:, :-1]], axis=1), kb], axis=2)
    vv = jnp.concatenate([jnp.concatenate([pad, vb[:, :-1]], axis=1), vb], axis=2)
    a = jnp.arange(BLK)[None, :, None]
    c = jnp.arange(2 * BLK)[None, None, :]
    blk = jnp.arange(nb)[:, None, None]
    rel = BLK + a - c
    mask = (rel >= 0) & (rel <= WINDOW) & ((blk - 1) * BLK + c >= 0)
    o = _sink_attend(qb, kk, vv, mask[None, :, None, None], sinks)
    return o.reshape(Bn, S, D_ATTN)


def _swa_sample(q, k_new, v_new, k_buf, v_buf, sinks):
    Bn, T = q.shape[0], q.shape[1]
    kk = jnp.concatenate([k_buf, k_new], axis=1)
    vv = jnp.concatenate([v_buf, v_new], axis=1)
    a = jnp.arange(T)[:, None]
    c = jnp.arange(WINDOW + T)[None, :]
    rel = WINDOW + a - c
    mask = (rel >= 0) & (rel <= WINDOW)
    o = _sink_attend(q.reshape(Bn, T, N_KV_HEADS, GROUP, HEAD_DIM), kk, vv, mask, sinks)
    return o.reshape(Bn, T, D_ATTN), kk[:, -WINDOW:], vv[:, -WINDOW:]


def _token_mix(x, w_in, conv_w, w_out, sinks, conv_prev, k_buf, v_buf):
    Bn, T = x.shape[0], x.shape[1]
    bg, cg, hc, q, k, v = _split_in(x @ w_in)
    u = cg * hc
    u_pad = jnp.concatenate([conv_prev, u], axis=1)
    z_conv = bg * _dwconv(u_pad, conv_w)
    new_conv = u_pad[:, -(CONV_W - 1):]
    k4 = k.reshape(Bn, T, N_KV_HEADS, HEAD_DIM)
    v4 = v.reshape(Bn, T, N_KV_HEADS, HEAD_DIM)
    if k_buf is None:
        z_attn = _swa_prompt(q, k4, v4, sinks)
        new_k, new_v = k4[:, -WINDOW:], v4[:, -WINDOW:]
    else:
        z_attn, new_k, new_v = _swa_sample(q, k4, v4, k_buf, v_buf, sinks)
    out = jnp.concatenate([z_conv, z_attn], axis=-1) @ w_out
    return out, new_conv, new_k, new_v


def _mem_kv(mem, w_mk, w_mv):
    Bn, M = mem.shape[0], mem.shape[1]
    mk = (mem @ w_mk).reshape(Bn, M, MEM_HEADS, MEM_HEAD_DIM)
    mv = (mem @ w_mv).reshape(Bn, M, MEM_HEADS, MEM_HEAD_DIM)
    return mk, mv


def _cross_attend(x, mk, mv, w_cq, w_co):
    Bn, T = x.shape[0], x.shape[1]
    q = (x @ w_cq).reshape(Bn, T, MEM_HEADS, MEM_HEAD_DIM)
    s = jnp.einsum('bthd,bmhd->bhtm', q.astype(jnp.float32), mk.astype(jnp.float32)) * (MEM_HEAD_DIM ** -0.5)
    p = jax.nn.softmax(s, axis=-1)
    o = jnp.einsum('bhtm,bmhd->bthd', p, mv.astype(jnp.float32)).astype(x.dtype)
    return o.reshape(Bn, T, MEM_INNER) @ w_co


def setup_inputs(seed: int = 0) -> dict:
    key = jax.random.key(seed)
    ks = jax.random.split(key, 24)
    f32 = jnp.float32
    nrm = lambda k, shape, scale: jax.random.normal(k, shape, f32) * scale
    w_in = nrm(ks[10], (DEPTH, D_MODEL, IN_COLS), D_MODEL ** -0.5)
    w_in = w_in.at[..., -D_KV:].multiply(BETA)
    return {
        'x_prompt': nrm(ks[0], (BATCH, SEQ, D_MODEL), 1.0),
        'x_sample': nrm(ks[1], (DEC_BATCH, DEC_SEQ, D_MODEL), 1.0),
        'mem_prompt': nrm(ks[2], (BATCH, MEM_LEN, D_MODEL), 1.0),
        'cache_win_k': nrm(ks[3], (DEPTH, DEC_BATCH, WINDOW, N_KV_HEADS, HEAD_DIM), 1.0),
        'cache_win_v': nrm(ks[4], (DEPTH, DEC_BATCH, WINDOW, N_KV_HEADS, HEAD_DIM), BETA),
        'state_conv': nrm(ks[5], (DEPTH, DEC_BATCH, CONV_W - 1, D_CONV), 1.0),
        'cache_mem_k': nrm(ks[6], (DEPTH, DEC_BATCH, MEM_LEN, MEM_HEADS, MEM_HEAD_DIM), 1.0),
        'cache_mem_v': nrm(ks[7], (DEPTH, DEC_BATCH, MEM_LEN, MEM_HEADS, MEM_HEAD_DIM), BETA),
        'ln_g': 1.0 + nrm(ks[8], (DEPTH, 4, D_MODEL), 0.02),
        'ln_b': nrm(ks[9], (DEPTH, 4, D_MODEL), 0.02),
        'ffn1_w_gu': nrm(ks[11], (DEPTH, D_MODEL, 2 * D_FF), D_MODEL ** -0.5),
        'ffn1_w_down': nrm(ks[12], (DEPTH, D_FF, D_MODEL), BETA * D_FF ** -0.5),
        'w_in': w_in,
        'conv_w': nrm(ks[13], (DEPTH, CONV_W, D_CONV), CONV_W ** -0.5),
        'attn_sinks': nrm(ks[14], (DEPTH, N_HEADS), 0.5),
        'w_out': nrm(ks[15], (DEPTH, MIX, D_MODEL), BETA * MIX ** -0.5),
        'w_cq': nrm(ks[16], (DEPTH, D_MODEL, MEM_INNER), D_MODEL ** -0.5),
        'w_mk': nrm(ks[17], (DEPTH, D_MODEL, MEM_INNER), D_MODEL ** -0.5),
        'w_mv': nrm(ks[18], (DEPTH, D_MODEL, MEM_INNER), BETA * D_MODEL ** -0.5),
        'w_co': nrm(ks[19], (DEPTH, MEM_INNER, D_MODEL), BETA * MEM_INNER ** -0.5),
        'ffn2_w_gu': nrm(ks[20], (DEPTH, D_MODEL, 2 * D_FF), D_MODEL ** -0.5),
        'ffn2_w_down': nrm(ks[21], (DEPTH, D_FF, D_MODEL), BETA * D_FF ** -0.5),
    }


def reference(x_prompt, x_sample, mem_prompt, cache_win_k, cache_win_v, state_conv, cache_mem_k, cache_mem_v,
              ln_g, ln_b, ffn1_w_gu, ffn1_w_down, w_in, conv_w, attn_sinks, w_out,
              w_cq, w_mk, w_mv, w_co, ffn2_w_gu, ffn2_w_down):
    yp, ys = x_prompt, x_sample
    wkp, wvp, cvp, mkp, mvp = [], [], [], [], []
    wks, wvs, cvs = [], [], []
    for l in range(DEPTH):
        yp = _deepnorm(yp, _swiglu_half(yp, ffn1_w_gu[l], ffn1_w_down[l]), ln_g[l, 0], ln_b[l, 0])
        ys = _deepnorm(ys, _swiglu_half(ys, ffn1_w_gu[l], ffn1_w_down[l]), ln_g[l, 0], ln_b[l, 0])
        conv0 = jnp.zeros((yp.shape[0], CONV_W - 1, D_CONV), yp.dtype)
        mix_p, cst_p, kp, vp = _token_mix(yp, w_in[l], conv_w[l], w_out[l], attn_sinks[l], conv0, None, None)
        mix_s, cst_s, ksn, vsn = _token_mix(ys, w_in[l], conv_w[l], w_out[l], attn_sinks[l],
                                            state_conv[l], cache_win_k[l], cache_win_v[l])
        yp = _deepnorm(yp, mix_p, ln_g[l, 1], ln_b[l, 1])
        ys = _deepnorm(ys, mix_s, ln_g[l, 1], ln_b[l, 1])
        wkp.append(kp); wvp.append(vp); cvp.append(cst_p)
        wks.append(ksn); wvs.append(vsn); cvs.append(cst_s)
        mk, mv = _mem_kv(mem_prompt, w_mk[l], w_mv[l])
        mkp.append(mk); mvp.append(mv)
        yp = _deepnorm(yp, _cross_attend(yp, mk, mv, w_cq[l], w_co[l]), ln_g[l, 2], ln_b[l, 2])
        ys = _deepnorm(ys, _cross_attend(ys, cache_mem_k[l], cache_mem_v[l], w_cq[l], w_co[l]), ln_g[l, 2], ln_b[l, 2])
        yp = _deepnorm(yp, _swiglu_half(yp, ffn2_w_gu[l], ffn2_w_down[l]), ln_g[l, 3], ln_b[l, 3])
        ys = _deepnorm(ys, _swiglu_half(ys, ffn2_w_gu[l], ffn2_w_down[l]), ln_g[l, 3], ln_b[l, 3])
    new_win_k_prompt = jnp.stack(wkp)
    new_win_v_prompt = jnp.stack(wvp)
    new_conv_prompt = jnp.stack(cvp)
    new_mem_k_prompt = jnp.stack(mkp)
    new_mem_v_prompt = jnp.stack(mvp)
    new_win_k_sample = jnp.stack(wks)
    new_win_v_sample = jnp.stack(wvs)
    new_conv_sample = jnp.stack(cvs)
    return (yp, ys, new_win_k_prompt, new_win_v_prompt, new_conv_prompt, new_mem_k_prompt, new_mem_v_prompt,
            new_win_k_sample, new_win_v_sample, new_conv_sample)
```

```python
import functools

import jax
import jax.numpy as jnp
from jax import lax
from jax.experimental import pallas as pl
from jax.experimental.pallas import tpu as pltpu

D_MODEL = 1024
DEPTH = 2
D_CONV = 512
CONV_W = 3
HEAD_DIM = 64
N_HEADS = 8
N_KV_HEADS = 2
GROUP = N_HEADS // N_KV_HEADS
WINDOW = 128
D_ATTN = N_HEADS * HEAD_DIM
D_KV = N_KV_HEADS * HEAD_DIM
IN_COLS = 3 * D_CONV + D_ATTN + 2 * D_KV
MEM_LEN = 256
MEM_HEADS = 4
MEM_HEAD_DIM = D_MODEL // MEM_HEADS
D_FF = 2816
ALPHA = (2.0 * DEPTH) ** 0.25
LN_EPS = 1e-5

_C_BG, _C_CG, _C_HC, _C_Q, _C_K, _C_V = 0, D_CONV, 2 * D_CONV, 3 * D_CONV, 3 * D_CONV + D_ATTN, 3 * D_CONV + D_ATTN + D_KV

LANES = 128
SUBLANES = 8
MXU_COLS = 256
VMEM_LIMIT_BYTES = 56 * 1024 * 1024

BF16 = jnp.bfloat16
F32 = jnp.float32


def _dot(a, b):
    return jnp.dot(a, b, preferred_element_type=F32)


def _dot_nt(a, b):
    return lax.dot_general(a, b, (((1,), (1,)), ((), ())), preferred_element_type=F32)


def _layer_norm(z, g, b):
    mu = jnp.mean(z, axis=-1, keepdims=True)
    d = z - mu
    var = jnp.mean(d * d, axis=-1, keepdims=True)
    return d * lax.rsqrt(var + LN_EPS) * g + b


def _resident(shape):
    zeros = (0,) * len(shape)
    return pl.BlockSpec(shape, lambda *_: zeros, pipeline_mode=pl.Buffered(1))


def _params(*sem):
    return pltpu.CompilerParams(dimension_semantics=sem, vmem_limit_bytes=VMEM_LIMIT_BYTES)


FF_CHUNK = MXU_COLS


def _ffn_kernel(x_ref, wgu_ref, wd_ref, g_ref, b_ref, o_ref, h_ref):
    x = x_ref[...]
    xb = x.astype(BF16)
    for c in range(D_FF // FF_CHUNK):
        lo = c * FF_CHUNK
        gate = _dot(xb, wgu_ref[:, lo:lo + FF_CHUNK])
        up = _dot(xb, wgu_ref[:, D_FF + lo:D_FF + lo + FF_CHUNK])
        h_ref[:, lo:lo + FF_CHUNK] = (gate * jax.nn.sigmoid(gate) * up).astype(BF16)
    y = _dot(h_ref[...], wd_ref[...])
    o_ref[...] = _layer_norm(ALPHA * x + 0.5 * y, g_ref[...], b_ref[...])


def _ffn(x, wgu, wd, g, b, tm):
    m = x.shape[0]
    return pl.pallas_call(
        _ffn_kernel,
        grid=(m // tm,),
        in_specs=[
            pl.BlockSpec((tm, D_MODEL), lambda i: (i, 0)),
            _resident((D_MODEL, 2 * D_FF)),
            _resident((D_FF, D_MODEL)),
            _resident((1, D_MODEL)),
            _resident((1, D_MODEL)),
        ],
        out_specs=pl.BlockSpec((tm, D_MODEL), lambda i: (i, 0)),
        out_shape=jax.ShapeDtypeStruct((m, D_MODEL), F32),
        scratch_shapes=[pltpu.VMEM((tm, D_FF), BF16)],
        compiler_params=_params("arbitrary"),
        name="ffn_ln",
    )(x, wgu, wd, g, b)


def _dup_halves(a):
    lane = lax.broadcasted_iota(jnp.int32, a.shape, 1)
    low = lane < HEAD_DIM
    rolled = pltpu.roll(a, HEAD_DIM, 1)
    return jnp.where(low, a, rolled), jnp.where(low, rolled, a)


def _sink_column(sink_ref, first_head, rows_per_head, n_heads):
    rows = n_heads * rows_per_head
    r = lax.broadcasted_iota(jnp.int32, (rows, 1), 0)
    col = jnp.full((rows, 1), sink_ref[first_head + n_heads - 1], F32)
    for g in range(n_heads - 2, -1, -1):
        col = jnp.where(r < (g + 1) * rows_per_head, sink_ref[first_head + g], col)
    return col


def _mix_prompt_kernel(sink_ref, x_ref, win_ref, cw_ref, wout_ref, g_ref, b_ref,
                       o_ref, nk_ref, nv_ref, nc_ref,
                       ubuf, kd0, kd1, vd0, vd1, zbuf, *, tm):
    i = pl.program_id(1)
    x = x_ref[0]
    xb = x.astype(BF16)

    @pl.when(i == 0)
    def _():
        ubuf[0:SUBLANES, :] = jnp.zeros((SUBLANES, D_CONV), F32)
        for r in (kd0, kd1, vd0, vd1):
            r[0:WINDOW, :] = jnp.zeros((WINDOW, LANES), BF16)

    u = _dot(xb, win_ref[:, _C_CG:_C_CG + D_CONV]) * _dot(xb, win_ref[:, _C_HC:_C_HC + D_CONV])
    ubuf[SUBLANES:SUBLANES + tm, :] = u
    conv = (cw_ref[0:1, :] * ubuf[SUBLANES - 2:SUBLANES - 2 + tm, :]
            + cw_ref[1:2, :] * ubuf[SUBLANES - 1:SUBLANES - 1 + tm, :]
            + cw_ref[2:3, :] * u)
    zbuf[:, 0:D_CONV] = (_dot(xb, win_ref[:, _C_BG:_C_BG + D_CONV]) * conv).astype(BF16)
    nc_ref[0] = u[tm - (CONV_W - 1):tm, :]
    ubuf[0:SUBLANES, :] = u[tm - SUBLANES:tm, :]

    q = _dot(xb, win_ref[:, _C_Q:_C_Q + D_ATTN]) * (HEAD_DIM ** -0.5)
    k = _dot(xb, win_ref[:, _C_K:_C_K + D_KV])
    v = _dot(xb, win_ref[:, _C_V:_C_V + D_KV])
    nk_ref[0] = k[tm - WINDOW:tm, :]
    nv_ref[0] = v[tm - WINDOW:tm, :]
    ka, kb = _dup_halves(k)
    va, vb = _dup_halves(v)
    kd0[WINDOW:WINDOW + tm, :] = ka.astype(BF16)
    kd1[WINDOW:WINDOW + tm, :] = kb.astype(BF16)
    vd0[WINDOW:WINDOW + tm, :] = va.astype(BF16)
    vd1[WINDOW:WINDOW + tm, :] = vb.astype(BF16)

    rows = GROUP * WINDOW
    a_idx = lax.broadcasted_iota(jnp.int32, (rows, 2 * WINDOW), 0) % WINDOW
    c_idx = lax.broadcasted_iota(jnp.int32, (rows, 2 * WINDOW), 1)
    band = (c_idx >= a_idx) & (c_idx <= a_idx + WINDOW)
    first_lb = jnp.where(i == 0, WINDOW, 0)
    low = lax.broadcasted_iota(jnp.int32, (WINDOW, LANES), 1) < HEAD_DIM

    for n in range(tm // WINDOW):
        r0 = n * WINDOW
        mask = (band & (c_idx >= first_lb)) if n == 0 else band
        for j, (kd, vd) in enumerate(((kd0, vd0), (kd1, vd1))):
            qa = q[r0:r0 + WINDOW, (2 * j) * LANES:(2 * j + 1) * LANES]
            qb = q[r0:r0 + WINDOW, (2 * j + 1) * LANES:(2 * j + 2) * LANES]
            zero = jnp.zeros_like(qa)
            qs = jnp.concatenate([jnp.where(low, qa, zero), jnp.where(low, zero, qa),
                                  jnp.where(low, qb, zero), jnp.where(low, zero, qb)], axis=0)
            s = _dot_nt(qs.astype(BF16), kd[r0:r0 + 2 * WINDOW, :])
            s = jnp.where(mask, s, -jnp.inf)
            sink = _sink_column(sink_ref, GROUP * j, WINDOW, GROUP)
            m = jnp.maximum(jnp.max(s, axis=-1, keepdims=True), sink)
            p = jnp.exp(s - m)
            denom = jnp.sum(p, axis=-1, keepdims=True) + jnp.exp(sink - m)
            o = _dot(p.astype(BF16), vd[r0:r0 + 2 * WINDOW, :]) / denom
            za = jnp.where(low, o[0:WINDOW], o[WINDOW:2 * WINDOW])
            zb = jnp.where(low, o[2 * WINDOW:3 * WINDOW], o[3 * WINDOW:4 * WINDOW])
            c0 = D_CONV + (2 * j) * LANES
            zbuf[r0:r0 + WINDOW, c0:c0 + LANES] = za.astype(BF16)
            zbuf[r0:r0 + WINDOW, c0 + LANES:c0 + 2 * LANES] = zb.astype(BF16)

    for r in (kd0, kd1, vd0, vd1):
        r[0:WINDOW, :] = r[tm:tm + WINDOW, :]

    y = _dot(zbuf[...], wout_ref[...])
    o_ref[0] = _layer_norm(ALPHA * x + y, g_ref[...], b_ref[...])


def _mix_prompt(x, sinks, win, cw, wout, g, b, tm):
    bsz, seq, _ = x.shape
    kern = functools.partial(_mix_prompt_kernel, tm=tm)
    grid_spec = pltpu.PrefetchScalarGridSpec(
        num_scalar_prefetch=1,
        grid=(bsz, seq // tm),
        in_specs=[
            pl.BlockSpec((1, tm, D_MODEL), lambda bi, i, s: (bi, i, 0)),
            _resident((D_MODEL, IN_COLS)),
            _resident((CONV_W, D_CONV)),
            _resident((D_CONV + D_ATTN, D_MODEL)),
            _resident((1, D_MODEL)),
            _resident((1, D_MODEL)),
        ],
        out_specs=[
            pl.BlockSpec((1, tm, D_MODEL), lambda bi, i, s: (bi, i, 0)),
            pl.BlockSpec((1, WINDOW, D_KV), lambda bi, i, s: (bi, 0, 0)),
            pl.BlockSpec((1, WINDOW, D_KV), lambda bi, i, s: (bi, 0, 0)),
            pl.BlockSpec((1, CONV_W - 1, D_CONV), lambda bi, i, s: (bi, 0, 0)),
        ],
        scratch_shapes=[
            pltpu.VMEM((tm + SUBLANES, D_CONV), F32),
            pltpu.VMEM((tm + WINDOW, LANES), BF16),
            pltpu.VMEM((tm + WINDOW, LANES), BF16),
            pltpu.VMEM((tm + WINDOW, LANES), BF16),
            pltpu.VMEM((tm + WINDOW, LANES), BF16),
            pltpu.VMEM((tm, D_CONV + D_ATTN), BF16),
        ],
    )
    return pl.pallas_call(
        kern,
        grid_spec=grid_spec,
        out_shape=[
            jax.ShapeDtypeStruct((bsz, seq, D_MODEL), F32),
            jax.ShapeDtypeStruct((bsz, WINDOW, D_KV), F32),
            jax.ShapeDtypeStruct((bsz, WINDOW, D_KV), F32),
            jax.ShapeDtypeStruct((bsz, CONV_W - 1, D_CONV), F32),
        ],
        compiler_params=_params("arbitrary", "arbitrary"),
        name="mix_prompt",
    )(sinks, x, win, cw, wout, g, b)


def _memkv_kernel(m_ref, wk_ref, wv_ref, k_ref, v_ref, kb_ref, vb_ref):
    mb = m_ref[...].astype(BF16)
    k = _dot(mb, wk_ref[...])
    v = _dot(mb, wv_ref[...])
    k_ref[...] = k
    v_ref[...] = v
    kb_ref[...] = k.astype(BF16)
    vb_ref[...] = v.astype(BF16)


def _memkv(mem, wk, wv):
    m = mem.shape[0]
    blk = pl.BlockSpec((MEM_LEN, D_MODEL), lambda i: (i, 0))
    return pl.pallas_call(
        _memkv_kernel,
        grid=(m // MEM_LEN,),
        in_specs=[blk, _resident((D_MODEL, D_MODEL)), _resident((D_MODEL, D_MODEL))],
        out_specs=[blk, blk, blk, blk],
        out_shape=[jax.ShapeDtypeStruct((m, D_MODEL), F32), jax.ShapeDtypeStruct((m, D_MODEL), F32),
                   jax.ShapeDtypeStruct((m, D_MODEL), BF16), jax.ShapeDtypeStruct((m, D_MODEL), BF16)],
        compiler_params=_params("arbitrary"),
        name="mem_kv",
    )(mem, wk, wv)


def _cross_prompt_kernel(x_ref, mk_ref, mv_ref, wq_ref, wo_ref, g_ref, b_ref, o_ref, obuf):
    x = x_ref[0]
    q = _dot(x.astype(BF16), wq_ref[...]) * (MEM_HEAD_DIM ** -0.5)
    for h in range(MEM_HEADS):
        c0 = h * MEM_HEAD_DIM
        s = _dot_nt(q[:, c0:c0 + MEM_HEAD_DIM].astype(BF16), mk_ref[0, :, c0:c0 + MEM_HEAD_DIM])
        m = jnp.max(s, axis=-1, keepdims=True)
        p = jnp.exp(s - m)
        denom = jnp.sum(p, axis=-1, keepdims=True)
        o = _dot(p.astype(BF16), mv_ref[0, :, c0:c0 + MEM_HEAD_DIM]) / denom
        obuf[:, c0:c0 + MEM_HEAD_DIM] = o.astype(BF16)
    y = _dot(obuf[...], wo_ref[...])
    o_ref[0] = _layer_norm(ALPHA * x + y, g_ref[...], b_ref[...])


def _cross_prompt(x, mk, mv, wq, wo, g, b, tm):
    bsz, seq, _ = x.shape
    xblk = pl.BlockSpec((1, tm, D_MODEL), lambda bi, i: (bi, i, 0))
    mblk = pl.BlockSpec((1, MEM_LEN, D_MODEL), lambda bi, i: (bi, 0, 0))
    return pl.pallas_call(
        _cross_prompt_kernel,
        grid=(bsz, seq // tm),
        in_specs=[xblk, mblk, mblk, _resident((D_MODEL, D_MODEL)), _resident((D_MODEL, D_MODEL)),
                  _resident((1, D_MODEL)), _resident((1, D_MODEL))],
        out_specs=xblk,
        out_shape=jax.ShapeDtypeStruct((bsz, seq, D_MODEL), F32),
        scratch_shapes=[pltpu.VMEM((tm, D_MODEL), BF16)],
        compiler_params=_params("arbitrary", "arbitrary"),
        name="cross_prompt",
    )(x, mk, mv, wq, wo, g, b)


def _mix_sample_kernel(sink_ref, x_ref, st_ref, kc_ref, vc_ref, win_ref, cw_ref, wout_ref, g_ref, b_ref,
                       o_ref, nst_ref, nk_ref, nv_ref, zbuf, *, bb):
    x = x_ref[...]
    xb = x.astype(BF16)

    u = _dot(xb, win_ref[:, _C_CG:_C_CG + D_CONV]) * _dot(xb, win_ref[:, _C_HC:_C_HC + D_CONV])
    st0 = st_ref[:, 0:D_CONV]
    st1 = st_ref[:, D_CONV:2 * D_CONV]
    conv = cw_ref[0:1, :] * st0 + cw_ref[1:2, :] * st1 + cw_ref[2:3, :] * u
    zbuf[:, 0:D_CONV] = (_dot(xb, win_ref[:, _C_BG:_C_BG + D_CONV]) * conv).astype(BF16)
    nst_ref[:, 0:D_CONV] = st1
    nst_ref[:, D_CONV:2 * D_CONV] = u

    q = _dot(xb, win_ref[:, _C_Q:_C_Q + D_ATTN]) * (HEAD_DIM ** -0.5)
    k = _dot(xb, win_ref[:, _C_K:_C_K + D_KV])
    v = _dot(xb, win_ref[:, _C_V:_C_V + D_KV])

    nkeys = bb * WINDOW
    nk_ref[0:nkeys - 1, :] = kc_ref[1:nkeys, :]
    nv_ref[0:nkeys - 1, :] = vc_ref[1:nkeys, :]
    nk_ref[pl.ds(WINDOW - 1, bb, stride=WINDOW), :] = k
    nv_ref[pl.ds(WINDOW - 1, bb, stride=WINDOW), :] = v

    lane = lax.broadcasted_iota(jnp.int32, (bb, LANES), 1)
    low = lane < HEAD_DIM
    qrows = []
    for h in range(N_HEADS):
        kvh = h // GROUP
        qt = q[:, (h // 2) * LANES:(h // 2 + 1) * LANES]
        if (h % 2) != kvh:
            qt = pltpu.roll(qt, HEAD_DIM, 1)
        keep = low if kvh == 0 else jnp.logical_not(low)
        qrows.append(jnp.where(keep, qt, 0.0))
    qm = jnp.concatenate(qrows, axis=0)
    rows = N_HEADS * bb

    s_all = _dot_nt(qm.astype(BF16), kc_ref[...].astype(BF16))
    own = (lax.broadcasted_iota(jnp.int32, (rows, nkeys), 0) % bb
           == lax.broadcasted_iota(jnp.int32, (rows, nkeys), 1) // WINDOW)
    s_own = jnp.where(own, s_all, 0.0)
    s = s_own[:, 0:WINDOW]
    for c in range(1, bb):
        s = s + s_own[:, c * WINDOW:(c + 1) * WINDOW]
    krep = jnp.concatenate([k] * N_HEADS, axis=0)
    vrep = jnp.concatenate([v] * N_HEADS, axis=0)
    s_new = jnp.sum(qm * krep, axis=-1, keepdims=True)

    sink = _sink_column(sink_ref, 0, bb, N_HEADS)
    m = jnp.maximum(jnp.maximum(jnp.max(s, axis=-1, keepdims=True), s_new), sink)
    p = jnp.exp(s - m)
    p_new = jnp.exp(s_new - m)
    denom = jnp.sum(p, axis=-1, keepdims=True) + p_new + jnp.exp(sink - m)
    p_all = jnp.where(own, jnp.concatenate([p] * bb, axis=1), 0.0)
    o = (_dot(p_all.astype(BF16), vc_ref[...].astype(BF16)) + p_new * vrep) / denom

    for t in range(D_ATTN // LANES):
        oa = o[(2 * t) * bb:(2 * t + 1) * bb]
        ob = o[(2 * t + 1) * bb:(2 * t + 2) * bb]
        if (2 * t) // GROUP == 0:
            z = jnp.where(low, oa, pltpu.roll(ob, HEAD_DIM, 1))
        else:
            z = jnp.where(low, pltpu.roll(oa, HEAD_DIM, 1), ob)
        zbuf[:, D_CONV + t * LANES:D_CONV + (t + 1) * LANES] = z.astype(BF16)

    y = _dot(zbuf[...], wout_ref[...])
    o_ref[...] = _layer_norm(ALPHA * x + y, g_ref[...], b_ref[...])


def _mix_sample(x, sinks, st, kc, vc, win, cw, wout, g, b, bb):
    nb = x.shape[0]
    kern = functools.partial(_mix_sample_kernel, bb=bb)
    row = lambda i, s: (i, 0)
    grid_spec = pltpu.PrefetchScalarGridSpec(
        num_scalar_prefetch=1,
        grid=(nb // bb,),
        in_specs=[
            pl.BlockSpec((bb, D_MODEL), row),
            pl.BlockSpec((bb, (CONV_W - 1) * D_CONV), row),
            pl.BlockSpec((bb * WINDOW, D_KV), row),
            pl.BlockSpec((bb * WINDOW, D_KV), row),
            _resident((D_MODEL, IN_COLS)),
            _resident((CONV_W, D_CONV)),
            _resident((D_CONV + D_ATTN, D_MODEL)),
            _resident((1, D_MODEL)),
            _resident((1, D_MODEL)),
        ],
        out_specs=[
            pl.BlockSpec((bb, D_MODEL), row),
            pl.BlockSpec((bb, (CONV_W - 1) * D_CONV), row),
            pl.BlockSpec((bb * WINDOW, D_KV), row),
            pl.BlockSpec((bb * WINDOW, D_KV), row),
        ],
        scratch_shapes=[pltpu.VMEM((bb, D_CONV + D_ATTN), BF16)],
    )
    return pl.pallas_call(
        kern,
        grid_spec=grid_spec,
        out_shape=[
            jax.ShapeDtypeStruct((nb, D_MODEL), F32),
            jax.ShapeDtypeStruct((nb, (CONV_W - 1) * D_CONV), F32),
            jax.ShapeDtypeStruct((nb * WINDOW, D_KV), F32),
            jax.ShapeDtypeStruct((nb * WINDOW, D_KV), F32),
        ],
        compiler_params=_params("arbitrary"),
        name="mix_sample",
    )(sinks, x, st, kc, vc, win, cw, wout, g, b)


def _cross_sample_kernel(x_ref, mk_ref, mv_ref, wq_ref, wo_ref, g_ref, b_ref, o_ref, obuf, *, bb):
    x = x_ref[...]
    q = _dot(x.astype(BF16), wq_ref[...]) * (MEM_HEAD_DIM ** -0.5)
    for r in range(bb):
        for h in range(MEM_HEADS):
            c0 = h * MEM_HEAD_DIM
            kh = mk_ref[0, r, :, h, :]
            s = jnp.sum(kh * q[r:r + 1, c0:c0 + MEM_HEAD_DIM], axis=-1, keepdims=True)
            m = jnp.max(s, axis=0, keepdims=True)
            p = jnp.exp(s - m)
            denom = jnp.sum(p, axis=0, keepdims=True)
            o = jnp.sum(p * mv_ref[0, r, :, h, :], axis=0, keepdims=True) / denom
            obuf[r:r + 1, c0:c0 + MEM_HEAD_DIM] = o
    y = _dot(obuf[...].astype(BF16), wo_ref[...])
    o_ref[...] = _layer_norm(ALPHA * x + y, g_ref[...], b_ref[...])


def _cross_sample(x, mk, mv, layer, wq, wo, g, b, bb):
    nb = x.shape[0]
    kern = functools.partial(_cross_sample_kernel, bb=bb)
    xblk = pl.BlockSpec((bb, D_MODEL), lambda i: (i, 0))
    mblk = pl.BlockSpec((1, bb, MEM_LEN, MEM_HEADS, MEM_HEAD_DIM), lambda i: (layer, i, 0, 0, 0))
    return pl.pallas_call(
        kern,
        grid=(nb // bb,),
        in_specs=[xblk, mblk, mblk, _resident((D_MODEL, D_MODEL)), _resident((D_MODEL, D_MODEL)),
                  _resident((1, D_MODEL)), _resident((1, D_MODEL))],
        out_specs=xblk,
        out_shape=jax.ShapeDtypeStruct((nb, D_MODEL), F32),
        scratch_shapes=[pltpu.VMEM((bb, D_MODEL), F32)],
        compiler_params=_params("arbitrary"),
        name="cross_sample",
    )(x, mk, mv, wq, wo, g, b)


TM_PROMPT = 512
BB_MIX_SAMPLE = 32
BB_CROSS_SAMPLE = 8


def kernel(x_prompt, x_sample, mem_prompt, cache_win_k, cache_win_v, state_conv, cache_mem_k, cache_mem_v,
           ln_g, ln_b, ffn1_w_gu, ffn1_w_down, w_in, conv_w, attn_sinks, w_out,
           w_cq, w_mk, w_mv, w_co, ffn2_w_gu, ffn2_w_down):
    bsz, seq, _ = x_prompt.shape
    nsmp = x_sample.shape[0]
    yp = x_prompt
    ys = x_sample.reshape(nsmp, D_MODEL)
    mem2 = mem_prompt.reshape(bsz * MEM_LEN, D_MODEL)
    bf = lambda w: w.astype(BF16)

    wkp, wvp, cvp, mkp, mvp, wks, wvs, cvs = [], [], [], [], [], [], [], []
    for l in range(DEPTH):
        g = lambda j: ln_g[l, j].reshape(1, D_MODEL)
        b = lambda j: ln_b[l, j].reshape(1, D_MODEL)
        wgu1, wd1 = bf(ffn1_w_gu[l]), bf(ffn1_w_down[l])
        wgu2, wd2 = bf(ffn2_w_gu[l]), bf(ffn2_w_down[l])
        win, wout = bf(w_in[l]), bf(w_out[l])
        wcq, wco, wmk, wmv = bf(w_cq[l]), bf(w_co[l]), bf(w_mk[l]), bf(w_mv[l])

        yp = _ffn(yp.reshape(bsz * seq, D_MODEL), wgu1, wd1, g(0), b(0), TM_PROMPT).reshape(bsz, seq, D_MODEL)
        ys = _ffn(ys, wgu1, wd1, g(0), b(0), nsmp)

        yp, nk, nv, ncv = _mix_prompt(yp, attn_sinks[l], win, conv_w[l], wout, g(1), b(1), TM_PROMPT)
        wkp.append(nk.reshape(bsz, WINDOW, N_KV_HEADS, HEAD_DIM))
        wvp.append(nv.reshape(bsz, WINDOW, N_KV_HEADS, HEAD_DIM))
        cvp.append(ncv)
        ys, nst, nks, nvs = _mix_sample(
            ys, attn_sinks[l], state_conv[l].reshape(nsmp, (CONV_W - 1) * D_CONV),
            cache_win_k[l].reshape(nsmp * WINDOW, D_KV), cache_win_v[l].reshape(nsmp * WINDOW, D_KV),
            win, conv_w[l], wout, g(1), b(1), BB_MIX_SAMPLE)
        wks.append(nks.reshape(nsmp, WINDOW, N_KV_HEADS, HEAD_DIM))
        wvs.append(nvs.reshape(nsmp, WINDOW, N_KV_HEADS, HEAD_DIM))
        cvs.append(nst.reshape(nsmp, CONV_W - 1, D_CONV))

        mk, mv, mkb, mvb = _memkv(mem2, wmk, wmv)
        mkp.append(mk.reshape(bsz, MEM_LEN, MEM_HEADS, MEM_HEAD_DIM))
        mvp.append(mv.reshape(bsz, MEM_LEN, MEM_HEADS, MEM_HEAD_DIM))
        yp = _cross_prompt(yp, mkb.reshape(bsz, MEM_LEN, D_MODEL), mvb.reshape(bsz, MEM_LEN, D_MODEL),
                           wcq, wco, g(2), b(2), TM_PROMPT)
        ys = _cross_sample(ys, cache_mem_k, cache_mem_v, l, wcq, wco, g(2), b(2), BB_CROSS_SAMPLE)

        yp = _ffn(yp.reshape(bsz * seq, D_MODEL), wgu2, wd2, g(3), b(3), TM_PROMPT).reshape(bsz, seq, D_MODEL)
        ys = _ffn(ys, wgu2, wd2, g(3), b(3), nsmp)

    return (yp, ys.reshape(nsmp, 1, D_MODEL),
            jnp.stack(wkp), jnp.stack(wvp), jnp.stack(cvp), jnp.stack(mkp), jnp.stack(mvp),
            jnp.stack(wks), jnp.stack(wvs), jnp.stack(cvs))
```

```python
import functools

import jax
import jax.numpy as jnp
from jax import lax
from jax.experimental import pallas as pl
from jax.experimental.pallas import tpu as pltpu

D_MODEL = 1024
DEPTH = 2
D_CONV = 512
CONV_W = 3
HEAD_DIM = 64
N_HEADS = 8
N_KV_HEADS = 2
GROUP = N_HEADS // N_KV_HEADS
WINDOW = 128
D_ATTN = N_HEADS * HEAD_DIM
D_KV = N_KV_HEADS * HEAD_DIM
IN_COLS = 3 * D_CONV + D_ATTN + 2 * D_KV
MEM_LEN = 256
MEM_HEADS = 4
MEM_HEAD_DIM = D_MODEL // MEM_HEADS
D_FF = 2816
N_LN = 4
ALPHA = (2.0 * DEPTH) ** 0.25
LN_EPS = 1e-5

_C_BG, _C_CG, _C_HC, _C_Q, _C_KV = 0, D_CONV, 2 * D_CONV, 3 * D_CONV, 3 * D_CONV + D_ATTN

LANES = 128
SUBLANES = 8
MXU_COLS = 256
VMEM_LIMIT_BYTES = 56 * 1024 * 1024

BF16 = jnp.bfloat16
F32 = jnp.float32


def _dot(a, b):
    return jnp.dot(a, b, preferred_element_type=F32)


def _dot_nt(a, b):
    return lax.dot_general(a, b, (((1,), (1,)), ((), ())), preferred_element_type=F32)


def _layer_norm(z, g, b):
    mu = jnp.mean(z, axis=-1, keepdims=True)
    d = z - mu
    var = jnp.mean(d * d, axis=-1, keepdims=True)
    return d * lax.rsqrt(var + LN_EPS) * g + b


def _resident(shape):
    zeros = (0,) * len(shape)
    return pl.BlockSpec(shape, lambda *_: zeros, pipeline_mode=pl.Buffered(1))


def _layer_resident(shape, layer):
    idx = (layer,) + (0,) * len(shape)
    return pl.BlockSpec((1,) + shape, lambda *_: idx, pipeline_mode=pl.Buffered(1))


def _ln_specs():
    return [_resident((DEPTH, N_LN, D_MODEL)), _resident((DEPTH, N_LN, D_MODEL))]


def _params(*sem):
    return pltpu.CompilerParams(dimension_semantics=sem, vmem_limit_bytes=VMEM_LIMIT_BYTES)


def _ffn_kernel(x_ref, wgu_ref, wd_ref, g_ref, b_ref, o_ref, h_ref, *, layer, ln):
    x = x_ref[...]
    xb = x.astype(BF16)
    for c in range(D_FF // MXU_COLS):
        lo = c * MXU_COLS
        gate = _dot(xb, wgu_ref[0, :, lo:lo + MXU_COLS])
        up = _dot(xb, wgu_ref[0, :, D_FF + lo:D_FF + lo + MXU_COLS])
        h_ref[:, lo:lo + MXU_COLS] = (gate * jax.nn.sigmoid(gate) * up).astype(BF16)
    y = _dot(h_ref[...], wd_ref[0])
    o_ref[...] = _layer_norm(ALPHA * x + 0.5 * y, g_ref[layer, ln:ln + 1, :], b_ref[layer, ln:ln + 1, :])


def _ffn(x, wgu, wd, g, b, layer, ln, tm):
    m = x.shape[0]
    return pl.pallas_call(
        functools.partial(_ffn_kernel, layer=layer, ln=ln),
        grid=(m // tm,),
        in_specs=[
            pl.BlockSpec((tm, D_MODEL), lambda i: (i, 0)),
            _layer_resident((D_MODEL, 2 * D_FF), layer),
            _layer_resident((D_FF, D_MODEL), layer),
        ] + _ln_specs(),
        out_specs=pl.BlockSpec((tm, D_MODEL), lambda i: (i, 0)),
        out_shape=jax.ShapeDtypeStruct((m, D_MODEL), F32),
        scratch_shapes=[pltpu.VMEM((tm, D_FF), BF16)],
        compiler_params=_params("arbitrary"),
        name="ffn_ln",
    )(x, wgu, wd, g, b)


def _dup_halves(a):
    lane = lax.broadcasted_iota(jnp.int32, a.shape, 1)
    low = lane < HEAD_DIM
    rolled = pltpu.roll(a, HEAD_DIM, 1)
    return jnp.where(low, a, rolled), jnp.where(low, rolled, a)


def _sink_column(sink_ref, layer, first_head, rows_per_head, n_heads):
    rows = n_heads * rows_per_head
    r = lax.broadcasted_iota(jnp.int32, (rows, 1), 0)
    col = jnp.full((rows, 1), sink_ref[layer, first_head + n_heads - 1], F32)
    for g in range(n_heads - 2, -1, -1):
        col = jnp.where(r < (g + 1) * rows_per_head, sink_ref[layer, first_head + g], col)
    return col


def _mix_prompt_kernel(sink_ref, x_ref, win_ref, cw_ref, wout_ref, g_ref, b_ref,
                       o_ref, nk_ref, nv_ref, nc_ref,
                       ubuf, qbuf, kd0, kd1, vd0, vd1, zbuf, *, tm, layer, ln):
    i = pl.program_id(1)
    x = x_ref[0]
    xb = x.astype(BF16)

    def proj(col, width):
        return _dot(xb, win_ref[0, :, col:col + width])

    @pl.when(i == 0)
    def _():
        ubuf[0:SUBLANES, :] = jnp.zeros((SUBLANES, D_CONV), F32)
        for r in (kd0, kd1, vd0, vd1):
            r[0:WINDOW, :] = jnp.zeros((WINDOW, LANES), BF16)

    u = proj(_C_CG, D_CONV) * proj(_C_HC, D_CONV)
    ubuf[SUBLANES:SUBLANES + tm, :] = u
    conv = (cw_ref[layer, 0:1, :] * ubuf[SUBLANES - 2:SUBLANES - 2 + tm, :]
            + cw_ref[layer, 1:2, :] * ubuf[SUBLANES - 1:SUBLANES - 1 + tm, :]
            + cw_ref[layer, 2:3, :] * u)
    zbuf[:, 0:D_CONV] = (proj(_C_BG, D_CONV) * conv).astype(BF16)
    nc_ref[0] = u[tm - (CONV_W - 1):tm, :]
    ubuf[0:SUBLANES, :] = u[tm - SUBLANES:tm, :]

    qbuf[...] = (proj(_C_Q, D_ATTN) * (HEAD_DIM ** -0.5)).astype(BF16)
    k = proj(_C_KV, D_KV)
    v = proj(_C_KV + D_KV, D_KV)
    nk_ref[0] = k[tm - WINDOW:tm, :]
    nv_ref[0] = v[tm - WINDOW:tm, :]
    ka, kb = _dup_halves(k)
    va, vb = _dup_halves(v)
    kd0[WINDOW:WINDOW + tm, :] = ka.astype(BF16)
    kd1[WINDOW:WINDOW + tm, :] = kb.astype(BF16)
    vd0[WINDOW:WINDOW + tm, :] = va.astype(BF16)
    vd1[WINDOW:WINDOW + tm, :] = vb.astype(BF16)

    rows = GROUP * WINDOW
    a_idx = lax.broadcasted_iota(jnp.int32, (rows, 2 * WINDOW), 0) % WINDOW
    c_idx = lax.broadcasted_iota(jnp.int32, (rows, 2 * WINDOW), 1)
    band = (c_idx >= a_idx) & (c_idx <= a_idx + WINDOW)
    first_lb = jnp.where(i == 0, WINDOW, 0)
    low = lax.broadcasted_iota(jnp.int32, (WINDOW, LANES), 1) < HEAD_DIM
    zero = jnp.zeros((WINDOW, LANES), BF16)

    for n in range(tm // WINDOW):
        r0 = n * WINDOW
        mask = (band & (c_idx >= first_lb)) if n == 0 else band
        for j, (kd, vd) in enumerate(((kd0, vd0), (kd1, vd1))):
            qa = qbuf[r0:r0 + WINDOW, (2 * j) * LANES:(2 * j + 1) * LANES]
            qb = qbuf[r0:r0 + WINDOW, (2 * j + 1) * LANES:(2 * j + 2) * LANES]
            qs = jnp.concatenate([jnp.where(low, qa, zero), jnp.where(low, zero, qa),
                                  jnp.where(low, qb, zero), jnp.where(low, zero, qb)], axis=0)
            s = _dot_nt(qs, kd[r0:r0 + 2 * WINDOW, :])
            s = jnp.where(mask, s, -jnp.inf)
            sink = _sink_column(sink_ref, layer, GROUP * j, WINDOW, GROUP)
            m = jnp.maximum(jnp.max(s, axis=-1, keepdims=True), sink)
            p = jnp.exp(s - m)
            denom = jnp.sum(p, axis=-1, keepdims=True) + jnp.exp(sink - m)
            o = _dot(p.astype(BF16), vd[r0:r0 + 2 * WINDOW, :]) / denom
            za = jnp.where(low, o[0:WINDOW], o[WINDOW:2 * WINDOW])
            zb = jnp.where(low, o[2 * WINDOW:3 * WINDOW], o[3 * WINDOW:4 * WINDOW])
            c0 = D_CONV + (2 * j) * LANES
            zbuf[r0:r0 + WINDOW, c0:c0 + LANES] = za.astype(BF16)
            zbuf[r0:r0 + WINDOW, c0 + LANES:c0 + 2 * LANES] = zb.astype(BF16)

    for r in (kd0, kd1, vd0, vd1):
        r[0:WINDOW, :] = r[tm:tm + WINDOW, :]

    y = _dot(zbuf[...], wout_ref[0])
    o_ref[0] = _layer_norm(ALPHA * x + y, g_ref[layer, ln:ln + 1, :], b_ref[layer, ln:ln + 1, :])


def _mix_prompt(x, sinks, win, cw, wout, g, b, layer, ln, tm):
    bsz, seq, _ = x.shape
    kern = functools.partial(_mix_prompt_kernel, tm=tm, layer=layer, ln=ln)
    grid_spec = pltpu.PrefetchScalarGridSpec(
        num_scalar_prefetch=1,
        grid=(bsz, seq // tm),
        in_specs=[
            pl.BlockSpec((1, tm, D_MODEL), lambda bi, i, s: (bi, i, 0)),
            _layer_resident((D_MODEL, IN_COLS), layer),
            _resident((DEPTH, CONV_W, D_CONV)),
            _layer_resident((D_CONV + D_ATTN, D_MODEL), layer),
        ] + _ln_specs(),
        out_specs=[
            pl.BlockSpec((1, tm, D_MODEL), lambda bi, i, s: (bi, i, 0)),
            pl.BlockSpec((1, WINDOW, D_KV), lambda bi, i, s: (bi, 0, 0)),
            pl.BlockSpec((1, WINDOW, D_KV), lambda bi, i, s: (bi, 0, 0)),
            pl.BlockSpec((1, CONV_W - 1, D_CONV), lambda bi, i, s: (bi, 0, 0)),
        ],
        scratch_shapes=[
            pltpu.VMEM((tm + SUBLANES, D_CONV), F32),
            pltpu.VMEM((tm, D_ATTN), BF16),
            pltpu.VMEM((tm + WINDOW, LANES), BF16),
            pltpu.VMEM((tm + WINDOW, LANES), BF16),
            pltpu.VMEM((tm + WINDOW, LANES), BF16),
            pltpu.VMEM((tm + WINDOW, LANES), BF16),
            pltpu.VMEM((tm, D_CONV + D_ATTN), BF16),
        ],
    )
    return pl.pallas_call(
        kern,
        grid_spec=grid_spec,
        out_shape=[
            jax.ShapeDtypeStruct((bsz, seq, D_MODEL), F32),
            jax.ShapeDtypeStruct((bsz, WINDOW, D_KV), F32),
            jax.ShapeDtypeStruct((bsz, WINDOW, D_KV), F32),
            jax.ShapeDtypeStruct((bsz, CONV_W - 1, D_CONV), F32),
        ],
        compiler_params=_params("arbitrary", "arbitrary"),
        name="mix_prompt",
    )(sinks, x, win, cw, wout, g, b)


def _memkv_kernel(m_ref, wk_ref, wv_ref, k_ref, v_ref, kb_ref, vb_ref):
    mb = m_ref[...].astype(BF16)
    k = _dot(mb, wk_ref[0])
    v = _dot(mb, wv_ref[0])
    k_ref[...] = k
    v_ref[...] = v
    kb_ref[...] = k.astype(BF16)
    vb_ref[...] = v.astype(BF16)


def _memkv(mem, wk, wv, layer):
    m = mem.shape[0]
    blk = pl.BlockSpec((MEM_LEN, D_MODEL), lambda i: (i, 0))
    return pl.pallas_call(
        _memkv_kernel,
        grid=(m // MEM_LEN,),
        in_specs=[blk, _layer_resident((D_MODEL, D_MODEL), layer), _layer_resident((D_MODEL, D_MODEL), layer)],
        out_specs=[blk, blk, blk, blk],
        out_shape=[jax.ShapeDtypeStruct((m, D_MODEL), F32), jax.ShapeDtypeStruct((m, D_MODEL), F32),
                   jax.ShapeDtypeStruct((m, D_MODEL), BF16), jax.ShapeDtypeStruct((m, D_MODEL), BF16)],
        compiler_params=_params("arbitrary"),
        name="mem_kv",
    )(mem, wk, wv)


def _cross_prompt_kernel(x_ref, mk_ref, mv_ref, wq_ref, wo_ref, g_ref, b_ref, o_ref, obuf, *, layer, ln):
    x = x_ref[0]
    q = _dot(x.astype(BF16), wq_ref[0]) * (MEM_HEAD_DIM ** -0.5)
    for h in range(MEM_HEADS):
        c0 = h * MEM_HEAD_DIM
        s = _dot_nt(q[:, c0:c0 + MEM_HEAD_DIM].astype(BF16), mk_ref[0, :, c0:c0 + MEM_HEAD_DIM])
        m = jnp.max(s, axis=-1, keepdims=True)
        p = jnp.exp(s - m)
        denom = jnp.sum(p, axis=-1, keepdims=True)
        o = _dot(p.astype(BF16), mv_ref[0, :, c0:c0 + MEM_HEAD_DIM]) / denom
        obuf[:, c0:c0 + MEM_HEAD_DIM] = o.astype(BF16)
    y = _dot(obuf[...], wo_ref[0])
    o_ref[0] = _layer_norm(ALPHA * x + y, g_ref[layer, ln:ln + 1, :], b_ref[layer, ln:ln + 1, :])


def _cross_prompt(x, mk, mv, wq, wo, g, b, layer, ln, tm):
    bsz, seq, _ = x.shape
    xblk = pl.BlockSpec((1, tm, D_MODEL), lambda bi, i: (bi, i, 0))
    mblk = pl.BlockSpec((1, MEM_LEN, D_MODEL), lambda bi, i: (bi, 0, 0))
    return pl.pallas_call(
        functools.partial(_cross_prompt_kernel, layer=layer, ln=ln),
        grid=(bsz, seq // tm),
        in_specs=[xblk, mblk, mblk, _layer_resident((D_MODEL, D_MODEL), layer),
                  _layer_resident((D_MODEL, D_MODEL), layer)] + _ln_specs(),
        out_specs=xblk,
        out_shape=jax.ShapeDtypeStruct((bsz, seq, D_MODEL), F32),
        scratch_shapes=[pltpu.VMEM((tm, D_MODEL), BF16)],
        compiler_params=_params("arbitrary", "arbitrary"),
        name="cross_prompt",
    )(x, mk, mv, wq, wo, g, b)


def _mix_sample_kernel(sink_ref, x_ref, st_ref, kc_ref, vc_ref, win_ref, cw_ref, wout_ref, g_ref, b_ref,
                       o_ref, nst_ref, nk_ref, nv_ref, zbuf, *, bb, layer, ln):
    x = x_ref[...]
    xb = x.astype(BF16)

    def proj(col, width):
        return _dot(xb, win_ref[0, :, col:col + width])

    u = proj(_C_CG, D_CONV) * proj(_C_HC, D_CONV)
    st0 = st_ref[:, 0:D_CONV]
    st1 = st_ref[:, D_CONV:2 * D_CONV]
    conv = cw_ref[layer, 0:1, :] * st0 + cw_ref[layer, 1:2, :] * st1 + cw_ref[layer, 2:3, :] * u
    zbuf[:, 0:D_CONV] = (proj(_C_BG, D_CONV) * conv).astype(BF16)
    nst_ref[:, 0:D_CONV] = st1
    nst_ref[:, D_CONV:2 * D_CONV] = u

    q = proj(_C_Q, D_ATTN) * (HEAD_DIM ** -0.5)
    k = proj(_C_KV, D_KV)
    v = proj(_C_KV + D_KV, D_KV)

    nkeys = bb * WINDOW
    nk_ref[0:nkeys - 1, :] = kc_ref[1:nkeys, :]
    nv_ref[0:nkeys - 1, :] = vc_ref[1:nkeys, :]
    nk_ref[pl.ds(WINDOW - 1, bb, stride=WINDOW), :] = k
    nv_ref[pl.ds(WINDOW - 1, bb, stride=WINDOW), :] = v

    lane = lax.broadcasted_iota(jnp.int32, (bb, LANES), 1)
    low = lane < HEAD_DIM
    qrows = []
    for h in range(N_HEADS):
        kvh = h // GROUP
        qt = q[:, (h // 2) * LANES:(h // 2 + 1) * LANES]
        if (h % 2) != kvh:
            qt = pltpu.roll(qt, HEAD_DIM, 1)
        keep = low if kvh == 0 else jnp.logical_not(low)
        qrows.append(jnp.where(keep, qt, 0.0))
    qm = jnp.concatenate(qrows, axis=0)
    rows = N_HEADS * bb

    s_all = _dot_nt(qm.astype(BF16), kc_ref[...].astype(BF16))
    own = (lax.broadcasted_iota(jnp.int32, (rows, nkeys), 0) % bb
           == lax.broadcasted_iota(jnp.int32, (rows, nkeys), 1) // WINDOW)
    s_own = jnp.where(own, s_all, 0.0)
    s = s_own[:, 0:WINDOW]
    for c in range(1, bb):
        s = s + s_own[:, c * WINDOW:(c + 1) * WINDOW]
    krep = jnp.concatenate([k] * N_HEADS, axis=0)
    vrep = jnp.concatenate([v] * N_HEADS, axis=0)
    s_new = jnp.sum(qm * krep, axis=-1, keepdims=True)

    sink = _sink_column(sink_ref, layer, 0, bb, N_HEADS)
    m = jnp.maximum(jnp.maximum(jnp.max(s, axis=-1, keepdims=True), s_new), sink)
    p = jnp.exp(s - m)
    p_new = jnp.exp(s_new - m)
    denom = jnp.sum(p, axis=-1, keepdims=True) + p_new + jnp.exp(sink - m)
    p_all = jnp.where(own, jnp.concatenate([p] * bb, axis=1), 0.0)
    o = (_dot(p_all.astype(BF16), vc_ref[...].astype(BF16)) + p_new * vrep) / denom

    for t in range(D_ATTN // LANES):
        oa = o[(2 * t) * bb:(2 * t + 1) * bb]
        ob = o[(2 * t + 1) * bb:(2 * t + 2) * bb]
        if (2 * t) // GROUP == 0:
            z = jnp.where(low, oa, pltpu.roll(ob, HEAD_DIM, 1))
        else:
            z = jnp.where(low, pltpu.roll(oa, HEAD_DIM, 1), ob)
        zbuf[:, D_CONV + t * LANES:D_CONV + (t + 1) * LANES] = z.astype(BF16)

    y = _dot(zbuf[...], wout_ref[0])
    o_ref[...] = _layer_norm(ALPHA * x + y, g_ref[layer, ln:ln + 1, :], b_ref[layer, ln:ln + 1, :])


def _mix_sample(x, sinks, st, kc, vc, win, cw, wout, g, b, layer, ln, bb):
    nb = x.shape[0]
    kern = functools.partial(_mix_sample_kernel, bb=bb, layer=layer, ln=ln)
    row = lambda i, s: (i, 0)
    grid_spec = pltpu.PrefetchScalarGridSpec(
        num_scalar_prefetch=1,
        grid=(nb // bb,),
        in_specs=[
            pl.BlockSpec((bb, D_MODEL), row),
            pl.BlockSpec((bb, (CONV_W - 1) * D_CONV), row),
            pl.BlockSpec((bb * WINDOW, D_KV), row),
            pl.BlockSpec((bb * WINDOW, D_KV), row),
            _layer_resident((D_MODEL, IN_COLS), layer),
            _resident((DEPTH, CONV_W, D_CONV)),
            _layer_resident((D_CONV + D_ATTN, D_MODEL), layer),
        ] + _ln_specs(),
        out_specs=[
            pl.BlockSpec((bb, D_MODEL), row),
            pl.BlockSpec((bb, (CONV_W - 1) * D_CONV), row),
            pl.BlockSpec((bb * WINDOW, D_KV), row),
            pl.BlockSpec((bb * WINDOW, D_KV), row),
        ],
        scratch_shapes=[pltpu.VMEM((bb, D_CONV + D_ATTN), BF16)],
    )
    return pl.pallas_call(
        kern,
        grid_spec=grid_spec,
        out_shape=[
            jax.ShapeDtypeStruct((nb, D_MODEL), F32),
            jax.ShapeDtypeStruct((nb, (CONV_W - 1) * D_CONV), F32),
            jax.ShapeDtypeStruct((nb * WINDOW, D_KV), F32),
            jax.ShapeDtypeStruct((nb * WINDOW, D_KV), F32),
        ],
        compiler_params=_params("arbitrary"),
        name="mix_sample",
    )(sinks, x, st, kc, vc, win, cw, wout, g, b)


_MEM_HALVES = MEM_HEAD_DIM // LANES
_MEM_ROWS = _MEM_HALVES * MEM_HEADS
_CROSS_SAMPLE_UNROLL = 4


def _cross_sample_kernel(x_ref, mk_ref, mv_ref, wq_ref, wo_ref, g_ref, b_ref, o_ref, q8, o8, obuf,
                         *, bb, layer, ln):
    x = x_ref[...]
    q = _dot(x.astype(BF16), wq_ref[0]) * (MEM_HEAD_DIM ** -0.5)
    for half in range(_MEM_HALVES):
        for h in range(MEM_HEADS):
            c0 = h * MEM_HEAD_DIM + half * LANES
            q8[:, half * MEM_HEADS + h, :] = q[:, c0:c0 + LANES]

    def one_sequence(r, carry):
        kq = mk_ref[0, r].reshape(MEM_LEN, _MEM_ROWS, LANES) * q8[r][None]
        kq = kq + pltpu.roll(kq, MEM_HEADS, 1)
        s = jnp.sum(kq, axis=-1, keepdims=True)
        m = jnp.max(s, axis=0, keepdims=True)
        p = jnp.exp(s - m)
        denom = jnp.sum(p, axis=0, keepdims=True)
        pv = p * mv_ref[0, r].reshape(MEM_LEN, _MEM_ROWS, LANES)
        o8[r] = jnp.sum(pv, axis=0) / denom[0]
        return carry

    lax.fori_loop(0, bb, one_sequence, 0, unroll=_CROSS_SAMPLE_UNROLL)
    for half in range(_MEM_HALVES):
        for h in range(MEM_HEADS):
            c0 = h * MEM_HEAD_DIM + half * LANES
            obuf[:, c0:c0 + LANES] = o8[:, half * MEM_HEADS + h, :].astype(BF16)
    y = _dot(obuf[...], wo_ref[0])
    o_ref[...] = _layer_norm(ALPHA * x + y, g_ref[layer, ln:ln + 1, :], b_ref[layer, ln:ln + 1, :])


def _cross_sample(x, mk, mv, wq, wo, g, b, layer, ln, bb):
    nb = x.shape[0]
    kern = functools.partial(_cross_sample_kernel, bb=bb, layer=layer, ln=ln)
    xblk = pl.BlockSpec((bb, D_MODEL), lambda i: (i, 0))
    mblk = pl.BlockSpec((1, bb, MEM_LEN * _MEM_ROWS, LANES), lambda i: (layer, i, 0, 0))
    return pl.pallas_call(
        kern,
        grid=(nb // bb,),
        in_specs=[xblk, mblk, mblk, _layer_resident((D_MODEL, D_MODEL), layer),
                  _layer_resident((D_MODEL, D_MODEL), layer)] + _ln_specs(),
        out_specs=xblk,
        out_shape=jax.ShapeDtypeStruct((nb, D_MODEL), F32),
        scratch_shapes=[pltpu.VMEM((bb, _MEM_ROWS, LANES), F32),
                        pltpu.VMEM((bb, _MEM_ROWS, LANES), F32),
                        pltpu.VMEM((bb, D_MODEL), BF16)],
        compiler_params=_params("arbitrary"),
        name="cross_sample",
    )(x, mk, mv, wq, wo, g, b)


TM_PROMPT = 512
BB_MIX_SAMPLE = 32
BB_CROSS_SAMPLE = 8


def kernel(x_prompt, x_sample, mem_prompt, cache_win_k, cache_win_v, state_conv, cache_mem_k, cache_mem_v,
           ln_g, ln_b, ffn1_w_gu, ffn1_w_down, w_in, conv_w, attn_sinks, w_out,
           w_cq, w_mk, w_mv, w_co, ffn2_w_gu, ffn2_w_down):
    bsz, seq, _ = x_prompt.shape
    nsmp = x_sample.shape[0]
    yp = x_prompt
    ys = x_sample.reshape(nsmp, D_MODEL)
    mem2 = mem_prompt.reshape(bsz * MEM_LEN, D_MODEL)
    bf = lambda w: w.astype(BF16)
    wgu1, wd1, wgu2, wd2 = bf(ffn1_w_gu), bf(ffn1_w_down), bf(ffn2_w_gu), bf(ffn2_w_down)
    win, wout, wcq, wco, wmk, wmv = bf(w_in), bf(w_out), bf(w_cq), bf(w_co), bf(w_mk), bf(w_mv)

    def token_tiles(c):
        c = c.reshape(DEPTH, nsmp, MEM_LEN, MEM_HEADS, _MEM_HALVES, LANES)
        return c.transpose(0, 1, 2, 4, 3, 5).reshape(DEPTH, nsmp, MEM_LEN * _MEM_ROWS, LANES)

    mem_k_tiles, mem_v_tiles = token_tiles(cache_mem_k), token_tiles(cache_mem_v)

    wkp, wvp, cvp, mkp, mvp, wks, wvs, cvs = [], [], [], [], [], [], [], []
    for l in range(DEPTH):
        yp = _ffn(yp.reshape(bsz * seq, D_MODEL), wgu1, wd1, ln_g, ln_b, l, 0, TM_PROMPT).reshape(bsz, seq, D_MODEL)
        ys = _ffn(ys, wgu1, wd1, ln_g, ln_b, l, 0, nsmp)

        yp, nk, nv, ncv = _mix_prompt(yp, attn_sinks, win, conv_w, wout, ln_g, ln_b, l, 1, TM_PROMPT)
        wkp.append(nk.reshape(bsz, WINDOW, N_KV_HEADS, HEAD_DIM))
        wvp.append(nv.reshape(bsz, WINDOW, N_KV_HEADS, HEAD_DIM))
        cvp.append(ncv)
        ys, nst, nks, nvs = _mix_sample(
            ys, attn_sinks, state_conv[l].reshape(nsmp, (CONV_W - 1) * D_CONV),
            cache_win_k[l].reshape(nsmp * WINDOW, D_KV), cache_win_v[l].reshape(nsmp * WINDOW, D_KV),
            win, conv_w, wout, ln_g, ln_b, l, 1, BB_MIX_SAMPLE)
        wks.append(nks.reshape(nsmp, WINDOW, N_KV_HEADS, HEAD_DIM))
        wvs.append(nvs.reshape(nsmp, WINDOW, N_KV_HEADS, HEAD_DIM))
        cvs.append(nst.reshape(nsmp, CONV_W - 1, D_CONV))

        mk, mv, mkb, mvb = _memkv(mem2, wmk, wmv, l)
        mkp.append(mk.reshape(bsz, MEM_LEN, MEM_HEADS, MEM_HEAD_DIM))
        mvp.append(mv.reshape(bsz, MEM_LEN, MEM_HEADS, MEM_HEAD_DIM))
        yp = _cross_prompt(yp, mkb.reshape(bsz, MEM_LEN, D_MODEL), mvb.reshape(bsz, MEM_LEN, D_MODEL),
                           wcq, wco, ln_g, ln_b, l, 2, TM_PROMPT)
        ys = _cross_sample(ys, mem_k_tiles, mem_v_tiles, wcq, wco, ln_g, ln_b, l, 2, BB_CROSS_SAMPLE)

        yp = _ffn(yp.reshape(bsz * seq, D_MODEL), wgu2, wd2, ln_g, ln_b, l, 3, TM_PROMPT).reshape(bsz, seq, D_MODEL)
        ys = _ffn(ys, wgu2, wd2, ln_g, ln_b, l, 3, nsmp)

    return (yp, ys.reshape(nsmp, 1, D_MODEL),
            jnp.stack(wkp), jnp.stack(wvp), jnp.stack(cvp), jnp.stack(mkp), jnp.stack(mvp),
            jnp.stack(wks), jnp.stack(wvs), jnp.stack(cvs))
```

```python
import functools

import jax
import jax.numpy as jnp
from jax import lax
from jax.experimental import pallas as pl
from jax.experimental.pallas import tpu as pltpu

D_MODEL = 1024
DEPTH = 2
D_CONV = 512
CONV_W = 3
HEAD_DIM = 64
N_HEADS = 8
N_KV_HEADS = 2
GROUP = N_HEADS // N_KV_HEADS
WINDOW = 128
D_ATTN = N_HEADS * HEAD_DIM
D_KV = N_KV_HEADS * HEAD_DIM
IN_COLS = 3 * D_CONV + D_ATTN + 2 * D_KV
MEM_LEN = 256
MEM_HEADS = 4
MEM_HEAD_DIM = D_MODEL // MEM_HEADS
D_FF = 2816
N_LN = 4
ALPHA = (2.0 * DEPTH) ** 0.25
LN_EPS = 1e-5

_C_BG, _C_CG, _C_HC, _C_Q, _C_KV = 0, D_CONV, 2 * D_CONV, 3 * D_CONV, 3 * D_CONV + D_ATTN

LANES = 128
SUBLANES = 8
MXU_COLS = 256
VMEM_LIMIT_BYTES = 56 * 1024 * 1024

BF16 = jnp.bfloat16
F32 = jnp.float32


def _dot(a, b):
    return jnp.dot(a, b, preferred_element_type=F32)


def _dot_nt(a, b):
    return lax.dot_general(a, b, (((1,), (1,)), ((), ())), preferred_element_type=F32)


def _layer_norm(z, g, b):
    mu = jnp.mean(z, axis=-1, keepdims=True)
    d = z - mu
    var = jnp.mean(d * d, axis=-1, keepdims=True)
    return d * lax.rsqrt(var + LN_EPS) * g + b


def _resident(shape):
    zeros = (0,) * len(shape)
    return pl.BlockSpec(shape, lambda *_: zeros, pipeline_mode=pl.Buffered(1))


def _layer_resident(shape, layer):
    idx = (layer,) + (0,) * len(shape)
    return pl.BlockSpec((1,) + shape, lambda *_: idx, pipeline_mode=pl.Buffered(1))


def _ln_specs():
    return [_resident((DEPTH, N_LN, D_MODEL)), _resident((DEPTH, N_LN, D_MODEL))]


def _params(*sem):
    return pltpu.CompilerParams(dimension_semantics=sem, vmem_limit_bytes=VMEM_LIMIT_BYTES)


def _ffn_kernel(x_ref, wgu_ref, wd_ref, g_ref, b_ref, o_ref, h_ref, *, layer, ln):
    x = x_ref[...]
    xb = x.astype(BF16)
    for c in range(D_FF // MXU_COLS):
        lo = c * MXU_COLS
        gate = _dot(xb, wgu_ref[0, :, lo:lo + MXU_COLS])
        up = _dot(xb, wgu_ref[0, :, D_FF + lo:D_FF + lo + MXU_COLS])
        h_ref[:, lo:lo + MXU_COLS] = (gate * jax.nn.sigmoid(gate) * up).astype(BF16)
    y = _dot(h_ref[...], wd_ref[0])
    o_ref[...] = _layer_norm(ALPHA * x + 0.5 * y, g_ref[layer, ln:ln + 1, :], b_ref[layer, ln:ln + 1, :])


def _ffn(x, wgu, wd, g, b, layer, ln, tm):
    m = x.shape[0]
    return pl.pallas_call(
        functools.partial(_ffn_kernel, layer=layer, ln=ln),
        grid=(m // tm,),
        in_specs=[
            pl.BlockSpec((tm, D_MODEL), lambda i: (i, 0)),
            _layer_resident((D_MODEL, 2 * D_FF), layer),
            _layer_resident((D_FF, D_MODEL), layer),
        ] + _ln_specs(),
        out_specs=pl.BlockSpec((tm, D_MODEL), lambda i: (i, 0)),
        out_shape=jax.ShapeDtypeStruct((m, D_MODEL), F32),
        scratch_shapes=[pltpu.VMEM((tm, D_FF), BF16)],
        compiler_params=_params("arbitrary"),
        name="ffn_ln",
    )(x, wgu, wd, g, b)


def _dup_halves(a):
    lane = lax.broadcasted_iota(jnp.int32, a.shape, 1)
    low = lane < HEAD_DIM
    rolled = pltpu.roll(a, HEAD_DIM, 1)
    return jnp.where(low, a, rolled), jnp.where(low, rolled, a)


def _sink_column(sink_ref, layer, first_head, rows_per_head, n_heads):
    rows = n_heads * rows_per_head
    r = lax.broadcasted_iota(jnp.int32, (rows, 1), 0)
    col = jnp.full((rows, 1), sink_ref[layer, first_head + n_heads - 1], F32)
    for g in range(n_heads - 2, -1, -1):
        col = jnp.where(r < (g + 1) * rows_per_head, sink_ref[layer, first_head + g], col)
    return col


def _mix_prompt_kernel(sink_ref, x_ref, win_ref, cw_ref, wout_ref, g_ref, b_ref,
                       o_ref, nk_ref, nv_ref, nc_ref,
                       ubuf, qbuf, kd0, kd1, vd0, vd1, zbuf, *, tm, layer, ln):
    i = pl.program_id(1)
    x = x_ref[0]
    xb = x.astype(BF16)

    def proj(col, width):
        return _dot(xb, win_ref[0, :, col:col + width])

    @pl.when(i == 0)
    def _():
        ubuf[0:SUBLANES, :] = jnp.zeros((SUBLANES, D_CONV), F32)
        for r in (kd0, kd1, vd0, vd1):
            r[0:WINDOW, :] = jnp.zeros((WINDOW, LANES), BF16)

    u = proj(_C_CG, D_CONV) * proj(_C_HC, D_CONV)
    ubuf[SUBLANES:SUBLANES + tm, :] = u
    conv = (cw_ref[layer, 0:1, :] * ubuf[SUBLANES - 2:SUBLANES - 2 + tm, :]
            + cw_ref[layer, 1:2, :] * ubuf[SUBLANES - 1:SUBLANES - 1 + tm, :]
            + cw_ref[layer, 2:3, :] * u)
    zbuf[:, 0:D_CONV] = (proj(_C_BG, D_CONV) * conv).astype(BF16)
    nc_ref[0] = u[tm - (CONV_W - 1):tm, :]
    ubuf[0:SUBLANES, :] = u[tm - SUBLANES:tm, :]

    qbuf[...] = (proj(_C_Q, D_ATTN) * (HEAD_DIM ** -0.5)).astype(BF16)
    k = proj(_C_KV, D_KV)
    v = proj(_C_KV + D_KV, D_KV)
    nk_ref[0] = k[tm - WINDOW:tm, :]
    nv_ref[0] = v[tm - WINDOW:tm, :]
    ka, kb = _dup_halves(k)
    va, vb = _dup_halves(v)
    kd0[WINDOW:WINDOW + tm, :] = ka.astype(BF16)
    kd1[WINDOW:WINDOW + tm, :] = kb.astype(BF16)
    vd0[WINDOW:WINDOW + tm, :] = va.astype(BF16)
    vd1[WINDOW:WINDOW + tm, :] = vb.astype(BF16)

    rows = GROUP * WINDOW
    a_idx = lax.broadcasted_iota(jnp.int32, (rows, 2 * WINDOW), 0) % WINDOW
    c_idx = lax.broadcasted_iota(jnp.int32, (rows, 2 * WINDOW), 1)
    band = (c_idx >= a_idx) & (c_idx <= a_idx + WINDOW)
    first_lb = jnp.where(i == 0, WINDOW, 0)
    low = lax.broadcasted_iota(jnp.int32, (WINDOW, LANES), 1) < HEAD_DIM
    zero = jnp.zeros((WINDOW, LANES), BF16)

    for n in range(tm // WINDOW):
        r0 = n * WINDOW
        mask = (band & (c_idx >= first_lb)) if n == 0 else band
        for j, (kd, vd) in enumerate(((kd0, vd0), (kd1, vd1))):
            qa = qbuf[r0:r0 + WINDOW, (2 * j) * LANES:(2 * j + 1) * LANES]
            qb = qbuf[r0:r0 + WINDOW, (2 * j + 1) * LANES:(2 * j + 2) * LANES]
            qs = jnp.concatenate([jnp.where(low, qa, zero), jnp.where(low, zero, qa),
                                  jnp.where(low, qb, zero), jnp.where(low, zero, qb)], axis=0)
            s = _dot_nt(qs, kd[r0:r0 + 2 * WINDOW, :])
            s = jnp.where(mask, s, -jnp.inf)
            sink = _sink_column(sink_ref, layer, GROUP * j, WINDOW, GROUP)
            m = jnp.maximum(jnp.max(s, axis=-1, keepdims=True), sink)
            p = jnp.exp(s - m)
            denom = jnp.sum(p, axis=-1, keepdims=True) + jnp.exp(sink - m)
            o = _dot(p.astype(BF16), vd[r0:r0 + 2 * WINDOW, :]) / denom
            za = jnp.where(low, o[0:WINDOW], o[WINDOW:2 * WINDOW])
            zb = jnp.where(low, o[2 * WINDOW:3 * WINDOW], o[3 * WINDOW:4 * WINDOW])
            c0 = D_CONV + (2 * j) * LANES
            zbuf[r0:r0 + WINDOW, c0:c0 + LANES] = za.astype(BF16)
            zbuf[r0:r0 + WINDOW, c0 + LANES:c0 + 2 * LANES] = zb.astype(BF16)

    for r in (kd0, kd1, vd0, vd1):
        r[0:WINDOW, :] = r[tm:tm + WINDOW, :]

    y = _dot(zbuf[...], wout_ref[0])
    o_ref[0] = _layer_norm(ALPHA * x + y, g_ref[layer, ln:ln + 1, :], b_ref[layer, ln:ln + 1, :])


def _mix_prompt(x, sinks, win, cw, wout, g, b, layer, ln, tm):
    bsz, seq, _ = x.shape
    kern = functools.partial(_mix_prompt_kernel, tm=tm, layer=layer, ln=ln)
    grid_spec = pltpu.PrefetchScalarGridSpec(
        num_scalar_prefetch=1,
        grid=(bsz, seq // tm),
        in_specs=[
            pl.BlockSpec((1, tm, D_MODEL), lambda bi, i, s: (bi, i, 0)),
            _layer_resident((D_MODEL, IN_COLS), layer),
            _resident((DEPTH, CONV_W, D_CONV)),
            _layer_resident((D_CONV + D_ATTN, D_MODEL), layer),
        ] + _ln_specs(),
        out_specs=[
            pl.BlockSpec((1, tm, D_MODEL), lambda bi, i, s: (bi, i, 0)),
            pl.BlockSpec((1, WINDOW, D_KV), lambda bi, i, s: (bi, 0, 0)),
            pl.BlockSpec((1, WINDOW, D_KV), lambda bi, i, s: (bi, 0, 0)),
            pl.BlockSpec((1, CONV_W - 1, D_CONV), lambda bi, i, s: (bi, 0, 0)),
        ],
        scratch_shapes=[
            pltpu.VMEM((tm + SUBLANES, D_CONV), F32),
            pltpu.VMEM((tm, D_ATTN), BF16),
            pltpu.VMEM((tm + WINDOW, LANES), BF16),
            pltpu.VMEM((tm + WINDOW, LANES), BF16),
            pltpu.VMEM((tm + WINDOW, LANES), BF16),
            pltpu.VMEM((tm + WINDOW, LANES), BF16),
            pltpu.VMEM((tm, D_CONV + D_ATTN), BF16),
        ],
    )
    return pl.pallas_call(
        kern,
        grid_spec=grid_spec,
        out_shape=[
            jax.ShapeDtypeStruct((bsz, seq, D_MODEL), F32),
            jax.ShapeDtypeStruct((bsz, WINDOW, D_KV), F32),
            jax.ShapeDtypeStruct((bsz, WINDOW, D_KV), F32),
            jax.ShapeDtypeStruct((bsz, CONV_W - 1, D_CONV), F32),
        ],
        compiler_params=_params("arbitrary", "arbitrary"),
        name="mix_prompt",
    )(sinks, x, win, cw, wout, g, b)


def _memkv_kernel(m_ref, wk_ref, wv_ref, k_ref, v_ref, kb_ref, vb_ref):
    mb = m_ref[...].astype(BF16)
    k = _dot(mb, wk_ref[0])
    v = _dot(mb, wv_ref[0])
    k_ref[...] = k
    v_ref[...] = v
    kb_ref[...] = k.astype(BF16)
    vb_ref[...] = v.astype(BF16)


def _memkv(mem, wk, wv, layer):
    m = mem.shape[0]
    blk = pl.BlockSpec((MEM_LEN, D_MODEL), lambda i: (i, 0))
    return pl.pallas_call(
        _memkv_kernel,
        grid=(m // MEM_LEN,),
        in_specs=[blk, _layer_resident((D_MODEL, D_MODEL), layer), _layer_resident((D_MODEL, D_MODEL), layer)],
        out_specs=[blk, blk, blk, blk],
        out_shape=[jax.ShapeDtypeStruct((m, D_MODEL), F32), jax.ShapeDtypeStruct((m, D_MODEL), F32),
                   jax.ShapeDtypeStruct((m, D_MODEL), BF16), jax.ShapeDtypeStruct((m, D_MODEL), BF16)],
        compiler_params=_params("arbitrary"),
        name="mem_kv",
    )(mem, wk, wv)


def _cross_prompt_kernel(x_ref, mk_ref, mv_ref, wq_ref, wo_ref, g_ref, b_ref, o_ref, obuf, *, layer, ln):
    x = x_ref[0]
    q = _dot(x.astype(BF16), wq_ref[0]) * (MEM_HEAD_DIM ** -0.5)
    for h in range(MEM_HEADS):
        c0 = h * MEM_HEAD_DIM
        s = _dot_nt(q[:, c0:c0 + MEM_HEAD_DIM].astype(BF16), mk_ref[0, :, c0:c0 + MEM_HEAD_DIM])
        m = jnp.max(s, axis=-1, keepdims=True)
        p = jnp.exp(s - m)
        denom = jnp.sum(p, axis=-1, keepdims=True)
        o = _dot(p.astype(BF16), mv_ref[0, :, c0:c0 + MEM_HEAD_DIM]) / denom
        obuf[:, c0:c0 + MEM_HEAD_DIM] = o.astype(BF16)
    y = _dot(obuf[...], wo_ref[0])
    o_ref[0] = _layer_norm(ALPHA * x + y, g_ref[layer, ln:ln + 1, :], b_ref[layer, ln:ln + 1, :])


def _cross_prompt(x, mk, mv, wq, wo, g, b, layer, ln, tm):
    bsz, seq, _ = x.shape
    xblk = pl.BlockSpec((1, tm, D_MODEL), lambda bi, i: (bi, i, 0))
    mblk = pl.BlockSpec((1, MEM_LEN, D_MODEL), lambda bi, i: (bi, 0, 0))
    return pl.pallas_call(
        functools.partial(_cross_prompt_kernel, layer=layer, ln=ln),
        grid=(bsz, seq // tm),
        in_specs=[xblk, mblk, mblk, _layer_resident((D_MODEL, D_MODEL), layer),
                  _layer_resident((D_MODEL, D_MODEL), layer)] + _ln_specs(),
        out_specs=xblk,
        out_shape=jax.ShapeDtypeStruct((bsz, seq, D_MODEL), F32),
        scratch_shapes=[pltpu.VMEM((tm, D_MODEL), BF16)],
        compiler_params=_params("arbitrary", "arbitrary"),
        name="cross_prompt",
    )(x, mk, mv, wq, wo, g, b)


def _bdot(a, b, contract_b):
    return lax.dot_general(a, b, (((2,), (contract_b,)), ((0,), (0,))), preferred_element_type=F32)


def _mix_sample_kernel(sink_ref, x_ref, st_ref, kc_ref, vc_ref, win_ref, cw_ref, wout_ref, g_ref, b_ref,
                       *rest, bb, layer, ln):
    o_ref, nst_ref, nk_ref, nv_ref, zbuf, qm_scr, kn_scr, vn_scr, o_scr = rest[-9:]
    x = x_ref[...]
    xb = x.astype(BF16)

    def proj(col, width):
        return _dot(xb, win_ref[0, :, col:col + width])

    u = proj(_C_CG, D_CONV) * proj(_C_HC, D_CONV)
    st0 = st_ref[:, 0:D_CONV]
    st1 = st_ref[:, D_CONV:2 * D_CONV]
    conv = cw_ref[layer, 0:1, :] * st0 + cw_ref[layer, 1:2, :] * st1 + cw_ref[layer, 2:3, :] * u
    zbuf[:, 0:D_CONV] = (proj(_C_BG, D_CONV) * conv).astype(BF16)
    nst_ref[:, 0:D_CONV] = st1
    nst_ref[:, D_CONV:2 * D_CONV] = u

    q = proj(_C_Q, D_ATTN) * (HEAD_DIM ** -0.5)
    k = proj(_C_KV, D_KV)
    v = proj(_C_KV + D_KV, D_KV)

    pad = jnp.zeros((LANES - bb, D_KV), F32)
    kt = jnp.concatenate([k, pad], axis=0).T if bb < LANES else k.T
    vt = jnp.concatenate([v, pad], axis=0).T if bb < LANES else v.T
    last = lax.broadcasted_iota(jnp.int32, (D_KV, WINDOW), 1) == WINDOW - 1
    for r in range(bb):
        nk_ref[0, r] = jnp.where(last, kt[:, r:r + 1], pltpu.roll(kc_ref[0, r], WINDOW - 1, 1))
        nv_ref[0, r] = jnp.where(last, vt[:, r:r + 1], pltpu.roll(vc_ref[0, r], WINDOW - 1, 1))

    lane = lax.broadcasted_iota(jnp.int32, (bb, LANES), 1)
    low = lane < HEAD_DIM
    for h in range(N_HEADS):
        kvh = h // GROUP
        qt = q[:, (h // 2) * LANES:(h // 2 + 1) * LANES]
        if (h % 2) != kvh:
            qt = pltpu.roll(qt, HEAD_DIM, 1)
        keep = low if kvh == 0 else jnp.logical_not(low)
        qm_scr[:, h, :] = jnp.where(keep, qt, 0.0)
        kn_scr[:, h, :] = k
        vn_scr[:, h, :] = v
    qm = qm_scr[...]

    s = _bdot(qm.astype(BF16), kc_ref[0].astype(BF16), 1)
    s_new = jnp.sum(qm * kn_scr[...], axis=-1, keepdims=True)
    hrow = lax.broadcasted_iota(jnp.int32, (1, N_HEADS, 1), 1)
    sink = jnp.full((1, N_HEADS, 1), sink_ref[layer, N_HEADS - 1], F32)
    for h in range(N_HEADS - 1):
        sink = jnp.where(hrow == h, sink_ref[layer, h], sink)
    m = jnp.maximum(jnp.maximum(jnp.max(s, axis=-1, keepdims=True), s_new), sink)
    p = jnp.exp(s - m)
    p_new = jnp.exp(s_new - m)
    denom = jnp.sum(p, axis=-1, keepdims=True) + p_new + jnp.exp(sink - m)
    o_scr[...] = (_bdot(p.astype(BF16), vc_ref[0].astype(BF16), 2) + p_new * vn_scr[...]) / denom

    for t in range(D_ATTN // LANES):
        oa = o_scr[:, 2 * t, :]
        ob = o_scr[:, 2 * t + 1, :]
        if (2 * t) // GROUP == 0:
            z = jnp.where(low, oa, pltpu.roll(ob, HEAD_DIM, 1))
        else:
            z = jnp.where(low, pltpu.roll(oa, HEAD_DIM, 1), ob)
        zbuf[:, D_CONV + t * LANES:D_CONV + (t + 1) * LANES] = z.astype(BF16)

    y = _dot(zbuf[...], wout_ref[0])
    o_ref[...] = _layer_norm(ALPHA * x + y, g_ref[layer, ln:ln + 1, :], b_ref[layer, ln:ln + 1, :])


def _mix_sample(x, sinks, st, kc, vc, prev_windows, win, cw, wout, g, b, layer, ln, bb):
    nb = x.shape[0]
    kern = functools.partial(_mix_sample_kernel, bb=bb, layer=layer, ln=ln)
    row = lambda i, s: (i, 0)
    win_blk = pl.BlockSpec((1, bb, D_KV, WINDOW), lambda i, s: (layer, i, 0, 0))
    in_specs = [
        pl.BlockSpec((bb, D_MODEL), row),
        pl.BlockSpec((bb, (CONV_W - 1) * D_CONV), row),
        win_blk,
        win_blk,
        _layer_resident((D_MODEL, IN_COLS), layer),
        _resident((DEPTH, CONV_W, D_CONV)),
        _layer_resident((D_CONV + D_ATTN, D_MODEL), layer),
    ] + _ln_specs()
    args = [sinks, x, st, kc, vc, win, cw, wout, g, b]
    aliases = {}
    if prev_windows is not None:
        n_in = len(args)
        in_specs += [pl.BlockSpec(memory_space=pl.ANY)] * 2
        args += list(prev_windows)
        aliases = {n_in: 2, n_in + 1: 3}
    grid_spec = pltpu.PrefetchScalarGridSpec(
        num_scalar_prefetch=1,
        grid=(nb // bb,),
        in_specs=in_specs,
        out_specs=[pl.BlockSpec((bb, D_MODEL), row), pl.BlockSpec((bb, (CONV_W - 1) * D_CONV), row), win_blk, win_blk],
        scratch_shapes=[pltpu.VMEM((bb, D_CONV + D_ATTN), BF16)]
        + [pltpu.VMEM((bb, N_HEADS, LANES), F32)] * 4,
    )
    return pl.pallas_call(
        kern,
        grid_spec=grid_spec,
        out_shape=[
            jax.ShapeDtypeStruct((nb, D_MODEL), F32),
            jax.ShapeDtypeStruct((nb, (CONV_W - 1) * D_CONV), F32),
            jax.ShapeDtypeStruct((DEPTH, nb, D_KV, WINDOW), F32),
            jax.ShapeDtypeStruct((DEPTH, nb, D_KV, WINDOW), F32),
        ],
        input_output_aliases=aliases,
        compiler_params=_params("arbitrary"),
        name="mix_sample",
    )(*args)


_MEM_HALVES = MEM_HEAD_DIM // LANES
_MEM_ROWS = _MEM_HALVES * MEM_HEADS
_CROSS_SAMPLE_UNROLL = 4


def _cross_sample_kernel(x_ref, mk_ref, mv_ref, wq_ref, wo_ref, g_ref, b_ref, o_ref, q8, o8, obuf,
                         *, bb, layer, ln):
    x = x_ref[...]
    q = _dot(x.astype(BF16), wq_ref[0]) * (MEM_HEAD_DIM ** -0.5)
    for half in range(_MEM_HALVES):
        for h in range(MEM_HEADS):
            c0 = h * MEM_HEAD_DIM + half * LANES
            q8[:, half * MEM_HEADS + h, :] = q[:, c0:c0 + LANES]

    def one_sequence(r, carry):
        kq = mk_ref[0, r].reshape(MEM_LEN, _MEM_ROWS, LANES) * q8[r][None]
        kq = kq + pltpu.roll(kq, MEM_HEADS, 1)
        s = jnp.sum(kq, axis=-1, keepdims=True)
        m = jnp.max(s, axis=0, keepdims=True)
        p = jnp.exp(s - m)
        denom = jnp.sum(p, axis=0, keepdims=True)
        pv = p * mv_ref[0, r].reshape(MEM_LEN, _MEM_ROWS, LANES)
        o8[r] = jnp.sum(pv, axis=0) / denom[0]
        return carry

    lax.fori_loop(0, bb, one_sequence, 0, unroll=_CROSS_SAMPLE_UNROLL)
    for half in range(_MEM_HALVES):
        for h in range(MEM_HEADS):
            c0 = h * MEM_HEAD_DIM + half * LANES
            obuf[:, c0:c0 + LANES] = o8[:, half * MEM_HEADS + h, :].astype(BF16)
    y = _dot(obuf[...], wo_ref[0])
    o_ref[...] = _layer_norm(ALPHA * x + y, g_ref[layer, ln:ln + 1, :], b_ref[layer, ln:ln + 1, :])


def _cross_sample(x, mk, mv, wq, wo, g, b, layer, ln, bb):
    nb = x.shape[0]
    kern = functools.partial(_cross_sample_kernel, bb=bb, layer=layer, ln=ln)
    xblk = pl.BlockSpec((bb, D_MODEL), lambda i: (i, 0))
    mblk = pl.BlockSpec((1, bb, MEM_LEN * _MEM_ROWS, LANES), lambda i: (layer, i, 0, 0))
    return pl.pallas_call(
        kern,
        grid=(nb // bb,),
        in_specs=[xblk, mblk, mblk, _layer_resident((D_MODEL, D_MODEL), layer),
                  _layer_resident((D_MODEL, D_MODEL), layer)] + _ln_specs(),
        out_specs=xblk,
        out_shape=jax.ShapeDtypeStruct((nb, D_MODEL), F32),
        scratch_shapes=[pltpu.VMEM((bb, _MEM_ROWS, LANES), F32),
                        pltpu.VMEM((bb, _MEM_ROWS, LANES), F32),
                        pltpu.VMEM((bb, D_MODEL), BF16)],
        compiler_params=_params("arbitrary"),
        name="cross_sample",
    )(x, mk, mv, wq, wo, g, b)


TM_PROMPT = 512
BB_MIX_SAMPLE = 32
BB_CROSS_SAMPLE = 8


def kernel(x_prompt, x_sample, mem_prompt, cache_win_k, cache_win_v, state_conv, cache_mem_k, cache_mem_v,
           ln_g, ln_b, ffn1_w_gu, ffn1_w_down, w_in, conv_w, attn_sinks, w_out,
           w_cq, w_mk, w_mv, w_co, ffn2_w_gu, ffn2_w_down):
    bsz, seq, _ = x_prompt.shape
    nsmp = x_sample.shape[0]
    yp = x_prompt
    ys = x_sample.reshape(nsmp, D_MODEL)
    mem2 = mem_prompt.reshape(bsz * MEM_LEN, D_MODEL)
    bf = lambda w: w.astype(BF16)
    wgu1, wd1, wgu2, wd2 = bf(ffn1_w_gu), bf(ffn1_w_down), bf(ffn2_w_gu), bf(ffn2_w_down)
    win, wout, wcq, wco, wmk, wmv = bf(w_in), bf(w_out), bf(w_cq), bf(w_co), bf(w_mk), bf(w_mv)

    def token_tiles(c):
        c = c.reshape(DEPTH, nsmp, MEM_LEN, MEM_HEADS, _MEM_HALVES, LANES)
        return c.transpose(0, 1, 2, 4, 3, 5).reshape(DEPTH, nsmp, MEM_LEN * _MEM_ROWS, LANES)

    mem_k_tiles, mem_v_tiles = token_tiles(cache_mem_k), token_tiles(cache_mem_v)

    def window_lanes(c):
        return c.transpose(0, 1, 3, 4, 2).reshape(DEPTH, nsmp, D_KV, WINDOW)

    win_k_lanes, win_v_lanes = window_lanes(cache_win_k), window_lanes(cache_win_v)
    new_windows = None

    wkp, wvp, cvp, mkp, mvp, cvs = [], [], [], [], [], []
    for l in range(DEPTH):
        yp = _ffn(yp.reshape(bsz * seq, D_MODEL), wgu1, wd1, ln_g, ln_b, l, 0, TM_PROMPT).reshape(bsz, seq, D_MODEL)
        ys = _ffn(ys, wgu1, wd1, ln_g, ln_b, l, 0, nsmp)

        yp, nk, nv, ncv = _mix_prompt(yp, attn_sinks, win, conv_w, wout, ln_g, ln_b, l, 1, TM_PROMPT)
        wkp.append(nk.reshape(bsz, WINDOW, N_KV_HEADS, HEAD_DIM))
        wvp.append(nv.reshape(bsz, WINDOW, N_KV_HEADS, HEAD_DIM))
        cvp.append(ncv)
        ys, nst, nks, nvs = _mix_sample(
            ys, attn_sinks, state_conv[l].reshape(nsmp, (CONV_W - 1) * D_CONV),
            win_k_lanes, win_v_lanes, new_windows, win, conv_w, wout, ln_g, ln_b, l, 1, BB_MIX_SAMPLE)
        new_windows = (nks, nvs)
        cvs.append(nst.reshape(nsmp, CONV_W - 1, D_CONV))

        mk, mv, mkb, mvb = _memkv(mem2, wmk, wmv, l)
        mkp.append(mk.reshape(bsz, MEM_LEN, MEM_HEADS, MEM_HEAD_DIM))
        mvp.append(mv.reshape(bsz, MEM_LEN, MEM_HEADS, MEM_HEAD_DIM))
        yp = _cross_prompt(yp, mkb.reshape(bsz, MEM_LEN, D_MODEL), mvb.reshape(bsz, MEM_LEN, D_MODEL),
                           wcq, wco, ln_g, ln_b, l, 2, TM_PROMPT)
        ys = _cross_sample(ys, mem_k_tiles, mem_v_tiles, wcq, wco, ln_g, ln_b, l, 2, BB_CROSS_SAMPLE)

        yp = _ffn(yp.reshape(bsz * seq, D_MODEL), wgu2, wd2, ln_g, ln_b, l, 3, TM_PROMPT).reshape(bsz, seq, D_MODEL)
        ys = _ffn(ys, wgu2, wd2, ln_g, ln_b, l, 3, nsmp)

    def window_rows(c):
        return c.reshape(DEPTH, nsmp, N_KV_HEADS, HEAD_DIM, WINDOW).transpose(0, 1, 4, 2, 3)

    return (yp, ys.reshape(nsmp, 1, D_MODEL),
            jnp.stack(wkp), jnp.stack(wvp), jnp.stack(cvp), jnp.stack(mkp), jnp.stack(mvp),
            window_rows(new_windows[0]), window_rows(new_windows[1]), jnp.stack(cvs))
```

```python
import functools

import jax
import jax.numpy as jnp
from jax import lax
from jax.experimental import pallas as pl
from jax.experimental.pallas import tpu as pltpu

D_MODEL = 1024
DEPTH = 2
D_CONV = 512
CONV_W = 3
HEAD_DIM = 64
N_HEADS = 8
N_KV_HEADS = 2
GROUP = N_HEADS // N_KV_HEADS
WINDOW = 128
D_ATTN = N_HEADS * HEAD_DIM
D_KV = N_KV_HEADS * HEAD_DIM
IN_COLS = 3 * D_CONV + D_ATTN + 2 * D_KV
MEM_LEN = 256
MEM_HEADS = 4
MEM_HEAD_DIM = D_MODEL // MEM_HEADS
D_FF = 2816
N_LN = 4
ALPHA = (2.0 * DEPTH) ** 0.25
LN_EPS = 1e-5

_C_BG, _C_CG, _C_HC, _C_Q, _C_KV = 0, D_CONV, 2 * D_CONV, 3 * D_CONV, 3 * D_CONV + D_ATTN

LANES = 128
SUBLANES = 8
MXU_COLS = 256
VMEM_LIMIT_BYTES = 56 * 1024 * 1024

BF16 = jnp.bfloat16
F32 = jnp.float32


def _dot(a, b):
    return jnp.dot(a, b, preferred_element_type=F32)


def _dot_nt(a, b):
    return lax.dot_general(a, b, (((1,), (1,)), ((), ())), preferred_element_type=F32)


def _layer_norm(z, g, b):
    mu = jnp.mean(z, axis=-1, keepdims=True)
    d = z - mu
    var = jnp.mean(d * d, axis=-1, keepdims=True)
    return d * lax.rsqrt(var + LN_EPS) * g + b


def _resident(shape):
    zeros = (0,) * len(shape)
    return pl.BlockSpec(shape, lambda *_: zeros, pipeline_mode=pl.Buffered(1))


def _layer_resident(shape, layer):
    idx = (layer,) + (0,) * len(shape)
    return pl.BlockSpec((1,) + shape, lambda *_: idx, pipeline_mode=pl.Buffered(1))


def _ln_specs():
    return [_resident((DEPTH, N_LN, D_MODEL)), _resident((DEPTH, N_LN, D_MODEL))]


def _params(*sem):
    return pltpu.CompilerParams(dimension_semantics=sem, vmem_limit_bytes=VMEM_LIMIT_BYTES)


NORM_ROWS = 128


def _project_and_norm(lhs_ref, w, x_rows, scale, g, b, write_rows, rows):
    for r0 in range(0, rows, min(NORM_ROWS, rows)):
        r1 = r0 + min(NORM_ROWS, rows)
        y = _dot(lhs_ref[r0:r1, :], w)
        write_rows(r0, r1, _layer_norm(ALPHA * x_rows(r0, r1) + scale * y, g, b))


def _ffn_kernel(x_ref, wgu_ref, wd_ref, g_ref, b_ref, o_ref, h_ref, *, layer, ln):
    xb = x_ref[...].astype(BF16)
    for c in range(D_FF // MXU_COLS):
        lo = c * MXU_COLS
        gate = _dot(xb, wgu_ref[0, :, lo:lo + MXU_COLS])
        up = _dot(xb, wgu_ref[0, :, D_FF + lo:D_FF + lo + MXU_COLS])
        h_ref[:, lo:lo + MXU_COLS] = (gate * jax.nn.sigmoid(gate) * up).astype(BF16)

    def write_rows(r0, r1, y):
        o_ref[r0:r1, :] = y

    _project_and_norm(h_ref, wd_ref[0], lambda r0, r1: x_ref[r0:r1, :], 0.5,
                      g_ref[layer, ln:ln + 1, :], b_ref[layer, ln:ln + 1, :], write_rows, x_ref.shape[0])


def _ffn(x, wgu, wd, g, b, layer, ln, tm):
    m = x.shape[0]
    return pl.pallas_call(
        functools.partial(_ffn_kernel, layer=layer, ln=ln),
        grid=(m // tm,),
        in_specs=[
            pl.BlockSpec((tm, D_MODEL), lambda i: (i, 0)),
            _layer_resident((D_MODEL, 2 * D_FF), layer),
            _layer_resident((D_FF, D_MODEL), layer),
        ] + _ln_specs(),
        out_specs=pl.BlockSpec((tm, D_MODEL), lambda i: (i, 0)),
        out_shape=jax.ShapeDtypeStruct((m, D_MODEL), F32),
        scratch_shapes=[pltpu.VMEM((tm, D_FF), BF16)],
        compiler_params=_params("arbitrary"),
        name="ffn_ln",
    )(x, wgu, wd, g, b)


def _dup_halves(a):
    lane = lax.broadcasted_iota(jnp.int32, a.shape, 1)
    low = lane < HEAD_DIM
    rolled = pltpu.roll(a, HEAD_DIM, 1)
    return jnp.where(low, a, rolled), jnp.where(low, rolled, a)


def _sink_column(sink_ref, layer, first_head, rows_per_head, n_heads):
    rows = n_heads * rows_per_head
    r = lax.broadcasted_iota(jnp.int32, (rows, 1), 0)
    col = jnp.full((rows, 1), sink_ref[layer, first_head + n_heads - 1], F32)
    for g in range(n_heads - 2, -1, -1):
        col = jnp.where(r < (g + 1) * rows_per_head, sink_ref[layer, first_head + g], col)
    return col


def _mix_prompt_kernel(sink_ref, x_ref, win_ref, cw_ref, wout_ref, g_ref, b_ref,
                       o_ref, nk_ref, nv_ref, nc_ref,
                       ubuf, qbuf, kd0, kd1, vd0, vd1, zbuf, *, tm, layer, ln):
    seq_start = pl.program_id(1) == 0
    xb = x_ref[0].astype(BF16)

    def proj(col, width):
        return _dot(xb, win_ref[0, :, col:col + width])

    @pl.when(seq_start)
    def _():
        ubuf[0:SUBLANES, :] = jnp.zeros((SUBLANES, D_CONV), F32)
        for r in (kd0, kd1, vd0, vd1):
            r[0:WINDOW, :] = jnp.zeros((WINDOW, LANES), BF16)

    u = proj(_C_CG, D_CONV) * proj(_C_HC, D_CONV)
    ubuf[SUBLANES:SUBLANES + tm, :] = u
    conv = (cw_ref[layer, 0:1, :] * ubuf[SUBLANES - 2:SUBLANES - 2 + tm, :]
            + cw_ref[layer, 1:2, :] * ubuf[SUBLANES - 1:SUBLANES - 1 + tm, :]
            + cw_ref[layer, 2:3, :] * u)
    zbuf[:, 0:D_CONV] = (proj(_C_BG, D_CONV) * conv).astype(BF16)
    nc_ref[0] = u[tm - (CONV_W - 1):tm, :]
    ubuf[0:SUBLANES, :] = u[tm - SUBLANES:tm, :]

    qbuf[...] = (proj(_C_Q, D_ATTN) * (HEAD_DIM ** -0.5)).astype(BF16)
    k = proj(_C_KV, D_KV)
    v = proj(_C_KV + D_KV, D_KV)
    nk_ref[0] = k[tm - WINDOW:tm, :]
    nv_ref[0] = v[tm - WINDOW:tm, :]
    ka, kb = _dup_halves(k)
    va, vb = _dup_halves(v)
    kd0[WINDOW:WINDOW + tm, :] = ka.astype(BF16)
    kd1[WINDOW:WINDOW + tm, :] = kb.astype(BF16)
    vd0[WINDOW:WINDOW + tm, :] = va.astype(BF16)
    vd1[WINDOW:WINDOW + tm, :] = vb.astype(BF16)

    rows = GROUP * WINDOW
    a_idx = lax.broadcasted_iota(jnp.int32, (rows, 2 * WINDOW), 0) % WINDOW
    c_idx = lax.broadcasted_iota(jnp.int32, (rows, 2 * WINDOW), 1)
    band = (c_idx >= a_idx) & (c_idx <= a_idx + WINDOW)
    first_lb = jnp.where(seq_start, WINDOW, 0)
    low = lax.broadcasted_iota(jnp.int32, (WINDOW, LANES), 1) < HEAD_DIM
    zero = jnp.zeros((WINDOW, LANES), BF16)

    for n in range(tm // WINDOW):
        r0 = n * WINDOW
        mask = (band & (c_idx >= first_lb)) if n == 0 else band
        for j, (kd, vd) in enumerate(((kd0, vd0), (kd1, vd1))):
            qa = qbuf[r0:r0 + WINDOW, (2 * j) * LANES:(2 * j + 1) * LANES]
            qb = qbuf[r0:r0 + WINDOW, (2 * j + 1) * LANES:(2 * j + 2) * LANES]
            qs = jnp.concatenate([jnp.where(low, qa, zero), jnp.where(low, zero, qa),
                                  jnp.where(low, qb, zero), jnp.where(low, zero, qb)], axis=0)
            s = _dot_nt(qs, kd[r0:r0 + 2 * WINDOW, :])
            s = jnp.where(mask, s, -jnp.inf)
            sink = _sink_column(sink_ref, layer, GROUP * j, WINDOW, GROUP)
            m = jnp.maximum(jnp.max(s, axis=-1, keepdims=True), sink)
            p = jnp.exp(s - m)
            denom = jnp.sum(p, axis=-1, keepdims=True) + jnp.exp(sink - m)
            o = _dot(p.astype(BF16), vd[r0:r0 + 2 * WINDOW, :]) / denom
            za = jnp.where(low, o[0:WINDOW], o[WINDOW:2 * WINDOW])
            zb = jnp.where(low, o[2 * WINDOW:3 * WINDOW], o[3 * WINDOW:4 * WINDOW])
            c0 = D_CONV + (2 * j) * LANES
            zbuf[r0:r0 + WINDOW, c0:c0 + LANES] = za.astype(BF16)
            zbuf[r0:r0 + WINDOW, c0 + LANES:c0 + 2 * LANES] = zb.astype(BF16)

    for r in (kd0, kd1, vd0, vd1):
        r[0:WINDOW, :] = r[tm:tm + WINDOW, :]

    def write_rows(r0, r1, y):
        o_ref[0, r0:r1, :] = y

    _project_and_norm(zbuf, wout_ref[0], lambda r0, r1: x_ref[0, r0:r1, :], 1.0,
                      g_ref[layer, ln:ln + 1, :], b_ref[layer, ln:ln + 1, :], write_rows, tm)


def _mix_prompt(x, sinks, win, cw, wout, g, b, layer, ln, tm):
    bsz, seq, _ = x.shape
    kern = functools.partial(_mix_prompt_kernel, tm=tm, layer=layer, ln=ln)
    per_seq = lambda bi, i, s: (bi, 0, 0)
    grid_spec = pltpu.PrefetchScalarGridSpec(
        num_scalar_prefetch=1,
        grid=(bsz, seq // tm),
        in_specs=[
            pl.BlockSpec((1, tm, D_MODEL), lambda bi, i, s: (bi, i, 0)),
            _layer_resident((D_MODEL, IN_COLS), layer),
            _resident((DEPTH, CONV_W, D_CONV)),
            _layer_resident((D_CONV + D_ATTN, D_MODEL), layer),
        ] + _ln_specs(),
        out_specs=[
            pl.BlockSpec((1, tm, D_MODEL), lambda bi, i, s: (bi, i, 0)),
            pl.BlockSpec((1, WINDOW, D_KV), per_seq),
            pl.BlockSpec((1, WINDOW, D_KV), per_seq),
            pl.BlockSpec((1, CONV_W - 1, D_CONV), per_seq),
        ],
        scratch_shapes=[
            pltpu.VMEM((tm + SUBLANES, D_CONV), F32),
            pltpu.VMEM((tm, D_ATTN), BF16),
            pltpu.VMEM((tm + WINDOW, LANES), BF16),
            pltpu.VMEM((tm + WINDOW, LANES), BF16),
            pltpu.VMEM((tm + WINDOW, LANES), BF16),
            pltpu.VMEM((tm + WINDOW, LANES), BF16),
            pltpu.VMEM((tm, D_CONV + D_ATTN), BF16),
        ],
    )
    return pl.pallas_call(
        kern,
        grid_spec=grid_spec,
        out_shape=[
            jax.ShapeDtypeStruct((bsz, seq, D_MODEL), F32),
            jax.ShapeDtypeStruct((bsz, WINDOW, D_KV), F32),
            jax.ShapeDtypeStruct((bsz, WINDOW, D_KV), F32),
            jax.ShapeDtypeStruct((bsz, CONV_W - 1, D_CONV), F32),
        ],
        compiler_params=_params("arbitrary", "arbitrary"),
        name="mix_prompt",
    )(sinks, x, win, cw, wout, g, b)


def _memkv_kernel(m_ref, wk_ref, wv_ref, k_ref, v_ref, kb_ref, vb_ref):
    mb = m_ref[...].astype(BF16)
    k = _dot(mb, wk_ref[0])
    v = _dot(mb, wv_ref[0])
    k_ref[...] = k
    v_ref[...] = v
    kb_ref[...] = k.astype(BF16)
    vb_ref[...] = v.astype(BF16)


def _memkv(mem, wk, wv, layer):
    m = mem.shape[0]
    blk = pl.BlockSpec((MEM_LEN, D_MODEL), lambda i: (i, 0))
    return pl.pallas_call(
        _memkv_kernel,
        grid=(m // MEM_LEN,),
        in_specs=[blk, _layer_resident((D_MODEL, D_MODEL), layer), _layer_resident((D_MODEL, D_MODEL), layer)],
        out_specs=[blk, blk, blk, blk],
        out_shape=[jax.ShapeDtypeStruct((m, D_MODEL), F32), jax.ShapeDtypeStruct((m, D_MODEL), F32),
                   jax.ShapeDtypeStruct((m, D_MODEL), BF16), jax.ShapeDtypeStruct((m, D_MODEL), BF16)],
        compiler_params=_params("arbitrary"),
        name="mem_kv",
    )(mem, wk, wv)


def _cross_prompt_kernel(x_ref, mk_ref, mv_ref, wq_ref, wo_ref, g_ref, b_ref, o_ref, obuf, *, layer, ln):
    q = _dot(x_ref[0].astype(BF16), wq_ref[0]) * (MEM_HEAD_DIM ** -0.5)
    for h in range(MEM_HEADS):
        c0 = h * MEM_HEAD_DIM
        s = _dot_nt(q[:, c0:c0 + MEM_HEAD_DIM].astype(BF16), mk_ref[0, :, c0:c0 + MEM_HEAD_DIM])
        m = jnp.max(s, axis=-1, keepdims=True)
        p = jnp.exp(s - m)
        denom = jnp.sum(p, axis=-1, keepdims=True)
        o = _dot(p.astype(BF16), mv_ref[0, :, c0:c0 + MEM_HEAD_DIM]) / denom
        obuf[:, c0:c0 + MEM_HEAD_DIM] = o.astype(BF16)

    def write_rows(r0, r1, y):
        o_ref[0, r0:r1, :] = y

    _project_and_norm(obuf, wo_ref[0], lambda r0, r1: x_ref[0, r0:r1, :], 1.0,
                      g_ref[layer, ln:ln + 1, :], b_ref[layer, ln:ln + 1, :], write_rows, x_ref.shape[1])


def _cross_prompt(x, mk, mv, wq, wo, g, b, layer, ln, tm):
    bsz, seq, _ = x.shape
    xblk = pl.BlockSpec((1, tm, D_MODEL), lambda bi, i: (bi, i, 0))
    mblk = pl.BlockSpec((1, MEM_LEN, D_MODEL), lambda bi, i: (bi, 0, 0))
    return pl.pallas_call(
        functools.partial(_cross_prompt_kernel, layer=layer, ln=ln),
        grid=(bsz, seq // tm),
        in_specs=[xblk, mblk, mblk, _layer_resident((D_MODEL, D_MODEL), layer),
                  _layer_resident((D_MODEL, D_MODEL), layer)] + _ln_specs(),
        out_specs=xblk,
        out_shape=jax.ShapeDtypeStruct((bsz, seq, D_MODEL), F32),
        scratch_shapes=[pltpu.VMEM((tm, D_MODEL), BF16)],
        compiler_params=_params("arbitrary", "arbitrary"),
        name="cross_prompt",
    )(x, mk, mv, wq, wo, g, b)


def _bdot(a, b, contract_b):
    return lax.dot_general(a, b, (((2,), (contract_b,)), ((0,), (0,))), preferred_element_type=F32)


def _mix_sample_kernel(sink_ref, x_ref, st_ref, kc_ref, vc_ref, win_ref, cw_ref, wout_ref, g_ref, b_ref,
                       *rest, bb, layer, ln):
    o_ref, nst_ref, nk_ref, nv_ref, zbuf, qm_scr, kn_scr, vn_scr, o_scr = rest[-9:]
    x = x_ref[...]
    xb = x.astype(BF16)

    def proj(col, width):
        return _dot(xb, win_ref[0, :, col:col + width])

    u = proj(_C_CG, D_CONV) * proj(_C_HC, D_CONV)
    st0 = st_ref[:, 0:D_CONV]
    st1 = st_ref[:, D_CONV:2 * D_CONV]
    conv = cw_ref[layer, 0:1, :] * st0 + cw_ref[layer, 1:2, :] * st1 + cw_ref[layer, 2:3, :] * u
    zbuf[:, 0:D_CONV] = (proj(_C_BG, D_CONV) * conv).astype(BF16)
    nst_ref[:, 0:D_CONV] = st1
    nst_ref[:, D_CONV:2 * D_CONV] = u

    q = proj(_C_Q, D_ATTN) * (HEAD_DIM ** -0.5)
    k = proj(_C_KV, D_KV)
    v = proj(_C_KV + D_KV, D_KV)

    pad = jnp.zeros((LANES - bb, D_KV), F32)
    kt = jnp.concatenate([k, pad], axis=0).T if bb < LANES else k.T
    vt = jnp.concatenate([v, pad], axis=0).T if bb < LANES else v.T
    last = lax.broadcasted_iota(jnp.int32, (D_KV, WINDOW), 1) == WINDOW - 1
    slab = layer if nk_ref.shape[0] > 1 else 0
    for other in range(nk_ref.shape[0]):
        if other != slab:
            nk_ref[other] = jnp.zeros(nk_ref.shape[1:], F32)
            nv_ref[other] = jnp.zeros(nv_ref.shape[1:], F32)
    for r in range(bb):
        nk_ref[slab, r] = jnp.where(last, kt[:, r:r + 1], pltpu.roll(kc_ref[0, r], WINDOW - 1, 1))
        nv_ref[slab, r] = jnp.where(last, vt[:, r:r + 1], pltpu.roll(vc_ref[0, r], WINDOW - 1, 1))

    lane = lax.broadcasted_iota(jnp.int32, (bb, LANES), 1)
    low = lane < HEAD_DIM
    for h in range(N_HEADS):
        kvh = h // GROUP
        qt = q[:, (h // 2) * LANES:(h // 2 + 1) * LANES]
        if (h % 2) != kvh:
            qt = pltpu.roll(qt, HEAD_DIM, 1)
        keep = low if kvh == 0 else jnp.logical_not(low)
        qm_scr[:, h, :] = jnp.where(keep, qt, 0.0)
        kn_scr[:, h, :] = k
        vn_scr[:, h, :] = v
    qm = qm_scr[...]

    s = _bdot(qm.astype(BF16), kc_ref[0].astype(BF16), 1)
    s_new = jnp.sum(qm * kn_scr[...], axis=-1, keepdims=True)
    hrow = lax.broadcasted_iota(jnp.int32, (1, N_HEADS, 1), 1)
    sink = jnp.full((1, N_HEADS, 1), sink_ref[layer, N_HEADS - 1], F32)
    for h in range(N_HEADS - 1):
        sink = jnp.where(hrow == h, sink_ref[layer, h], sink)
    m = jnp.maximum(jnp.maximum(jnp.max(s, axis=-1, keepdims=True), s_new), sink)
    p = jnp.exp(s - m)
    p_new = jnp.exp(s_new - m)
    denom = jnp.sum(p, axis=-1, keepdims=True) + p_new + jnp.exp(sink - m)
    o_scr[...] = (_bdot(p.astype(BF16), vc_ref[0].astype(BF16), 2) + p_new * vn_scr[...]) / denom

    for t in range(D_ATTN // LANES):
        oa = o_scr[:, 2 * t, :]
        ob = o_scr[:, 2 * t + 1, :]
        if (2 * t) // GROUP == 0:
            z = jnp.where(low, oa, pltpu.roll(ob, HEAD_DIM, 1))
        else:
            z = jnp.where(low, pltpu.roll(oa, HEAD_DIM, 1), ob)
        zbuf[:, D_CONV + t * LANES:D_CONV + (t + 1) * LANES] = z.astype(BF16)

    y = _dot(zbuf[...], wout_ref[0])
    o_ref[...] = _layer_norm(ALPHA * x + y, g_ref[layer, ln:ln + 1, :], b_ref[layer, ln:ln + 1, :])


def _mix_sample(x, sinks, st, kc, vc, prev_windows, win, cw, wout, g, b, layer, ln, bb):
    nb = x.shape[0]
    kern = functools.partial(_mix_sample_kernel, bb=bb, layer=layer, ln=ln)
    row = lambda i, s: (i, 0)
    win_blk = pl.BlockSpec((1, bb, D_KV, WINDOW), lambda i, s: (layer, i, 0, 0))
    in_specs = [
        pl.BlockSpec((bb, D_MODEL), row),
        pl.BlockSpec((bb, (CONV_W - 1) * D_CONV), row),
        win_blk,
        win_blk,
        _layer_resident((D_MODEL, IN_COLS), layer),
        _resident((DEPTH, CONV_W, D_CONV)),
        _layer_resident((D_CONV + D_ATTN, D_MODEL), layer),
    ] + _ln_specs()
    args = [sinks, x, st, kc, vc, win, cw, wout, g, b]
    aliases = {}
    out_win_blk = win_blk
    if prev_windows is None:
        out_win_blk = pl.BlockSpec((DEPTH, bb, D_KV, WINDOW), lambda i, s: (0, i, 0, 0))
    else:
        n_in = len(args)
        in_specs += [pl.BlockSpec(memory_space=pl.ANY)] * 2
        args += list(prev_windows)
        aliases = {n_in: 2, n_in + 1: 3}
    grid_spec = pltpu.PrefetchScalarGridSpec(
        num_scalar_prefetch=1,
        grid=(nb // bb,),
        in_specs=in_specs,
        out_specs=[pl.BlockSpec((bb, D_MODEL), row), pl.BlockSpec((bb, (CONV_W - 1) * D_CONV), row),
                   out_win_blk, out_win_blk],
        scratch_shapes=[pltpu.VMEM((bb, D_CONV + D_ATTN), BF16)]
        + [pltpu.VMEM((bb, N_HEADS, LANES), F32)] * 4,
    )
    return pl.pallas_call(
        kern,
        grid_spec=grid_spec,
        out_shape=[
            jax.ShapeDtypeStruct((nb, D_MODEL), F32),
            jax.ShapeDtypeStruct((nb, (CONV_W - 1) * D_CONV), F32),
            jax.ShapeDtypeStruct((DEPTH, nb, D_KV, WINDOW), F32),
            jax.ShapeDtypeStruct((DEPTH, nb, D_KV, WINDOW), F32),
        ],
        input_output_aliases=aliases,
        compiler_params=_params("arbitrary"),
        name="mix_sample",
    )(*args)


_MEM_HALVES = MEM_HEAD_DIM // LANES
_MEM_ROWS = _MEM_HALVES * MEM_HEADS
_CROSS_SAMPLE_UNROLL = 4


def _cross_sample_kernel(x_ref, mk_ref, mv_ref, wq_ref, wo_ref, g_ref, b_ref, o_ref, q8, o8, obuf,
                         *, bb, layer, ln):
    x = x_ref[...]
    q = _dot(x.astype(BF16), wq_ref[0]) * (MEM_HEAD_DIM ** -0.5)
    for half in range(_MEM_HALVES):
        for h in range(MEM_HEADS):
            c0 = h * MEM_HEAD_DIM + half * LANES
            q8[:, half * MEM_HEADS + h, :] = q[:, c0:c0 + LANES]

    def one_sequence(r, carry):
        kq = mk_ref[0, r].reshape(MEM_LEN, _MEM_ROWS, LANES) * q8[r][None]
        kq = kq + pltpu.roll(kq, MEM_HEADS, 1)
        s = jnp.sum(kq, axis=-1, keepdims=True)
        m = jnp.max(s, axis=0, keepdims=True)
        p = jnp.exp(s - m)
        denom = jnp.sum(p, axis=0, keepdims=True)
        pv = p * mv_ref[0, r].reshape(MEM_LEN, _MEM_ROWS, LANES)
        o8[r] = jnp.sum(pv, axis=0) / denom[0]
        return carry

    lax.fori_loop(0, bb, one_sequence, 0, unroll=_CROSS_SAMPLE_UNROLL)
    for half in range(_MEM_HALVES):
        for h in range(MEM_HEADS):
            c0 = h * MEM_HEAD_DIM + half * LANES
            obuf[:, c0:c0 + LANES] = o8[:, half * MEM_HEADS + h, :].astype(BF16)
    y = _dot(obuf[...], wo_ref[0])
    o_ref[...] = _layer_norm(ALPHA * x + y, g_ref[layer, ln:ln + 1, :], b_ref[layer, ln:ln + 1, :])


def _cross_sample(x, mk, mv, wq, wo, g, b, layer, ln, bb):
    nb = x.shape[0]
    kern = functools.partial(_cross_sample_kernel, bb=bb, layer=layer, ln=ln)
    xblk = pl.BlockSpec((bb, D_MODEL), lambda i: (i, 0))
    mblk = pl.BlockSpec((1, bb, MEM_LEN * _MEM_ROWS, LANES), lambda i: (layer, i, 0, 0))
    return pl.pallas_call(
        kern,
        grid=(nb // bb,),
        in_specs=[xblk, mblk, mblk, _layer_resident((D_MODEL, D_MODEL), layer),
                  _layer_resident((D_MODEL, D_MODEL), layer)] + _ln_specs(),
        out_specs=xblk,
        out_shape=jax.ShapeDtypeStruct((nb, D_MODEL), F32),
        scratch_shapes=[pltpu.VMEM((bb, _MEM_ROWS, LANES), F32),
                        pltpu.VMEM((bb, _MEM_ROWS, LANES), F32),
                        pltpu.VMEM((bb, D_MODEL), BF16)],
        compiler_params=_params("arbitrary"),
        name="cross_sample",
    )(x, mk, mv, wq, wo, g, b)


TM_PROMPT = 1024
BB_MIX_SAMPLE = 32
BB_CROSS_SAMPLE = 8


def kernel(x_prompt, x_sample, mem_prompt, cache_win_k, cache_win_v, state_conv, cache_mem_k, cache_mem_v,
           ln_g, ln_b, ffn1_w_gu, ffn1_w_down, w_in, conv_w, attn_sinks, w_out,
           w_cq, w_mk, w_mv, w_co, ffn2_w_gu, ffn2_w_down):
    bsz, seq, _ = x_prompt.shape
    nsmp = x_sample.shape[0]
    yp = x_prompt
    ys = x_sample.reshape(nsmp, D_MODEL)
    mem2 = mem_prompt.reshape(bsz * MEM_LEN, D_MODEL)
    bf = lambda w: w.astype(BF16)
    wgu1, wd1, wgu2, wd2 = bf(ffn1_w_gu), bf(ffn1_w_down), bf(ffn2_w_gu), bf(ffn2_w_down)
    win, wout, wcq, wco, wmk, wmv = bf(w_in), bf(w_out), bf(w_cq), bf(w_co), bf(w_mk), bf(w_mv)

    def token_tiles(c):
        c = c.reshape(DEPTH, nsmp, MEM_LEN, MEM_HEADS, _MEM_HALVES, LANES)
        return c.transpose(0, 1, 2, 4, 3, 5).reshape(DEPTH, nsmp, MEM_LEN * _MEM_ROWS, LANES)

    mem_k_tiles, mem_v_tiles = token_tiles(cache_mem_k), token_tiles(cache_mem_v)

    def window_lanes(c):
        return c.transpose(0, 1, 3, 4, 2).reshape(DEPTH, nsmp, D_KV, WINDOW)

    win_k_lanes, win_v_lanes = window_lanes(cache_win_k), window_lanes(cache_win_v)
    new_windows = None

    wkp, wvp, cvp, mkp, mvp, cvs = [], [], [], [], [], []
    for l in range(DEPTH):
        yp = _ffn(yp.reshape(bsz * seq, D_MODEL), wgu1, wd1, ln_g, ln_b, l, 0, TM_PROMPT).reshape(bsz, seq, D_MODEL)
        ys = _ffn(ys, wgu1, wd1, ln_g, ln_b, l, 0, nsmp)

        yp, nk, nv, ncv = _mix_prompt(yp, attn_sinks, win, conv_w, wout, ln_g, ln_b, l, 1, TM_PROMPT)
        wkp.append(nk.reshape(bsz, WINDOW, N_KV_HEADS, HEAD_DIM))
        wvp.append(nv.reshape(bsz, WINDOW, N_KV_HEADS, HEAD_DIM))
        cvp.append(ncv)
        ys, nst, nks, nvs = _mix_sample(
            ys, attn_sinks, state_conv[l].reshape(nsmp, (CONV_W - 1) * D_CONV),
            win_k_lanes, win_v_lanes, new_windows, win, conv_w, wout, ln_g, ln_b, l, 1, BB_MIX_SAMPLE)
        new_windows = (nks, nvs)
        cvs.append(nst.reshape(nsmp, CONV_W - 1, D_CONV))

        mk, mv, mkb, mvb = _memkv(mem2, wmk, wmv, l)
        mkp.append(mk.reshape(bsz, MEM_LEN, MEM_HEADS, MEM_HEAD_DIM))
        mvp.append(mv.reshape(bsz, MEM_LEN, MEM_HEADS, MEM_HEAD_DIM))
        yp = _cross_prompt(yp, mkb.reshape(bsz, MEM_LEN, D_MODEL), mvb.reshape(bsz, MEM_LEN, D_MODEL),
                           wcq, wco, ln_g, ln_b, l, 2, TM_PROMPT)
        ys = _cross_sample(ys, mem_k_tiles, mem_v_tiles, wcq, wco, ln_g, ln_b, l, 2, BB_CROSS_SAMPLE)

        yp = _ffn(yp.reshape(bsz * seq, D_MODEL), wgu2, wd2, ln_g, ln_b, l, 3, TM_PROMPT).reshape(bsz, seq, D_MODEL)
        ys = _ffn(ys, wgu2, wd2, ln_g, ln_b, l, 3, nsmp)

    def window_rows(c):
        return c.reshape(DEPTH, nsmp, N_KV_HEADS, HEAD_DIM, WINDOW).transpose(0, 1, 4, 2, 3)

    return (yp, ys.reshape(nsmp, 1, D_MODEL),
            jnp.stack(wkp), jnp.stack(wvp), jnp.stack(cvp), jnp.stack(mkp), jnp.stack(mvp),
            window_rows(new_windows[0]), window_rows(new_windows[1]), jnp.stack(cvs))
```

```python
import functools

import jax
import jax.numpy as jnp
from jax import lax
from jax.experimental import pallas as pl
from jax.experimental.pallas import tpu as pltpu

D_MODEL = 1024
DEPTH = 2
D_CONV = 512
CONV_W = 3
HEAD_DIM = 64
N_HEADS = 8
N_KV_HEADS = 2
GROUP = N_HEADS // N_KV_HEADS
WINDOW = 128
D_ATTN = N_HEADS * HEAD_DIM
D_KV = N_KV_HEADS * HEAD_DIM
IN_COLS = 3 * D_CONV + D_ATTN + 2 * D_KV
MEM_LEN = 256
MEM_HEADS = 4
MEM_HEAD_DIM = D_MODEL // MEM_HEADS
D_FF = 2816
N_LN = 4
ALPHA = (2.0 * DEPTH) ** 0.25
LN_EPS = 1e-5

_C_BG, _C_CG, _C_HC, _C_Q, _C_KV = 0, D_CONV, 2 * D_CONV, 3 * D_CONV, 3 * D_CONV + D_ATTN

LANES = 128
SUBLANES = 8
MXU_COLS = 256
VMEM_LIMIT_BYTES = 56 * 1024 * 1024

BF16 = jnp.bfloat16
F32 = jnp.float32


def _dot(a, b):
    return jnp.dot(a, b, preferred_element_type=F32)


def _dot_nt(a, b):
    return lax.dot_general(a, b, (((1,), (1,)), ((), ())), preferred_element_type=F32)


def _layer_norm(z, g, b):
    mu = jnp.mean(z, axis=-1, keepdims=True)
    d = z - mu
    var = jnp.mean(d * d, axis=-1, keepdims=True)
    return d * lax.rsqrt(var + LN_EPS) * g + b


def _resident(shape):
    zeros = (0,) * len(shape)
    return pl.BlockSpec(shape, lambda *_: zeros, pipeline_mode=pl.Buffered(1))


def _weight(shape):
    return _resident((1,) + shape)


def _cast_specs(jobs, n_steps, step_of):
    in_specs, out_specs, out_shapes, args = [], [], [], []
    for w, layer in jobs:
        _, k, n = w.shape
        rows = k // n_steps
        assert rows * n_steps == k and rows % (2 * SUBLANES) == 0, (w.shape, n_steps)
        in_specs.append(pl.BlockSpec((1, rows, n), lambda *idx, layer=layer: (layer, step_of(*idx), 0)))
        out_specs.append(pl.BlockSpec((1, rows, n), lambda *idx: (0, step_of(*idx), 0)))
        out_shapes.append(jax.ShapeDtypeStruct((1, k, n), BF16))
        args.append(w)
    return in_specs, out_specs, out_shapes, args


def _run_casts(src_refs, dst_refs):
    for src, dst in zip(src_refs, dst_refs, strict=True):
        dst[...] = src[...].astype(BF16)


def _ln_specs():
    return [_resident((DEPTH, N_LN, D_MODEL)), _resident((DEPTH, N_LN, D_MODEL))]


def _params(*sem):
    return pltpu.CompilerParams(dimension_semantics=sem, vmem_limit_bytes=VMEM_LIMIT_BYTES)


NORM_ROWS = 128


def _project_and_norm(lhs_ref, w, x_rows, scale, g, b, write_rows, rows):
    for r0 in range(0, rows, min(NORM_ROWS, rows)):
        r1 = r0 + min(NORM_ROWS, rows)
        y = _dot(lhs_ref[r0:r1, :], w)
        write_rows(r0, r1, _layer_norm(ALPHA * x_rows(r0, r1) + scale * y, g, b))


def _ffn_kernel(x_ref, wgu_ref, wd_ref, g_ref, b_ref, *rest, layer, ln, n_cast):
    cast_src, o_ref, cast_dst, h_ref = rest[:n_cast], rest[n_cast], rest[n_cast + 1:2 * n_cast + 1], rest[-1]
    _run_casts(cast_src, cast_dst)
    xb = x_ref[...].astype(BF16)
    for c in range(D_FF // MXU_COLS):
        lo = c * MXU_COLS
        gate = _dot(xb, wgu_ref[0, :, lo:lo + MXU_COLS])
        up = _dot(xb, wgu_ref[0, :, D_FF + lo:D_FF + lo + MXU_COLS])
        h_ref[:, lo:lo + MXU_COLS] = (gate * jax.nn.sigmoid(gate) * up).astype(BF16)

    def write_rows(r0, r1, y):
        o_ref[r0:r1, :] = y

    _project_and_norm(h_ref, wd_ref[0], lambda r0, r1: x_ref[r0:r1, :], 0.5,
                      g_ref[layer, ln:ln + 1, :], b_ref[layer, ln:ln + 1, :], write_rows, x_ref.shape[0])


def _ffn(x, wgu, wd, g, b, layer, ln, tm, casts=()):
    m = x.shape[0]
    n_steps = m // tm
    c_in, c_out, c_shapes, c_args = _cast_specs(casts, n_steps, lambda i: i)
    return pl.pallas_call(
        functools.partial(_ffn_kernel, layer=layer, ln=ln, n_cast=len(casts)),
        grid=(n_steps,),
        in_specs=[
            pl.BlockSpec((tm, D_MODEL), lambda i: (i, 0)),
            _weight((D_MODEL, 2 * D_FF)),
            _weight((D_FF, D_MODEL)),
        ] + _ln_specs() + c_in,
        out_specs=[pl.BlockSpec((tm, D_MODEL), lambda i: (i, 0))] + c_out,
        out_shape=[jax.ShapeDtypeStruct((m, D_MODEL), F32)] + c_shapes,
        scratch_shapes=[pltpu.VMEM((tm, D_FF), BF16)],
        compiler_params=_params("arbitrary"),
        name="ffn_ln",
    )(x, wgu, wd, g, b, *c_args)


def _dup_halves(a):
    lane = lax.broadcasted_iota(jnp.int32, a.shape, 1)
    low = lane < HEAD_DIM
    rolled = pltpu.roll(a, HEAD_DIM, 1)
    return jnp.where(low, a, rolled), jnp.where(low, rolled, a)


def _sink_column(sink_ref, layer, first_head, rows_per_head, n_heads):
    rows = n_heads * rows_per_head
    r = lax.broadcasted_iota(jnp.int32, (rows, 1), 0)
    col = jnp.full((rows, 1), sink_ref[layer, first_head + n_heads - 1], F32)
    for g in range(n_heads - 2, -1, -1):
        col = jnp.where(r < (g + 1) * rows_per_head, sink_ref[layer, first_head + g], col)
    return col


def _mix_prompt_kernel(sink_ref, x_ref, win_ref, cw_ref, wout_ref, g_ref, b_ref, *rest, tm, layer, ln, n_cast):
    cast_src, rest = rest[:n_cast], rest[n_cast:]
    (o_ref, nk_ref, nv_ref, nc_ref), rest = rest[:4], rest[4:]
    cast_dst, (ubuf, qbuf, kd0, kd1, vd0, vd1, zbuf) = rest[:n_cast], rest[n_cast:]
    _run_casts(cast_src, cast_dst)
    seq_start = pl.program_id(1) == 0
    xb = x_ref[0].astype(BF16)

    def proj(col, width):
        return _dot(xb, win_ref[0, :, col:col + width])

    @pl.when(seq_start)
    def _():
        ubuf[0:SUBLANES, :] = jnp.zeros((SUBLANES, D_CONV), F32)
        for r in (kd0, kd1, vd0, vd1):
            r[0:WINDOW, :] = jnp.zeros((WINDOW, LANES), BF16)

    u = proj(_C_CG, D_CONV) * proj(_C_HC, D_CONV)
    ubuf[SUBLANES:SUBLANES + tm, :] = u
    conv = (cw_ref[layer, 0:1, :] * ubuf[SUBLANES - 2:SUBLANES - 2 + tm, :]
            + cw_ref[layer, 1:2, :] * ubuf[SUBLANES - 1:SUBLANES - 1 + tm, :]
            + cw_ref[layer, 2:3, :] * u)
    zbuf[:, 0:D_CONV] = (proj(_C_BG, D_CONV) * conv).astype(BF16)
    nc_ref[0] = u[tm - (CONV_W - 1):tm, :]
    ubuf[0:SUBLANES, :] = u[tm - SUBLANES:tm, :]

    qbuf[...] = (proj(_C_Q, D_ATTN) * (HEAD_DIM ** -0.5)).astype(BF16)
    k = proj(_C_KV, D_KV)
    v = proj(_C_KV + D_KV, D_KV)
    nk_ref[0] = k[tm - WINDOW:tm, :]
    nv_ref[0] = v[tm - WINDOW:tm, :]
    ka, kb = _dup_halves(k)
    va, vb = _dup_halves(v)
    kd0[WINDOW:WINDOW + tm, :] = ka.astype(BF16)
    kd1[WINDOW:WINDOW + tm, :] = kb.astype(BF16)
    vd0[WINDOW:WINDOW + tm, :] = va.astype(BF16)
    vd1[WINDOW:WINDOW + tm, :] = vb.astype(BF16)

    rows = GROUP * WINDOW
    a_idx = lax.broadcasted_iota(jnp.int32, (rows, 2 * WINDOW), 0) % WINDOW
    c_idx = lax.broadcasted_iota(jnp.int32, (rows, 2 * WINDOW), 1)
    band = (c_idx >= a_idx) & (c_idx <= a_idx + WINDOW)
    first_lb = jnp.where(seq_start, WINDOW, 0)
    low = lax.broadcasted_iota(jnp.int32, (WINDOW, LANES), 1) < HEAD_DIM
    zero = jnp.zeros((WINDOW, LANES), BF16)

    for n in range(tm // WINDOW):
        r0 = n * WINDOW
        mask = (band & (c_idx >= first_lb)) if n == 0 else band
        for j, (kd, vd) in enumerate(((kd0, vd0), (kd1, vd1))):
            qa = qbuf[r0:r0 + WINDOW, (2 * j) * LANES:(2 * j + 1) * LANES]
            qb = qbuf[r0:r0 + WINDOW, (2 * j + 1) * LANES:(2 * j + 2) * LANES]
            qs = jnp.concatenate([jnp.where(low, qa, zero), jnp.where(low, zero, qa),
                                  jnp.where(low, qb, zero), jnp.where(low, zero, qb)], axis=0)
            s = _dot_nt(qs, kd[r0:r0 + 2 * WINDOW, :])
            s = jnp.where(mask, s, -jnp.inf)
            sink = _sink_column(sink_ref, layer, GROUP * j, WINDOW, GROUP)
            m = jnp.maximum(jnp.max(s, axis=-1, keepdims=True), sink)
            p = jnp.exp(s - m)
            denom = jnp.sum(p, axis=-1, keepdims=True) + jnp.exp(sink - m)
            o = _dot(p.astype(BF16), vd[r0:r0 + 2 * WINDOW, :]) / denom
            za = jnp.where(low, o[0:WINDOW], o[WINDOW:2 * WINDOW])
            zb = jnp.where(low, o[2 * WINDOW:3 * WINDOW], o[3 * WINDOW:4 * WINDOW])
            c0 = D_CONV + (2 * j) * LANES
            zbuf[r0:r0 + WINDOW, c0:c0 + LANES] = za.astype(BF16)
            zbuf[r0:r0 + WINDOW, c0 + LANES:c0 + 2 * LANES] = zb.astype(BF16)

    for r in (kd0, kd1, vd0, vd1):
        r[0:WINDOW, :] = r[tm:tm + WINDOW, :]

    def write_rows(r0, r1, y):
        o_ref[0, r0:r1, :] = y

    _project_and_norm(zbuf, wout_ref[0], lambda r0, r1: x_ref[0, r0:r1, :], 1.0,
                      g_ref[layer, ln:ln + 1, :], b_ref[layer, ln:ln + 1, :], write_rows, tm)


def _mix_prompt(x, sinks, win, cw, wout, g, b, layer, ln, tm, casts=()):
    bsz, seq, _ = x.shape
    tps = seq // tm
    kern = functools.partial(_mix_prompt_kernel, tm=tm, layer=layer, ln=ln, n_cast=len(casts))
    per_seq = lambda bi, i, s: (bi, 0, 0)
    c_in, c_out, c_shapes, c_args = _cast_specs(casts, bsz * tps, lambda bi, i, *_: bi * tps + i)
    grid_spec = pltpu.PrefetchScalarGridSpec(
        num_scalar_prefetch=1,
        grid=(bsz, tps),
        in_specs=[
            pl.BlockSpec((1, tm, D_MODEL), lambda bi, i, s: (bi, i, 0)),
            _weight((D_MODEL, IN_COLS)),
            _resident((DEPTH, CONV_W, D_CONV)),
            _weight((D_CONV + D_ATTN, D_MODEL)),
        ] + _ln_specs() + c_in,
        out_specs=[
            pl.BlockSpec((1, tm, D_MODEL), lambda bi, i, s: (bi, i, 0)),
            pl.BlockSpec((1, WINDOW, D_KV), per_seq),
            pl.BlockSpec((1, WINDOW, D_KV), per_seq),
            pl.BlockSpec((1, CONV_W - 1, D_CONV), per_seq),
        ] + c_out,
        scratch_shapes=[
            pltpu.VMEM((tm + SUBLANES, D_CONV), F32),
            pltpu.VMEM((tm, D_ATTN), BF16),
            pltpu.VMEM((tm + WINDOW, LANES), BF16),
            pltpu.VMEM((tm + WINDOW, LANES), BF16),
            pltpu.VMEM((tm + WINDOW, LANES), BF16),
            pltpu.VMEM((tm + WINDOW, LANES), BF16),
            pltpu.VMEM((tm, D_CONV + D_ATTN), BF16),
        ],
    )
    return pl.pallas_call(
        kern,
        grid_spec=grid_spec,
        out_shape=[
            jax.ShapeDtypeStruct((bsz, seq, D_MODEL), F32),
            jax.ShapeDtypeStruct((bsz, WINDOW, D_KV), F32),
            jax.ShapeDtypeStruct((bsz, WINDOW, D_KV), F32),
            jax.ShapeDtypeStruct((bsz, CONV_W - 1, D_CONV), F32),
        ] + c_shapes,
        compiler_params=_params("arbitrary", "arbitrary"),
        name="mix_prompt",
    )(sinks, x, win, cw, wout, g, b, *c_args)


def _memkv_kernel(m_ref, wk_ref, wv_ref, k_ref, v_ref, kb_ref, vb_ref):
    mb = m_ref[...].astype(BF16)
    k = _dot(mb, wk_ref[0])
    v = _dot(mb, wv_ref[0])
    k_ref[...] = k
    v_ref[...] = v
    kb_ref[...] = k.astype(BF16)
    vb_ref[...] = v.astype(BF16)


def _memkv(mem, wk, wv):
    m = mem.shape[0]
    blk = pl.BlockSpec((MEM_LEN, D_MODEL), lambda i: (i, 0))
    return pl.pallas_call(
        _memkv_kernel,
        grid=(m // MEM_LEN,),
        in_specs=[blk, _weight((D_MODEL, D_MODEL)), _weight((D_MODEL, D_MODEL))],
        out_specs=[blk, blk, blk, blk],
        out_shape=[jax.ShapeDtypeStruct((m, D_MODEL), F32), jax.ShapeDtypeStruct((m, D_MODEL), F32),
                   jax.ShapeDtypeStruct((m, D_MODEL), BF16), jax.ShapeDtypeStruct((m, D_MODEL), BF16)],
        compiler_params=_params("arbitrary"),
        name="mem_kv",
    )(mem, wk, wv)


def _cross_prompt_kernel(x_ref, mk_ref, mv_ref, wq_ref, wo_ref, g_ref, b_ref, *rest, layer, ln, n_cast):
    cast_src, o_ref, cast_dst, obuf = rest[:n_cast], rest[n_cast], rest[n_cast + 1:2 * n_cast + 1], rest[-1]
    _run_casts(cast_src, cast_dst)
    q = _dot(x_ref[0].astype(BF16), wq_ref[0]) * (MEM_HEAD_DIM ** -0.5)
    for h in range(MEM_HEADS):
        c0 = h * MEM_HEAD_DIM
        s = _dot_nt(q[:, c0:c0 + MEM_HEAD_DIM].astype(BF16), mk_ref[0, :, c0:c0 + MEM_HEAD_DIM])
        m = jnp.max(s, axis=-1, keepdims=True)
        p = jnp.exp(s - m)
        denom = jnp.sum(p, axis=-1, keepdims=True)
        o = _dot(p.astype(BF16), mv_ref[0, :, c0:c0 + MEM_HEAD_DIM]) / denom
        obuf[:, c0:c0 + MEM_HEAD_DIM] = o.astype(BF16)

    def write_rows(r0, r1, y):
        o_ref[0, r0:r1, :] = y

    _project_and_norm(obuf, wo_ref[0], lambda r0, r1: x_ref[0, r0:r1, :], 1.0,
                      g_ref[layer, ln:ln + 1, :], b_ref[layer, ln:ln + 1, :], write_rows, x_ref.shape[1])


def _cross_prompt(x, mk, mv, wq, wo, g, b, layer, ln, tm, casts=()):
    bsz, seq, _ = x.shape
    tps = seq // tm
    xblk = pl.BlockSpec((1, tm, D_MODEL), lambda bi, i: (bi, i, 0))
    mblk = pl.BlockSpec((1, MEM_LEN, D_MODEL), lambda bi, i: (bi, 0, 0))
    c_in, c_out, c_shapes, c_args = _cast_specs(casts, bsz * tps, lambda bi, i: bi * tps + i)
    return pl.pallas_call(
        functools.partial(_cross_prompt_kernel, layer=layer, ln=ln, n_cast=len(casts)),
        grid=(bsz, tps),
        in_specs=[xblk, mblk, mblk, _weight((D_MODEL, D_MODEL)), _weight((D_MODEL, D_MODEL))] + _ln_specs() + c_in,
        out_specs=[xblk] + c_out,
        out_shape=[jax.ShapeDtypeStruct((bsz, seq, D_MODEL), F32)] + c_shapes,
        scratch_shapes=[pltpu.VMEM((tm, D_MODEL), BF16)],
        compiler_params=_params("arbitrary", "arbitrary"),
        name="cross_prompt",
    )(x, mk, mv, wq, wo, g, b, *c_args)


def _bdot(a, b, contract_b):
    return lax.dot_general(a, b, (((2,), (contract_b,)), ((0,), (0,))), preferred_element_type=F32)


def _mix_sample_kernel(sink_ref, x_ref, st_ref, kc_ref, vc_ref, win_ref, cw_ref, wout_ref, g_ref, b_ref,
                       *rest, bb, layer, ln):
    o_ref, nst_ref, nk_ref, nv_ref, zbuf, qm_scr, kn_scr, vn_scr, o_scr = rest[-9:]
    x = x_ref[...]
    xb = x.astype(BF16)

    def proj(col, width):
        return _dot(xb, win_ref[0, :, col:col + width])

    u = proj(_C_CG, D_CONV) * proj(_C_HC, D_CONV)
    st0 = st_ref[:, 0:D_CONV]
    st1 = st_ref[:, D_CONV:2 * D_CONV]
    conv = cw_ref[layer, 0:1, :] * st0 + cw_ref[layer, 1:2, :] * st1 + cw_ref[layer, 2:3, :] * u
    zbuf[:, 0:D_CONV] = (proj(_C_BG, D_CONV) * conv).astype(BF16)
    nst_ref[:, 0:D_CONV] = st1
    nst_ref[:, D_CONV:2 * D_CONV] = u

    q = proj(_C_Q, D_ATTN) * (HEAD_DIM ** -0.5)
    k = proj(_C_KV, D_KV)
    v = proj(_C_KV + D_KV, D_KV)

    pad = jnp.zeros((LANES - bb, D_KV), F32)
    kt = jnp.concatenate([k, pad], axis=0).T if bb < LANES else k.T
    vt = jnp.concatenate([v, pad], axis=0).T if bb < LANES else v.T
    last = lax.broadcasted_iota(jnp.int32, (D_KV, WINDOW), 1) == WINDOW - 1
    slab = layer if nk_ref.shape[0] > 1 else 0
    for other in range(nk_ref.shape[0]):
        if other != slab:
            nk_ref[other] = jnp.zeros(nk_ref.shape[1:], F32)
            nv_ref[other] = jnp.zeros(nv_ref.shape[1:], F32)
    for r in range(bb):
        nk_ref[slab, r] = jnp.where(last, kt[:, r:r + 1], pltpu.roll(kc_ref[0, r], WINDOW - 1, 1))
        nv_ref[slab, r] = jnp.where(last, vt[:, r:r + 1], pltpu.roll(vc_ref[0, r], WINDOW - 1, 1))

    lane = lax.broadcasted_iota(jnp.int32, (bb, LANES), 1)
    low = lane < HEAD_DIM
    for h in range(N_HEADS):
        kvh = h // GROUP
        qt = q[:, (h // 2) * LANES:(h // 2 + 1) * LANES]
        if (h % 2) != kvh:
            qt = pltpu.roll(qt, HEAD_DIM, 1)
        keep = low if kvh == 0 else jnp.logical_not(low)
        qm_scr[:, h, :] = jnp.where(keep, qt, 0.0)
        kn_scr[:, h, :] = k
        vn_scr[:, h, :] = v
    qm = qm_scr[...]

    s = _bdot(qm.astype(BF16), kc_ref[0].astype(BF16), 1)
    s_new = jnp.sum(qm * kn_scr[...], axis=-1, keepdims=True)
    hrow = lax.broadcasted_iota(jnp.int32, (1, N_HEADS, 1), 1)
    sink = jnp.full((1, N_HEADS, 1), sink_ref[layer, N_HEADS - 1], F32)
    for h in range(N_HEADS - 1):
        sink = jnp.where(hrow == h, sink_ref[layer, h], sink)
    m = jnp.maximum(jnp.maximum(jnp.max(s, axis=-1, keepdims=True), s_new), sink)
    p = jnp.exp(s - m)
    p_new = jnp.exp(s_new - m)
    denom = jnp.sum(p, axis=-1, keepdims=True) + p_new + jnp.exp(sink - m)
    o_scr[...] = (_bdot(p.astype(BF16), vc_ref[0].astype(BF16), 2) + p_new * vn_scr[...]) / denom

    for t in range(D_ATTN // LANES):
        oa = o_scr[:, 2 * t, :]
        ob = o_scr[:, 2 * t + 1, :]
        if (2 * t) // GROUP == 0:
            z = jnp.where(low, oa, pltpu.roll(ob, HEAD_DIM, 1))
        else:
            z = jnp.where(low, pltpu.roll(oa, HEAD_DIM, 1), ob)
        zbuf[:, D_CONV + t * LANES:D_CONV + (t + 1) * LANES] = z.astype(BF16)

    y = _dot(zbuf[...], wout_ref[0])
    o_ref[...] = _layer_norm(ALPHA * x + y, g_ref[layer, ln:ln + 1, :], b_ref[layer, ln:ln + 1, :])


def _mix_sample(x, sinks, st, kc, vc, prev_windows, win, cw, wout, g, b, layer, ln, bb):
    nb = x.shape[0]
    kern = functools.partial(_mix_sample_kernel, bb=bb, layer=layer, ln=ln)
    row = lambda i, s: (i, 0)
    win_blk = pl.BlockSpec((1, bb, D_KV, WINDOW), lambda i, s: (layer, i, 0, 0))
    in_specs = [
        pl.BlockSpec((bb, D_MODEL), row),
        pl.BlockSpec((bb, (CONV_W - 1) * D_CONV), row),
        win_blk,
        win_blk,
        _weight((D_MODEL, IN_COLS)),
        _resident((DEPTH, CONV_W, D_CONV)),
        _weight((D_CONV + D_ATTN, D_MODEL)),
    ] + _ln_specs()
    args = [sinks, x, st, kc, vc, win, cw, wout, g, b]
    aliases = {}
    out_win_blk = win_blk
    if prev_windows is None:
        out_win_blk = pl.BlockSpec((DEPTH, bb, D_KV, WINDOW), lambda i, s: (0, i, 0, 0))
    else:
        n_in = len(args)
        in_specs += [pl.BlockSpec(memory_space=pl.ANY)] * 2
        args += list(prev_windows)
        aliases = {n_in: 2, n_in + 1: 3}
    grid_spec = pltpu.PrefetchScalarGridSpec(
        num_scalar_prefetch=1,
        grid=(nb // bb,),
        in_specs=in_specs,
        out_specs=[pl.BlockSpec((bb, D_MODEL), row), pl.BlockSpec((bb, (CONV_W - 1) * D_CONV), row),
                   out_win_blk, out_win_blk],
        scratch_shapes=[pltpu.VMEM((bb, D_CONV + D_ATTN), BF16)]
        + [pltpu.VMEM((bb, N_HEADS, LANES), F32)] * 4,
    )
    return pl.pallas_call(
        kern,
        grid_spec=grid_spec,
        out_shape=[
            jax.ShapeDtypeStruct((nb, D_MODEL), F32),
            jax.ShapeDtypeStruct((nb, (CONV_W - 1) * D_CONV), F32),
            jax.ShapeDtypeStruct((DEPTH, nb, D_KV, WINDOW), F32),
            jax.ShapeDtypeStruct((DEPTH, nb, D_KV, WINDOW), F32),
        ],
        input_output_aliases=aliases,
        compiler_params=_params("arbitrary"),
        name="mix_sample",
    )(*args)


_MEM_HALVES = MEM_HEAD_DIM // LANES
_MEM_ROWS = _MEM_HALVES * MEM_HEADS
_CROSS_SAMPLE_UNROLL = 4


def _cross_sample_kernel(x_ref, mk_ref, mv_ref, wq_ref, wo_ref, g_ref, b_ref, o_ref, q8, o8, obuf,
                         *, bb, layer, ln):
    x = x_ref[...]
    q = _dot(x.astype(BF16), wq_ref[0]) * (MEM_HEAD_DIM ** -0.5)
    for half in range(_MEM_HALVES):
        for h in range(MEM_HEADS):
            c0 = h * MEM_HEAD_DIM + half * LANES
            q8[:, half * MEM_HEADS + h, :] = q[:, c0:c0 + LANES]

    def one_sequence(r, carry):
        kq = mk_ref[0, r].reshape(MEM_LEN, _MEM_ROWS, LANES) * q8[r][None]
        kq = kq + pltpu.roll(kq, MEM_HEADS, 1)
        s = jnp.sum(kq, axis=-1, keepdims=True)
        m = jnp.max(s, axis=0, keepdims=True)
        p = jnp.exp(s - m)
        denom = jnp.sum(p, axis=0, keepdims=True)
        pv = p * mv_ref[0, r].reshape(MEM_LEN, _MEM_ROWS, LANES)
        o8[r] = jnp.sum(pv, axis=0) / denom[0]
        return carry

    lax.fori_loop(0, bb, one_sequence, 0, unroll=_CROSS_SAMPLE_UNROLL)
    for half in range(_MEM_HALVES):
        for h in range(MEM_HEADS):
            c0 = h * MEM_HEAD_DIM + half * LANES
            obuf[:, c0:c0 + LANES] = o8[:, half * MEM_HEADS + h, :].astype(BF16)
    y = _dot(obuf[...], wo_ref[0])
    o_ref[...] = _layer_norm(ALPHA * x + y, g_ref[layer, ln:ln + 1, :], b_ref[layer, ln:ln + 1, :])


def _cross_sample(x, mk, mv, wq, wo, g, b, layer, ln, bb):
    nb = x.shape[0]
    kern = functools.partial(_cross_sample_kernel, bb=bb, layer=layer, ln=ln)
    xblk = pl.BlockSpec((bb, D_MODEL), lambda i: (i, 0))
    mblk = pl.BlockSpec((1, bb, MEM_LEN * _MEM_ROWS, LANES), lambda i: (layer, i, 0, 0))
    return pl.pallas_call(
        kern,
        grid=(nb // bb,),
        in_specs=[xblk, mblk, mblk, _weight((D_MODEL, D_MODEL)), _weight((D_MODEL, D_MODEL))] + _ln_specs(),
        out_specs=xblk,
        out_shape=jax.ShapeDtypeStruct((nb, D_MODEL), F32),
        scratch_shapes=[pltpu.VMEM((bb, _MEM_ROWS, LANES), F32),
                        pltpu.VMEM((bb, _MEM_ROWS, LANES), F32),
                        pltpu.VMEM((bb, D_MODEL), BF16)],
        compiler_params=_params("arbitrary"),
        name="cross_sample",
    )(x, mk, mv, wq, wo, g, b)


TM_PROMPT = 1024
BB_MIX_SAMPLE = 32
BB_CROSS_SAMPLE = 8


def kernel(x_prompt, x_sample, mem_prompt, cache_win_k, cache_win_v, state_conv, cache_mem_k, cache_mem_v,
           ln_g, ln_b, ffn1_w_gu, ffn1_w_down, w_in, conv_w, attn_sinks, w_out,
           w_cq, w_mk, w_mv, w_co, ffn2_w_gu, ffn2_w_down):
    bsz, seq, _ = x_prompt.shape
    nsmp = x_sample.shape[0]
    yp = x_prompt
    ys = x_sample.reshape(nsmp, D_MODEL)
    mem2 = mem_prompt.reshape(bsz * MEM_LEN, D_MODEL)

    stacked = dict(ffn1=(ffn1_w_gu, ffn1_w_down), ffn2=(ffn2_w_gu, ffn2_w_down),
                   mix=(w_in, w_out), cross=(w_cq, w_co, w_mk, w_mv))
    wb = {("ffn1", 0): tuple(w[0:1].astype(BF16) for w in stacked["ffn1"])}

    def jobs(*groups):
        return [(w, layer) for name, layer in groups for w in stacked[name]]

    def keep(groups, casts):
        casts = list(casts)
        for name, layer in groups:
            wb[(name, layer)] = tuple(casts[:len(stacked[name])])
            del casts[:len(stacked[name])]

    def token_tiles(c):
        c = c.reshape(DEPTH, nsmp, MEM_LEN, MEM_HEADS, _MEM_HALVES, LANES)
        return c.transpose(0, 1, 2, 4, 3, 5).reshape(DEPTH, nsmp, MEM_LEN * _MEM_ROWS, LANES)

    mem_k_tiles, mem_v_tiles = token_tiles(cache_mem_k), token_tiles(cache_mem_v)

    def window_lanes(c):
        return c.transpose(0, 1, 3, 4, 2).reshape(DEPTH, nsmp, D_KV, WINDOW)

    win_k_lanes, win_v_lanes = window_lanes(cache_win_k), window_lanes(cache_win_v)
    new_windows = None

    wkp, wvp, cvp, mkp, mvp, cvs = [], [], [], [], [], []
    for l in range(DEPTH):
        more = l + 1 < DEPTH
        side = [("mix", 0), ("cross", 0)] if l == 0 else [("ffn2", l)]
        yp, *casts = _ffn(yp.reshape(bsz * seq, D_MODEL), *wb[("ffn1", l)], ln_g, ln_b, l, 0, TM_PROMPT, jobs(*side))
        keep(side, casts)
        yp = yp.reshape(bsz, seq, D_MODEL)
        ys, = _ffn(ys, *wb[("ffn1", l)], ln_g, ln_b, l, 0, nsmp)

        win, wout = wb[("mix", l)]
        side = [("ffn2", 0)] if l == 0 else []
        yp, nk, nv, ncv, *casts = _mix_prompt(yp, attn_sinks, win, conv_w, wout, ln_g, ln_b, l, 1, TM_PROMPT,
                                              jobs(*side))
        keep(side, casts)
        wkp.append(nk.reshape(bsz, WINDOW, N_KV_HEADS, HEAD_DIM))
        wvp.append(nv.reshape(bsz, WINDOW, N_KV_HEADS, HEAD_DIM))
        cvp.append(ncv)
        ys, nst, nks, nvs = _mix_sample(
            ys, attn_sinks, state_conv[l].reshape(nsmp, (CONV_W - 1) * D_CONV),
            win_k_lanes, win_v_lanes, new_windows, win, conv_w, wout, ln_g, ln_b, l, 1, BB_MIX_SAMPLE)
        new_windows = (nks, nvs)
        cvs.append(nst.reshape(nsmp, CONV_W - 1, D_CONV))

        wcq, wco, wmk, wmv = wb[("cross", l)]
        mk, mv, mkb, mvb = _memkv(mem2, wmk, wmv)
        mkp.append(mk.reshape(bsz, MEM_LEN, MEM_HEADS, MEM_HEAD_DIM))
        mvp.append(mv.reshape(bsz, MEM_LEN, MEM_HEADS, MEM_HEAD_DIM))
        side = [("ffn1", l + 1)] if more else []
        yp, *casts = _cross_prompt(yp, mkb.reshape(bsz, MEM_LEN, D_MODEL), mvb.reshape(bsz, MEM_LEN, D_MODEL),
                                   wcq, wco, ln_g, ln_b, l, 2, TM_PROMPT, jobs(*side))
        keep(side, casts)
        ys = _cross_sample(ys, mem_k_tiles, mem_v_tiles, wcq, wco, ln_g, ln_b, l, 2, BB_CROSS_SAMPLE)

        side = [("mix", l + 1), ("cross", l + 1)] if more else []
        yp, *casts = _ffn(yp.reshape(bsz * seq, D_MODEL), *wb[("ffn2", l)], ln_g, ln_b, l, 3, TM_PROMPT, jobs(*side))
        keep(side, casts)
        yp = yp.reshape(bsz, seq, D_MODEL)
        ys, = _ffn(ys, *wb[("ffn2", l)], ln_g, ln_b, l, 3, nsmp)

    def window_rows(c):
        return c.reshape(DEPTH, nsmp, N_KV_HEADS, HEAD_DIM, WINDOW).transpose(0, 1, 4, 2, 3)

    return (yp, ys.reshape(nsmp, 1, D_MODEL),
            jnp.stack(wkp), jnp.stack(wvp), jnp.stack(cvp), jnp.stack(mkp), jnp.stack(mvp),
            window_rows(new_windows[0]), window_rows(new_windows[1]), jnp.stack(cvs))
```

```python
import functools

import jax
import jax.numpy as jnp
from jax import lax
from jax.experimental import pallas as pl
from jax.experimental.pallas import tpu as pltpu

D_MODEL = 1024
DEPTH = 2
D_CONV = 512
CONV_W = 3
HEAD_DIM = 64
N_HEADS = 8
N_KV_HEADS = 2
GROUP = N_HEADS // N_KV_HEADS
WINDOW = 128
D_ATTN = N_HEADS * HEAD_DIM
D_KV = N_KV_HEADS * HEAD_DIM
IN_COLS = 3 * D_CONV + D_ATTN + 2 * D_KV
MEM_LEN = 256
MEM_HEADS = 4
MEM_HEAD_DIM = D_MODEL // MEM_HEADS
D_FF = 2816
N_LN = 4
ALPHA = (2.0 * DEPTH) ** 0.25
LN_EPS = 1e-5

_C_BG, _C_CG, _C_HC, _C_Q, _C_KV = 0, D_CONV, 2 * D_CONV, 3 * D_CONV, 3 * D_CONV + D_ATTN

LANES = 128
SUBLANES = 8
MXU_COLS = 256
VMEM_LIMIT_BYTES = 56 * 1024 * 1024

BF16 = jnp.bfloat16
F32 = jnp.float32


def _dot(a, b):
    return jnp.dot(a, b, preferred_element_type=F32)


def _dot_nt(a, b):
    return lax.dot_general(a, b, (((1,), (1,)), ((), ())), preferred_element_type=F32)


def _layer_norm(z, g, b):
    mu = jnp.mean(z, axis=-1, keepdims=True)
    d = z - mu
    var = jnp.mean(d * d, axis=-1, keepdims=True)
    return d * lax.rsqrt(var + LN_EPS) * g + b


def _resident(shape):
    zeros = (0,) * len(shape)
    return pl.BlockSpec(shape, lambda *_: zeros, pipeline_mode=pl.Buffered(1))


def _weight(shape):
    return _resident((1,) + shape)


def _cast_specs(jobs, n_steps, step_of):
    in_specs, out_specs, out_shapes, args = [], [], [], []
    for w, layer in jobs:
        _, k, n = w.shape
        rows = k // n_steps
        assert rows * n_steps == k and rows % (2 * SUBLANES) == 0, (w.shape, n_steps)
        in_specs.append(pl.BlockSpec((1, rows, n), lambda *idx, layer=layer: (layer, step_of(*idx), 0)))
        out_specs.append(pl.BlockSpec((1, rows, n), lambda *idx: (0, step_of(*idx), 0)))
        out_shapes.append(jax.ShapeDtypeStruct((1, k, n), BF16))
        args.append(w)
    return in_specs, out_specs, out_shapes, args


def _run_casts(src_refs, dst_refs):
    for src, dst in zip(src_refs, dst_refs, strict=True):
        dst[...] = src[...].astype(BF16)


def _ln_specs():
    return [_resident((DEPTH, N_LN, D_MODEL)), _resident((DEPTH, N_LN, D_MODEL))]


def _params(*sem):
    return pltpu.CompilerParams(dimension_semantics=sem, vmem_limit_bytes=VMEM_LIMIT_BYTES)


NORM_ROWS = 128


def _project_and_norm(lhs_ref, w, x_rows, scale, g, b, write_rows, rows):
    for r0 in range(0, rows, min(NORM_ROWS, rows)):
        r1 = r0 + min(NORM_ROWS, rows)
        y = _dot(lhs_ref[r0:r1, :], w)
        write_rows(r0, r1, _layer_norm(ALPHA * x_rows(r0, r1) + scale * y, g, b))


def _ffn_rows(x_ref, o_ref, wgu_ref, wd_ref, h_ref, g, b):
    rows = x_ref.shape[0]
    xb = x_ref[...].astype(BF16)
    for c in range(D_FF // MXU_COLS):
        lo = c * MXU_COLS
        gate = _dot(xb, wgu_ref[0, :, lo:lo + MXU_COLS])
        up = _dot(xb, wgu_ref[0, :, D_FF + lo:D_FF + lo + MXU_COLS])
        h_ref[0:rows, lo:lo + MXU_COLS] = (gate * jax.nn.sigmoid(gate) * up).astype(BF16)

    def write_rows(r0, r1, y):
        o_ref[r0:r1, :] = y

    _project_and_norm(h_ref, wd_ref[0], lambda r0, r1: x_ref[r0:r1, :], 0.5, g, b, write_rows, rows)


def _ffn_kernel(x_ref, xs_ref, wgu_ref, wd_ref, g_ref, b_ref, *rest, layer, ln, n_cast, n_tiles):
    cast_src, (o_ref, os_ref), rest = rest[:n_cast], rest[n_cast:n_cast + 2], rest[n_cast + 2:]
    cast_dst, h_ref = rest[:n_cast], rest[-1]
    g = g_ref[layer, ln:ln + 1, :]
    b = b_ref[layer, ln:ln + 1, :]
    step = pl.program_id(0)

    @pl.when(step < n_tiles)
    def _():
        _run_casts(cast_src, cast_dst)
        _ffn_rows(x_ref, o_ref, wgu_ref, wd_ref, h_ref, g, b)

    @pl.when(step == n_tiles)
    def _():
        _ffn_rows(xs_ref, os_ref, wgu_ref, wd_ref, h_ref, g, b)


def _ffn(x, xs, wgu, wd, g, b, layer, ln, tm, casts=()):
    m = x.shape[0]
    ns = xs.shape[0]
    n_tiles = m // tm
    tile = lambda i: (jnp.minimum(i, n_tiles - 1), 0)
    c_in, c_out, c_shapes, c_args = _cast_specs(casts, n_tiles, lambda i: jnp.minimum(i, n_tiles - 1))
    return pl.pallas_call(
        functools.partial(_ffn_kernel, layer=layer, ln=ln, n_cast=len(casts), n_tiles=n_tiles),
        grid=(n_tiles + 1,),
        in_specs=[
            pl.BlockSpec((tm, D_MODEL), tile),
            _resident((ns, D_MODEL)),
            _weight((D_MODEL, 2 * D_FF)),
            _weight((D_FF, D_MODEL)),
        ] + _ln_specs() + c_in,
        out_specs=[pl.BlockSpec((tm, D_MODEL), tile), pl.BlockSpec((ns, D_MODEL), lambda i: (0, 0))] + c_out,
        out_shape=[jax.ShapeDtypeStruct((m, D_MODEL), F32), jax.ShapeDtypeStruct((ns, D_MODEL), F32)] + c_shapes,
        scratch_shapes=[pltpu.VMEM((tm, D_FF), BF16)],
        compiler_params=_params("arbitrary"),
        name="ffn_ln",
    )(x, xs, wgu, wd, g, b, *c_args)


def _dup_halves(a):
    lane = lax.broadcasted_iota(jnp.int32, a.shape, 1)
    low = lane < HEAD_DIM
    rolled = pltpu.roll(a, HEAD_DIM, 1)
    return jnp.where(low, a, rolled), jnp.where(low, rolled, a)


def _sink_column(sink_ref, layer, first_head, rows_per_head, n_heads):
    rows = n_heads * rows_per_head
    r = lax.broadcasted_iota(jnp.int32, (rows, 1), 0)
    col = jnp.full((rows, 1), sink_ref[layer, first_head + n_heads - 1], F32)
    for g in range(n_heads - 2, -1, -1):
        col = jnp.where(r < (g + 1) * rows_per_head, sink_ref[layer, first_head + g], col)
    return col


def _mix_prompt_kernel(sink_ref, x_ref, win_ref, cw_ref, wout_ref, g_ref, b_ref, *rest, tm, layer, ln, n_cast):
    cast_src, rest = rest[:n_cast], rest[n_cast:]
    (o_ref, nk_ref, nv_ref, nc_ref), rest = rest[:4], rest[4:]
    cast_dst, (ubuf, qbuf, kd0, kd1, vd0, vd1, zbuf) = rest[:n_cast], rest[n_cast:]
    _run_casts(cast_src, cast_dst)
    seq_start = pl.program_id(1) == 0
    xb = x_ref[0].astype(BF16)

    def proj(col, width):
        return _dot(xb, win_ref[0, :, col:col + width])

    @pl.when(seq_start)
    def _():
        ubuf[0:SUBLANES, :] = jnp.zeros((SUBLANES, D_CONV), F32)
        for r in (kd0, kd1, vd0, vd1):
            r[0:WINDOW, :] = jnp.zeros((WINDOW, LANES), BF16)

    u = proj(_C_CG, D_CONV) * proj(_C_HC, D_CONV)
    ubuf[SUBLANES:SUBLANES + tm, :] = u
    conv = (cw_ref[layer, 0:1, :] * ubuf[SUBLANES - 2:SUBLANES - 2 + tm, :]
            + cw_ref[layer, 1:2, :] * ubuf[SUBLANES - 1:SUBLANES - 1 + tm, :]
            + cw_ref[layer, 2:3, :] * u)
    zbuf[:, 0:D_CONV] = (proj(_C_BG, D_CONV) * conv).astype(BF16)
    nc_ref[0] = u[tm - (CONV_W - 1):tm, :]
    ubuf[0:SUBLANES, :] = u[tm - SUBLANES:tm, :]

    qbuf[...] = (proj(_C_Q, D_ATTN) * (HEAD_DIM ** -0.5)).astype(BF16)
    k = proj(_C_KV, D_KV)
    v = proj(_C_KV + D_KV, D_KV)
    nk_ref[0] = k[tm - WINDOW:tm, :]
    nv_ref[0] = v[tm - WINDOW:tm, :]
    ka, kb = _dup_halves(k)
    va, vb = _dup_halves(v)
    kd0[WINDOW:WINDOW + tm, :] = ka.astype(BF16)
    kd1[WINDOW:WINDOW + tm, :] = kb.astype(BF16)
    vd0[WINDOW:WINDOW + tm, :] = va.astype(BF16)
    vd1[WINDOW:WINDOW + tm, :] = vb.astype(BF16)

    rows = GROUP * WINDOW
    a_idx = lax.broadcasted_iota(jnp.int32, (rows, 2 * WINDOW), 0) % WINDOW
    c_idx = lax.broadcasted_iota(jnp.int32, (rows, 2 * WINDOW), 1)
    band = (c_idx >= a_idx) & (c_idx <= a_idx + WINDOW)
    first_lb = jnp.where(seq_start, WINDOW, 0)
    low = lax.broadcasted_iota(jnp.int32, (WINDOW, LANES), 1) < HEAD_DIM
    zero = jnp.zeros((WINDOW, LANES), BF16)

    for n in range(tm // WINDOW):
        r0 = n * WINDOW
        mask = (band & (c_idx >= first_lb)) if n == 0 else band
        for j, (kd, vd) in enumerate(((kd0, vd0), (kd1, vd1))):
            qa = qbuf[r0:r0 + WINDOW, (2 * j) * LANES:(2 * j + 1) * LANES]
            qb = qbuf[r0:r0 + WINDOW, (2 * j + 1) * LANES:(2 * j + 2) * LANES]
            qs = jnp.concatenate([jnp.where(low, qa, zero), jnp.where(low, zero, qa),
                                  jnp.where(low, qb, zero), jnp.where(low, zero, qb)], axis=0)
            s = _dot_nt(qs, kd[r0:r0 + 2 * WINDOW, :])
            s = jnp.where(mask, s, -jnp.inf)
            sink = _sink_column(sink_ref, layer, GROUP * j, WINDOW, GROUP)
            m = jnp.maximum(jnp.max(s, axis=-1, keepdims=True), sink)
            p = jnp.exp(s - m)
            denom = jnp.sum(p, axis=-1, keepdims=True) + jnp.exp(sink - m)
            o = _dot(p.astype(BF16), vd[r0:r0 + 2 * WINDOW, :]) / denom
            za = jnp.where(low, o[0:WINDOW], o[WINDOW:2 * WINDOW])
            zb = jnp.where(low, o[2 * WINDOW:3 * WINDOW], o[3 * WINDOW:4 * WINDOW])
            c0 = D_CONV + (2 * j) * LANES
            zbuf[r0:r0 + WINDOW, c0:c0 + LANES] = za.astype(BF16)
            zbuf[r0:r0 + WINDOW, c0 + LANES:c0 + 2 * LANES] = zb.astype(BF16)

    for r in (kd0, kd1, vd0, vd1):
        r[0:WINDOW, :] = r[tm:tm + WINDOW, :]

    def write_rows(r0, r1, y):
        o_ref[0, r0:r1, :] = y

    _project_and_norm(zbuf, wout_ref[0], lambda r0, r1: x_ref[0, r0:r1, :], 1.0,
                      g_ref[layer, ln:ln + 1, :], b_ref[layer, ln:ln + 1, :], write_rows, tm)


def _mix_prompt(x, sinks, win, cw, wout, g, b, layer, ln, tm, casts=()):
    bsz, seq, _ = x.shape
    tps = seq // tm
    kern = functools.partial(_mix_prompt_kernel, tm=tm, layer=layer, ln=ln, n_cast=len(casts))
    per_seq = lambda bi, i, s: (bi, 0, 0)
    c_in, c_out, c_shapes, c_args = _cast_specs(casts, bsz * tps, lambda bi, i, *_: bi * tps + i)
    grid_spec = pltpu.PrefetchScalarGridSpec(
        num_scalar_prefetch=1,
        grid=(bsz, tps),
        in_specs=[
            pl.BlockSpec((1, tm, D_MODEL), lambda bi, i, s: (bi, i, 0)),
            _weight((D_MODEL, IN_COLS)),
            _resident((DEPTH, CONV_W, D_CONV)),
            _weight((D_CONV + D_ATTN, D_MODEL)),
        ] + _ln_specs() + c_in,
        out_specs=[
            pl.BlockSpec((1, tm, D_MODEL), lambda bi, i, s: (bi, i, 0)),
            pl.BlockSpec((1, WINDOW, D_KV), per_seq),
            pl.BlockSpec((1, WINDOW, D_KV), per_seq),
            pl.BlockSpec((1, CONV_W - 1, D_CONV), per_seq),
        ] + c_out,
        scratch_shapes=[
            pltpu.VMEM((tm + SUBLANES, D_CONV), F32),
            pltpu.VMEM((tm, D_ATTN), BF16),
            pltpu.VMEM((tm + WINDOW, LANES), BF16),
            pltpu.VMEM((tm + WINDOW, LANES), BF16),
            pltpu.VMEM((tm + WINDOW, LANES), BF16),
            pltpu.VMEM((tm + WINDOW, LANES), BF16),
            pltpu.VMEM((tm, D_CONV + D_ATTN), BF16),
        ],
    )
    return pl.pallas_call(
        kern,
        grid_spec=grid_spec,
        out_shape=[
            jax.ShapeDtypeStruct((bsz, seq, D_MODEL), F32),
            jax.ShapeDtypeStruct((bsz, WINDOW, D_KV), F32),
            jax.ShapeDtypeStruct((bsz, WINDOW, D_KV), F32),
            jax.ShapeDtypeStruct((bsz, CONV_W - 1, D_CONV), F32),
        ] + c_shapes,
        compiler_params=_params("arbitrary", "arbitrary"),
        name="mix_prompt",
    )(sinks, x, win, cw, wout, g, b, *c_args)


def _memkv_kernel(m_ref, wk_ref, wv_ref, k_ref, v_ref, kb_ref, vb_ref):
    mb = m_ref[...].astype(BF16)
    k = _dot(mb, wk_ref[0])
    v = _dot(mb, wv_ref[0])
    k_ref[...] = k
    v_ref[...] = v
    kb_ref[...] = k.astype(BF16)
    vb_ref[...] = v.astype(BF16)


def _memkv(mem, wk, wv):
    m = mem.shape[0]
    blk = pl.BlockSpec((MEM_LEN, D_MODEL), lambda i: (i, 0))
    return pl.pallas_call(
        _memkv_kernel,
        grid=(m // MEM_LEN,),
        in_specs=[blk, _weight((D_MODEL, D_MODEL)), _weight((D_MODEL, D_MODEL))],
        out_specs=[blk, blk, blk, blk],
        out_shape=[jax.ShapeDtypeStruct((m, D_MODEL), F32), jax.ShapeDtypeStruct((m, D_MODEL), F32),
                   jax.ShapeDtypeStruct((m, D_MODEL), BF16), jax.ShapeDtypeStruct((m, D_MODEL), BF16)],
        compiler_params=_params("arbitrary"),
        name="mem_kv",
    )(mem, wk, wv)


def _cross_prompt_kernel(x_ref, mk_ref, mv_ref, wq_ref, wo_ref, g_ref, b_ref, *rest, layer, ln, n_cast):
    cast_src, o_ref, cast_dst, obuf = rest[:n_cast], rest[n_cast], rest[n_cast + 1:2 * n_cast + 1], rest[-1]
    _run_casts(cast_src, cast_dst)
    q = _dot(x_ref[0].astype(BF16), wq_ref[0]) * (MEM_HEAD_DIM ** -0.5)
    for h in range(MEM_HEADS):
        c0 = h * MEM_HEAD_DIM
        s = _dot_nt(q[:, c0:c0 + MEM_HEAD_DIM].astype(BF16), mk_ref[0, :, c0:c0 + MEM_HEAD_DIM])
        m = jnp.max(s, axis=-1, keepdims=True)
        p = jnp.exp(s - m)
        denom = jnp.sum(p, axis=-1, keepdims=True)
        o = _dot(p.astype(BF16), mv_ref[0, :, c0:c0 + MEM_HEAD_DIM]) / denom
        obuf[:, c0:c0 + MEM_HEAD_DIM] = o.astype(BF16)

    def write_rows(r0, r1, y):
        o_ref[0, r0:r1, :] = y

    _project_and_norm(obuf, wo_ref[0], lambda r0, r1: x_ref[0, r0:r1, :], 1.0,
                      g_ref[layer, ln:ln + 1, :], b_ref[layer, ln:ln + 1, :], write_rows, x_ref.shape[1])


def _cross_prompt(x, mk, mv, wq, wo, g, b, layer, ln, tm, casts=()):
    bsz, seq, _ = x.shape
    tps = seq // tm
    xblk = pl.BlockSpec((1, tm, D_MODEL), lambda bi, i: (bi, i, 0))
    mblk = pl.BlockSpec((1, MEM_LEN, D_MODEL), lambda bi, i: (bi, 0, 0))
    c_in, c_out, c_shapes, c_args = _cast_specs(casts, bsz * tps, lambda bi, i: bi * tps + i)
    return pl.pallas_call(
        functools.partial(_cross_prompt_kernel, layer=layer, ln=ln, n_cast=len(casts)),
        grid=(bsz, tps),
        in_specs=[xblk, mblk, mblk, _weight((D_MODEL, D_MODEL)), _weight((D_MODEL, D_MODEL))] + _ln_specs() + c_in,
        out_specs=[xblk] + c_out,
        out_shape=[jax.ShapeDtypeStruct((bsz, seq, D_MODEL), F32)] + c_shapes,
        scratch_shapes=[pltpu.VMEM((tm, D_MODEL), BF16)],
        compiler_params=_params("arbitrary", "arbitrary"),
        name="cross_prompt",
    )(x, mk, mv, wq, wo, g, b, *c_args)


def _bdot(a, b, contract_b):
    return lax.dot_general(a, b, (((2,), (contract_b,)), ((0,), (0,))), preferred_element_type=F32)


def _mix_sample_kernel(sink_ref, x_ref, st_ref, kc_ref, vc_ref, win_ref, cw_ref, wout_ref, g_ref, b_ref,
                       *rest, bb, layer, ln):
    o_ref, nst_ref, nk_ref, nv_ref, zbuf, qm_scr, kn_scr, vn_scr, o_scr = rest[-9:]
    x = x_ref[...]
    xb = x.astype(BF16)

    def proj(col, width):
        return _dot(xb, win_ref[0, :, col:col + width])

    u = proj(_C_CG, D_CONV) * proj(_C_HC, D_CONV)
    st0 = st_ref[:, 0:D_CONV]
    st1 = st_ref[:, D_CONV:2 * D_CONV]
    conv = cw_ref[layer, 0:1, :] * st0 + cw_ref[layer, 1:2, :] * st1 + cw_ref[layer, 2:3, :] * u
    zbuf[:, 0:D_CONV] = (proj(_C_BG, D_CONV) * conv).astype(BF16)
    nst_ref[:, 0:D_CONV] = st1
    nst_ref[:, D_CONV:2 * D_CONV] = u

    q = proj(_C_Q, D_ATTN) * (HEAD_DIM ** -0.5)
    k = proj(_C_KV, D_KV)
    v = proj(_C_KV + D_KV, D_KV)

    pad = jnp.zeros((LANES - bb, D_KV), F32)
    kt = jnp.concatenate([k, pad], axis=0).T if bb < LANES else k.T
    vt = jnp.concatenate([v, pad], axis=0).T if bb < LANES else v.T
    last = lax.broadcasted_iota(jnp.int32, (D_KV, WINDOW), 1) == WINDOW - 1
    slab = layer if nk_ref.shape[0] > 1 else 0
    for other in range(nk_ref.shape[0]):
        if other != slab:
            nk_ref[other] = jnp.zeros(nk_ref.shape[1:], F32)
            nv_ref[other] = jnp.zeros(nv_ref.shape[1:], F32)
    for r in range(bb):
        nk_ref[slab, r] = jnp.where(last, kt[:, r:r + 1], pltpu.roll(kc_ref[0, r], WINDOW - 1, 1))
        nv_ref[slab, r] = jnp.where(last, vt[:, r:r + 1], pltpu.roll(vc_ref[0, r], WINDOW - 1, 1))

    lane = lax.broadcasted_iota(jnp.int32, (bb, LANES), 1)
    low = lane < HEAD_DIM
    for h in range(N_HEADS):
        kvh = h // GROUP
        qt = q[:, (h // 2) * LANES:(h // 2 + 1) * LANES]
        if (h % 2) != kvh:
            qt = pltpu.roll(qt, HEAD_DIM, 1)
        keep = low if kvh == 0 else jnp.logical_not(low)
        qm_scr[:, h, :] = jnp.where(keep, qt, 0.0)
        kn_scr[:, h, :] = k
        vn_scr[:, h, :] = v
    qm = qm_scr[...]

    s = _bdot(qm.astype(BF16), kc_ref[0].astype(BF16), 1)
    s_new = jnp.sum(qm * kn_scr[...], axis=-1, keepdims=True)
    hrow = lax.broadcasted_iota(jnp.int32, (1, N_HEADS, 1), 1)
    sink = jnp.full((1, N_HEADS, 1), sink_ref[layer, N_HEADS - 1], F32)
    for h in range(N_HEADS - 1):
        sink = jnp.where(hrow == h, sink_ref[layer, h], sink)
    m = jnp.maximum(jnp.maximum(jnp.max(s, axis=-1, keepdims=True), s_new), sink)
    p = jnp.exp(s - m)
    p_new = jnp.exp(s_new - m)
    denom = jnp.sum(p, axis=-1, keepdims=True) + p_new + jnp.exp(sink - m)
    o_scr[...] = (_bdot(p.astype(BF16), vc_ref[0].astype(BF16), 2) + p_new * vn_scr[...]) / denom

    for t in range(D_ATTN // LANES):
        oa = o_scr[:, 2 * t, :]
        ob = o_scr[:, 2 * t + 1, :]
        if (2 * t) // GROUP == 0:
            z = jnp.where(low, oa, pltpu.roll(ob, HEAD_DIM, 1))
        else:
            z = jnp.where(low, pltpu.roll(oa, HEAD_DIM, 1), ob)
        zbuf[:, D_CONV + t * LANES:D_CONV + (t + 1) * LANES] = z.astype(BF16)

    y = _dot(zbuf[...], wout_ref[0])
    o_ref[...] = _layer_norm(ALPHA * x + y, g_ref[layer, ln:ln + 1, :], b_ref[layer, ln:ln + 1, :])


def _mix_sample(x, sinks, st, kc, vc, prev_windows, win, cw, wout, g, b, layer, ln, bb):
    nb = x.shape[0]
    kern = functools.partial(_mix_sample_kernel, bb=bb, layer=layer, ln=ln)
    row = lambda i, s: (i, 0)
    win_blk = pl.BlockSpec((1, bb, D_KV, WINDOW), lambda i, s: (layer, i, 0, 0))
    in_specs = [
        pl.BlockSpec((bb, D_MODEL), row),
        pl.BlockSpec((bb, (CONV_W - 1) * D_CONV), row),
        win_blk,
        win_blk,
        _weight((D_MODEL, IN_COLS)),
        _resident((DEPTH, CONV_W, D_CONV)),
        _weight((D_CONV + D_ATTN, D_MODEL)),
    ] + _ln_specs()
    args = [sinks, x, st, kc, vc, win, cw, wout, g, b]
    aliases = {}
    out_win_blk = win_blk
    if prev_windows is None:
        out_win_blk = pl.BlockSpec((DEPTH, bb, D_KV, WINDOW), lambda i, s: (0, i, 0, 0))
    else:
        n_in = len(args)
        in_specs += [pl.BlockSpec(memory_space=pl.ANY)] * 2
        args += list(prev_windows)
        aliases = {n_in: 2, n_in + 1: 3}
    grid_spec = pltpu.PrefetchScalarGridSpec(
        num_scalar_prefetch=1,
        grid=(nb // bb,),
        in_specs=in_specs,
        out_specs=[pl.BlockSpec((bb, D_MODEL), row), pl.BlockSpec((bb, (CONV_W - 1) * D_CONV), row),
                   out_win_blk, out_win_blk],
        scratch_shapes=[pltpu.VMEM((bb, D_CONV + D_ATTN), BF16)]
        + [pltpu.VMEM((bb, N_HEADS, LANES), F32)] * 4,
    )
    return pl.pallas_call(
        kern,
        grid_spec=grid_spec,
        out_shape=[
            jax.ShapeDtypeStruct((nb, D_MODEL), F32),
            jax.ShapeDtypeStruct((nb, (CONV_W - 1) * D_CONV), F32),
            jax.ShapeDtypeStruct((DEPTH, nb, D_KV, WINDOW), F32),
            jax.ShapeDtypeStruct((DEPTH, nb, D_KV, WINDOW), F32),
        ],
        input_output_aliases=aliases,
        compiler_params=_params("arbitrary"),
        name="mix_sample",
    )(*args)


_MEM_HALVES = MEM_HEAD_DIM // LANES
_MEM_ROWS = _MEM_HALVES * MEM_HEADS
_CROSS_SAMPLE_UNROLL = 4


def _cross_sample_kernel(x_ref, mk_ref, mv_ref, wq_ref, wo_ref, g_ref, b_ref, o_ref, q8, o8, obuf,
                         *, bb, layer, ln):
    x = x_ref[...]
    q = _dot(x.astype(BF16), wq_ref[0]) * (MEM_HEAD_DIM ** -0.5)
    for half in range(_MEM_HALVES):
        for h in range(MEM_HEADS):
            c0 = h * MEM_HEAD_DIM + half * LANES
            q8[:, half * MEM_HEADS + h, :] = q[:, c0:c0 + LANES]

    def one_sequence(r, carry):
        kq = mk_ref[0, r].reshape(MEM_LEN, _MEM_ROWS, LANES) * q8[r][None]
        kq = kq + pltpu.roll(kq, MEM_HEADS, 1)
        s = jnp.sum(kq, axis=-1, keepdims=True)
        m = jnp.max(s, axis=0, keepdims=True)
        p = jnp.exp(s - m)
        denom = jnp.sum(p, axis=0, keepdims=True)
        pv = p * mv_ref[0, r].reshape(MEM_LEN, _MEM_ROWS, LANES)
        o8[r] = jnp.sum(pv, axis=0) / denom[0]
        return carry

    lax.fori_loop(0, bb, one_sequence, 0, unroll=_CROSS_SAMPLE_UNROLL)
    for half in range(_MEM_HALVES):
        for h in range(MEM_HEADS):
            c0 = h * MEM_HEAD_DIM + half * LANES
            obuf[:, c0:c0 + LANES] = o8[:, half * MEM_HEADS + h, :].astype(BF16)
    y = _dot(obuf[...], wo_ref[0])
    o_ref[...] = _layer_norm(ALPHA * x + y, g_ref[layer, ln:ln + 1, :], b_ref[layer, ln:ln + 1, :])


def _cross_sample(x, mk, mv, wq, wo, g, b, layer, ln, bb):
    nb = x.shape[0]
    kern = functools.partial(_cross_sample_kernel, bb=bb, layer=layer, ln=ln)
    xblk = pl.BlockSpec((bb, D_MODEL), lambda i: (i, 0))
    mblk = pl.BlockSpec((1, bb, MEM_LEN * _MEM_ROWS, LANES), lambda i: (layer, i, 0, 0))
    return pl.pallas_call(
        kern,
        grid=(nb // bb,),
        in_specs=[xblk, mblk, mblk, _weight((D_MODEL, D_MODEL)), _weight((D_MODEL, D_MODEL))] + _ln_specs(),
        out_specs=xblk,
        out_shape=jax.ShapeDtypeStruct((nb, D_MODEL), F32),
        scratch_shapes=[pltpu.VMEM((bb, _MEM_ROWS, LANES), F32),
                        pltpu.VMEM((bb, _MEM_ROWS, LANES), F32),
                        pltpu.VMEM((bb, D_MODEL), BF16)],
        compiler_params=_params("arbitrary"),
        name="cross_sample",
    )(x, mk, mv, wq, wo, g, b)


TM_PROMPT = 1024
BB_MIX_SAMPLE = 32
BB_CROSS_SAMPLE = 8


def kernel(x_prompt, x_sample, mem_prompt, cache_win_k, cache_win_v, state_conv, cache_mem_k, cache_mem_v,
           ln_g, ln_b, ffn1_w_gu, ffn1_w_down, w_in, conv_w, attn_sinks, w_out,
           w_cq, w_mk, w_mv, w_co, ffn2_w_gu, ffn2_w_down):
    bsz, seq, _ = x_prompt.shape
    nsmp = x_sample.shape[0]
    yp = x_prompt
    ys = x_sample.reshape(nsmp, D_MODEL)
    mem2 = mem_prompt.reshape(bsz * MEM_LEN, D_MODEL)

    stacked = dict(ffn1=(ffn1_w_gu, ffn1_w_down), ffn2=(ffn2_w_gu, ffn2_w_down),
                   mix=(w_in, w_out), cross=(w_cq, w_co, w_mk, w_mv))
    wb = {("ffn1", 0): tuple(w[0:1].astype(BF16) for w in stacked["ffn1"])}

    def jobs(*groups):
        return [(w, layer) for name, layer in groups for w in stacked[name]]

    def keep(groups, casts):
        casts = list(casts)
        for name, layer in groups:
            wb[(name, layer)] = tuple(casts[:len(stacked[name])])
            del casts[:len(stacked[name])]

    def token_tiles(c):
        c = c.reshape(DEPTH, nsmp, MEM_LEN, MEM_HEADS, _MEM_HALVES, LANES)
        return c.transpose(0, 1, 2, 4, 3, 5).reshape(DEPTH, nsmp, MEM_LEN * _MEM_ROWS, LANES)

    mem_k_tiles, mem_v_tiles = token_tiles(cache_mem_k), token_tiles(cache_mem_v)

    def window_lanes(c):
        return c.transpose(0, 1, 3, 4, 2).reshape(DEPTH, nsmp, D_KV, WINDOW)

    win_k_lanes, win_v_lanes = window_lanes(cache_win_k), window_lanes(cache_win_v)
    new_windows = None

    wkp, wvp, cvp, mkp, mvp, cvs = [], [], [], [], [], []
    for l in range(DEPTH):
        more = l + 1 < DEPTH
        side = [("mix", 0), ("cross", 0)] if l == 0 else [("ffn2", l)]
        yp, ys, *casts = _ffn(yp.reshape(bsz * seq, D_MODEL), ys, *wb[("ffn1", l)], ln_g, ln_b, l, 0, TM_PROMPT,
                              jobs(*side))
        keep(side, casts)
        yp = yp.reshape(bsz, seq, D_MODEL)

        win, wout = wb[("mix", l)]
        side = [("ffn2", 0)] if l == 0 else []
        yp, nk, nv, ncv, *casts = _mix_prompt(yp, attn_sinks, win, conv_w, wout, ln_g, ln_b, l, 1, TM_PROMPT,
                                              jobs(*side))
        keep(side, casts)
        wkp.append(nk.reshape(bsz, WINDOW, N_KV_HEADS, HEAD_DIM))
        wvp.append(nv.reshape(bsz, WINDOW, N_KV_HEADS, HEAD_DIM))
        cvp.append(ncv)
        ys, nst, nks, nvs = _mix_sample(
            ys, attn_sinks, state_conv[l].reshape(nsmp, (CONV_W - 1) * D_CONV),
            win_k_lanes, win_v_lanes, new_windows, win, conv_w, wout, ln_g, ln_b, l, 1, BB_MIX_SAMPLE)
        new_windows = (nks, nvs)
        cvs.append(nst.reshape(nsmp, CONV_W - 1, D_CONV))

        wcq, wco, wmk, wmv = wb[("cross", l)]
        mk, mv, mkb, mvb = _memkv(mem2, wmk, wmv)
        mkp.append(mk.reshape(bsz, MEM_LEN, MEM_HEADS, MEM_HEAD_DIM))
        mvp.append(mv.reshape(bsz, MEM_LEN, MEM_HEADS, MEM_HEAD_DIM))
        side = [("ffn1", l + 1)] if more else []
        yp, *casts = _cross_prompt(yp, mkb.reshape(bsz, MEM_LEN, D_MODEL), mvb.reshape(bsz, MEM_LEN, D_MODEL),
                                   wcq, wco, ln_g, ln_b, l, 2, TM_PROMPT, jobs(*side))
        keep(side, casts)
        ys = _cross_sample(ys, mem_k_tiles, mem_v_tiles, wcq, wco, ln_g, ln_b, l, 2, BB_CROSS_SAMPLE)

        side = [("mix", l + 1), ("cross", l + 1)] if more else []
        yp, ys, *casts = _ffn(yp.reshape(bsz * seq, D_MODEL), ys, *wb[("ffn2", l)], ln_g, ln_b, l, 3, TM_PROMPT,
                              jobs(*side))
        keep(side, casts)
        yp = yp.reshape(bsz, seq, D_MODEL)

    def window_rows(c):
        return c.reshape(DEPTH, nsmp, N_KV_HEADS, HEAD_DIM, WINDOW).transpose(0, 1, 4, 2, 3)

    return (yp, ys.reshape(nsmp, 1, D_MODEL),
            jnp.stack(wkp), jnp.stack(wvp), jnp.stack(cvp), jnp.stack(mkp), jnp.stack(mvp),
            window_rows(new_windows[0]), window_rows(new_windows[1]), jnp.stack(cvs))
```

```python
import functools

import jax
import jax.numpy as jnp
from jax import lax
from jax.experimental import pallas as pl
from jax.experimental.pallas import tpu as pltpu
from jax.experimental.pallas import tpu_sc as plsc

D_MODEL = 1024
DEPTH = 2
D_CONV = 512
CONV_W = 3
HEAD_DIM = 64
N_HEADS = 8
N_KV_HEADS = 2
GROUP = N_HEADS // N_KV_HEADS
WINDOW = 128
D_ATTN = N_HEADS * HEAD_DIM
D_KV = N_KV_HEADS * HEAD_DIM
IN_COLS = 3 * D_CONV + D_ATTN + 2 * D_KV
MEM_LEN = 256
MEM_HEADS = 4
MEM_HEAD_DIM = D_MODEL // MEM_HEADS
D_FF = 2816
N_LN = 4
ALPHA = (2.0 * DEPTH) ** 0.25
LN_EPS = 1e-5

_C_BG, _C_CG, _C_HC, _C_Q, _C_KV = 0, D_CONV, 2 * D_CONV, 3 * D_CONV, 3 * D_CONV + D_ATTN

LANES = 128
SUBLANES = 8
MXU_COLS = 256
VMEM_LIMIT_BYTES = 56 * 1024 * 1024

BF16 = jnp.bfloat16
F32 = jnp.float32


def _dot(a, b):
    return jnp.dot(a, b, preferred_element_type=F32)


def _dot_nt(a, b):
    return lax.dot_general(a, b, (((1,), (1,)), ((), ())), preferred_element_type=F32)


def _layer_norm(z, g, b):
    mu = jnp.mean(z, axis=-1, keepdims=True)
    d = z - mu
    var = jnp.mean(d * d, axis=-1, keepdims=True)
    return d * lax.rsqrt(var + LN_EPS) * g + b


def _resident(shape):
    zeros = (0,) * len(shape)
    return pl.BlockSpec(shape, lambda *_: zeros, pipeline_mode=pl.Buffered(1))


def _weight(shape):
    return _resident((1,) + shape)


def _cast_specs(jobs, n_steps, step_of):
    in_specs, out_specs, out_shapes, args = [], [], [], []
    for w, layer in jobs:
        _, k, n = w.shape
        rows = k // n_steps
        assert rows * n_steps == k and rows % (2 * SUBLANES) == 0, (w.shape, n_steps)
        in_specs.append(pl.BlockSpec((1, rows, n), lambda *idx, layer=layer: (layer, step_of(*idx), 0)))
        out_specs.append(pl.BlockSpec((1, rows, n), lambda *idx: (0, step_of(*idx), 0)))
        out_shapes.append(jax.ShapeDtypeStruct((1, k, n), BF16))
        args.append(w)
    return in_specs, out_specs, out_shapes, args


def _run_casts(src_refs, dst_refs):
    for src, dst in zip(src_refs, dst_refs, strict=True):
        dst[...] = src[...].astype(BF16)


def _ln_specs():
    return [_resident((DEPTH, N_LN, D_MODEL)), _resident((DEPTH, N_LN, D_MODEL))]


def _params(*sem):
    return pltpu.CompilerParams(dimension_semantics=sem, vmem_limit_bytes=VMEM_LIMIT_BYTES)


NORM_ROWS = 128


def _project_and_norm(lhs_ref, w, x_rows, scale, g, b, write_rows, rows):
    for r0 in range(0, rows, min(NORM_ROWS, rows)):
        r1 = r0 + min(NORM_ROWS, rows)
        y = _dot(lhs_ref[r0:r1, :], w)
        write_rows(r0, r1, _layer_norm(ALPHA * x_rows(r0, r1) + scale * y, g, b))


def _ffn_rows(x_ref, o_ref, wgu_ref, wd_ref, h_ref, g, b):
    rows = x_ref.shape[0]
    xb = x_ref[...].astype(BF16)
    for c in range(D_FF // MXU_COLS):
        lo = c * MXU_COLS
        gate = _dot(xb, wgu_ref[0, :, lo:lo + MXU_COLS])
        up = _dot(xb, wgu_ref[0, :, D_FF + lo:D_FF + lo + MXU_COLS])
        h_ref[0:rows, lo:lo + MXU_COLS] = (gate * jax.nn.sigmoid(gate) * up).astype(BF16)

    def write_rows(r0, r1, y):
        o_ref[r0:r1, :] = y

    _project_and_norm(h_ref, wd_ref[0], lambda r0, r1: x_ref[r0:r1, :], 0.5, g, b, write_rows, rows)


def _ffn_kernel(x_ref, xs_ref, wgu_ref, wd_ref, g_ref, b_ref, *rest, layer, ln, n_cast, n_tiles):
    cast_src, (o_ref, os_ref), rest = rest[:n_cast], rest[n_cast:n_cast + 2], rest[n_cast + 2:]
    cast_dst, h_ref = rest[:n_cast], rest[-1]
    g = g_ref[layer, ln:ln + 1, :]
    b = b_ref[layer, ln:ln + 1, :]
    step = pl.program_id(0)

    @pl.when(step < n_tiles)
    def _():
        _run_casts(cast_src, cast_dst)
        _ffn_rows(x_ref, o_ref, wgu_ref, wd_ref, h_ref, g, b)

    @pl.when(step == n_tiles)
    def _():
        _ffn_rows(xs_ref, os_ref, wgu_ref, wd_ref, h_ref, g, b)


def _ffn(x, xs, wgu, wd, g, b, layer, ln, tm, casts=()):
    m = x.shape[0]
    ns = xs.shape[0]
    n_tiles = m // tm
    tile = lambda i: (jnp.minimum(i, n_tiles - 1), 0)
    c_in, c_out, c_shapes, c_args = _cast_specs(casts, n_tiles, lambda i: jnp.minimum(i, n_tiles - 1))
    return pl.pallas_call(
        functools.partial(_ffn_kernel, layer=layer, ln=ln, n_cast=len(casts), n_tiles=n_tiles),
        grid=(n_tiles + 1,),
        in_specs=[
            pl.BlockSpec((tm, D_MODEL), tile),
            _resident((ns, D_MODEL)),
            _weight((D_MODEL, 2 * D_FF)),
            _weight((D_FF, D_MODEL)),
        ] + _ln_specs() + c_in,
        out_specs=[pl.BlockSpec((tm, D_MODEL), tile), pl.BlockSpec((ns, D_MODEL), lambda i: (0, 0))] + c_out,
        out_shape=[jax.ShapeDtypeStruct((m, D_MODEL), F32), jax.ShapeDtypeStruct((ns, D_MODEL), F32)] + c_shapes,
        scratch_shapes=[pltpu.VMEM((tm, D_FF), BF16)],
        compiler_params=_params("arbitrary"),
        name="ffn_ln",
    )(x, xs, wgu, wd, g, b, *c_args)


def _dup_halves(a):
    lane = lax.broadcasted_iota(jnp.int32, a.shape, 1)
    low = lane < HEAD_DIM
    rolled = pltpu.roll(a, HEAD_DIM, 1)
    return jnp.where(low, a, rolled), jnp.where(low, rolled, a)


def _sink_column(sink_ref, layer, first_head, rows_per_head, n_heads):
    rows = n_heads * rows_per_head
    r = lax.broadcasted_iota(jnp.int32, (rows, 1), 0)
    col = jnp.full((rows, 1), sink_ref[layer, first_head + n_heads - 1], F32)
    for g in range(n_heads - 2, -1, -1):
        col = jnp.where(r < (g + 1) * rows_per_head, sink_ref[layer, first_head + g], col)
    return col


def _mix_prompt_kernel(sink_ref, x_ref, win_ref, cw_ref, wout_ref, g_ref, b_ref, *rest, tm, layer, ln, n_cast):
    cast_src, rest = rest[:n_cast], rest[n_cast:]
    (o_ref, nk_ref, nv_ref, nc_ref), rest = rest[:4], rest[4:]
    cast_dst, (ubuf, qbuf, kd0, kd1, vd0, vd1, zbuf) = rest[:n_cast], rest[n_cast:]
    _run_casts(cast_src, cast_dst)
    seq_start = pl.program_id(1) == 0
    xb = x_ref[0].astype(BF16)

    def proj(col, width):
        return _dot(xb, win_ref[0, :, col:col + width])

    @pl.when(seq_start)
    def _():
        ubuf[0:SUBLANES, :] = jnp.zeros((SUBLANES, D_CONV), F32)
        for r in (kd0, kd1, vd0, vd1):
            r[0:WINDOW, :] = jnp.zeros((WINDOW, LANES), BF16)

    u = proj(_C_CG, D_CONV) * proj(_C_HC, D_CONV)
    ubuf[SUBLANES:SUBLANES + tm, :] = u
    conv = (cw_ref[layer, 0:1, :] * ubuf[SUBLANES - 2:SUBLANES - 2 + tm, :]
            + cw_ref[layer, 1:2, :] * ubuf[SUBLANES - 1:SUBLANES - 1 + tm, :]
            + cw_ref[layer, 2:3, :] * u)
    zbuf[:, 0:D_CONV] = (proj(_C_BG, D_CONV) * conv).astype(BF16)
    nc_ref[0] = u[tm - (CONV_W - 1):tm, :]
    ubuf[0:SUBLANES, :] = u[tm - SUBLANES:tm, :]

    qbuf[...] = (proj(_C_Q, D_ATTN) * (HEAD_DIM ** -0.5)).astype(BF16)
    k = proj(_C_KV, D_KV)
    v = proj(_C_KV + D_KV, D_KV)
    nk_ref[0] = k[tm - WINDOW:tm, :]
    nv_ref[0] = v[tm - WINDOW:tm, :]
    ka, kb = _dup_halves(k)
    va, vb = _dup_halves(v)
    kd0[WINDOW:WINDOW + tm, :] = ka.astype(BF16)
    kd1[WINDOW:WINDOW + tm, :] = kb.astype(BF16)
    vd0[WINDOW:WINDOW + tm, :] = va.astype(BF16)
    vd1[WINDOW:WINDOW + tm, :] = vb.astype(BF16)

    rows = GROUP * WINDOW
    a_idx = lax.broadcasted_iota(jnp.int32, (rows, 2 * WINDOW), 0) % WINDOW
    c_idx = lax.broadcasted_iota(jnp.int32, (rows, 2 * WINDOW), 1)
    band = (c_idx >= a_idx) & (c_idx <= a_idx + WINDOW)
    first_lb = jnp.where(seq_start, WINDOW, 0)
    low = lax.broadcasted_iota(jnp.int32, (WINDOW, LANES), 1) < HEAD_DIM
    zero = jnp.zeros((WINDOW, LANES), BF16)

    for n in range(tm // WINDOW):
        r0 = n * WINDOW
        mask = (band & (c_idx >= first_lb)) if n == 0 else band
        for j, (kd, vd) in enumerate(((kd0, vd0), (kd1, vd1))):
            qa = qbuf[r0:r0 + WINDOW, (2 * j) * LANES:(2 * j + 1) * LANES]
            qb = qbuf[r0:r0 + WINDOW, (2 * j + 1) * LANES:(2 * j + 2) * LANES]
            qs = jnp.concatenate([jnp.where(low, qa, zero), jnp.where(low, zero, qa),
                                  jnp.where(low, qb, zero), jnp.where(low, zero, qb)], axis=0)
            s = _dot_nt(qs, kd[r0:r0 + 2 * WINDOW, :])
            s = jnp.where(mask, s, -jnp.inf)
            sink = _sink_column(sink_ref, layer, GROUP * j, WINDOW, GROUP)
            m = jnp.maximum(jnp.max(s, axis=-1, keepdims=True), sink)
            p = jnp.exp(s - m)
            denom = jnp.sum(p, axis=-1, keepdims=True) + jnp.exp(sink - m)
            o = _dot(p.astype(BF16), vd[r0:r0 + 2 * WINDOW, :]) / denom
            za = jnp.where(low, o[0:WINDOW], o[WINDOW:2 * WINDOW])
            zb = jnp.where(low, o[2 * WINDOW:3 * WINDOW], o[3 * WINDOW:4 * WINDOW])
            c0 = D_CONV + (2 * j) * LANES
            zbuf[r0:r0 + WINDOW, c0:c0 + LANES] = za.astype(BF16)
            zbuf[r0:r0 + WINDOW, c0 + LANES:c0 + 2 * LANES] = zb.astype(BF16)

    for r in (kd0, kd1, vd0, vd1):
        r[0:WINDOW, :] = r[tm:tm + WINDOW, :]

    def write_rows(r0, r1, y):
        o_ref[0, r0:r1, :] = y

    _project_and_norm(zbuf, wout_ref[0], lambda r0, r1: x_ref[0, r0:r1, :], 1.0,
                      g_ref[layer, ln:ln + 1, :], b_ref[layer, ln:ln + 1, :], write_rows, tm)


def _mix_prompt(x, sinks, win, cw, wout, g, b, layer, ln, tm, casts=()):
    bsz, seq, _ = x.shape
    tps = seq // tm
    kern = functools.partial(_mix_prompt_kernel, tm=tm, layer=layer, ln=ln, n_cast=len(casts))
    per_seq = lambda bi, i, s: (bi, 0, 0)
    c_in, c_out, c_shapes, c_args = _cast_specs(casts, bsz * tps, lambda bi, i, *_: bi * tps + i)
    grid_spec = pltpu.PrefetchScalarGridSpec(
        num_scalar_prefetch=1,
        grid=(bsz, tps),
        in_specs=[
            pl.BlockSpec((1, tm, D_MODEL), lambda bi, i, s: (bi, i, 0)),
            _weight((D_MODEL, IN_COLS)),
            _resident((DEPTH, CONV_W, D_CONV)),
            _weight((D_CONV + D_ATTN, D_MODEL)),
        ] + _ln_specs() + c_in,
        out_specs=[
            pl.BlockSpec((1, tm, D_MODEL), lambda bi, i, s: (bi, i, 0)),
            pl.BlockSpec((1, WINDOW, D_KV), per_seq),
            pl.BlockSpec((1, WINDOW, D_KV), per_seq),
            pl.BlockSpec((1, CONV_W - 1, D_CONV), per_seq),
        ] + c_out,
        scratch_shapes=[
            pltpu.VMEM((tm + SUBLANES, D_CONV), F32),
            pltpu.VMEM((tm, D_ATTN), BF16),
            pltpu.VMEM((tm + WINDOW, LANES), BF16),
            pltpu.VMEM((tm + WINDOW, LANES), BF16),
            pltpu.VMEM((tm + WINDOW, LANES), BF16),
            pltpu.VMEM((tm + WINDOW, LANES), BF16),
            pltpu.VMEM((tm, D_CONV + D_ATTN), BF16),
        ],
    )
    return pl.pallas_call(
        kern,
        grid_spec=grid_spec,
        out_shape=[
            jax.ShapeDtypeStruct((bsz, seq, D_MODEL), F32),
            jax.ShapeDtypeStruct((bsz, WINDOW, D_KV), F32),
            jax.ShapeDtypeStruct((bsz, WINDOW, D_KV), F32),
            jax.ShapeDtypeStruct((bsz, CONV_W - 1, D_CONV), F32),
        ] + c_shapes,
        compiler_params=_params("arbitrary", "arbitrary"),
        name="mix_prompt",
    )(sinks, x, win, cw, wout, g, b, *c_args)


def _memkv_kernel(m_ref, wk_ref, wv_ref, k_ref, v_ref, kb_ref, vb_ref):
    mb = m_ref[...].astype(BF16)
    k = _dot(mb, wk_ref[0])
    v = _dot(mb, wv_ref[0])
    k_ref[...] = k
    v_ref[...] = v
    kb_ref[...] = k.astype(BF16)
    vb_ref[...] = v.astype(BF16)


def _memkv(mem, wk, wv):
    m = mem.shape[0]
    blk = pl.BlockSpec((MEM_LEN, D_MODEL), lambda i: (i, 0))
    return pl.pallas_call(
        _memkv_kernel,
        grid=(m // MEM_LEN,),
        in_specs=[blk, _weight((D_MODEL, D_MODEL)), _weight((D_MODEL, D_MODEL))],
        out_specs=[blk, blk, blk, blk],
        out_shape=[jax.ShapeDtypeStruct((m, D_MODEL), F32), jax.ShapeDtypeStruct((m, D_MODEL), F32),
                   jax.ShapeDtypeStruct((m, D_MODEL), BF16), jax.ShapeDtypeStruct((m, D_MODEL), BF16)],
        compiler_params=_params("arbitrary"),
        name="mem_kv",
    )(mem, wk, wv)


def _cross_prompt_kernel(x_ref, mk_ref, mv_ref, wq_ref, wo_ref, g_ref, b_ref, *rest, layer, ln, n_cast):
    cast_src, o_ref, cast_dst, obuf = rest[:n_cast], rest[n_cast], rest[n_cast + 1:2 * n_cast + 1], rest[-1]
    _run_casts(cast_src, cast_dst)
    q = _dot(x_ref[0].astype(BF16), wq_ref[0]) * (MEM_HEAD_DIM ** -0.5)
    for h in range(MEM_HEADS):
        c0 = h * MEM_HEAD_DIM
        s = _dot_nt(q[:, c0:c0 + MEM_HEAD_DIM].astype(BF16), mk_ref[0, :, c0:c0 + MEM_HEAD_DIM])
        m = jnp.max(s, axis=-1, keepdims=True)
        p = jnp.exp(s - m)
        denom = jnp.sum(p, axis=-1, keepdims=True)
        o = _dot(p.astype(BF16), mv_ref[0, :, c0:c0 + MEM_HEAD_DIM]) / denom
        obuf[:, c0:c0 + MEM_HEAD_DIM] = o.astype(BF16)

    def write_rows(r0, r1, y):
        o_ref[0, r0:r1, :] = y

    _project_and_norm(obuf, wo_ref[0], lambda r0, r1: x_ref[0, r0:r1, :], 1.0,
                      g_ref[layer, ln:ln + 1, :], b_ref[layer, ln:ln + 1, :], write_rows, x_ref.shape[1])


def _cross_prompt(x, mk, mv, wq, wo, g, b, layer, ln, tm, casts=()):
    bsz, seq, _ = x.shape
    tps = seq // tm
    xblk = pl.BlockSpec((1, tm, D_MODEL), lambda bi, i: (bi, i, 0))
    mblk = pl.BlockSpec((1, MEM_LEN, D_MODEL), lambda bi, i: (bi, 0, 0))
    c_in, c_out, c_shapes, c_args = _cast_specs(casts, bsz * tps, lambda bi, i: bi * tps + i)
    return pl.pallas_call(
        functools.partial(_cross_prompt_kernel, layer=layer, ln=ln, n_cast=len(casts)),
        grid=(bsz, tps),
        in_specs=[xblk, mblk, mblk, _weight((D_MODEL, D_MODEL)), _weight((D_MODEL, D_MODEL))] + _ln_specs() + c_in,
        out_specs=[xblk] + c_out,
        out_shape=[jax.ShapeDtypeStruct((bsz, seq, D_MODEL), F32)] + c_shapes,
        scratch_shapes=[pltpu.VMEM((tm, D_MODEL), BF16)],
        compiler_params=_params("arbitrary", "arbitrary"),
        name="cross_prompt",
    )(x, mk, mv, wq, wo, g, b, *c_args)


def _bdot(a, b, contract_b):
    return lax.dot_general(a, b, (((2,), (contract_b,)), ((0,), (0,))), preferred_element_type=F32)


def _mix_sample_kernel(sink_ref, x_ref, st_ref, kc_ref, vc_ref, win_ref, cw_ref, wout_ref, g_ref, b_ref,
                       *rest, bb, layer, ln):
    o_ref, nst_ref, nk_ref, nv_ref, zbuf, qm_scr, kn_scr, vn_scr, o_scr = rest[-9:]
    x = x_ref[...]
    xb = x.astype(BF16)

    def proj(col, width):
        return _dot(xb, win_ref[0, :, col:col + width])

    u = proj(_C_CG, D_CONV) * proj(_C_HC, D_CONV)
    st0 = st_ref[:, 0:D_CONV]
    st1 = st_ref[:, D_CONV:2 * D_CONV]
    conv = cw_ref[layer, 0:1, :] * st0 + cw_ref[layer, 1:2, :] * st1 + cw_ref[layer, 2:3, :] * u
    zbuf[:, 0:D_CONV] = (proj(_C_BG, D_CONV) * conv).astype(BF16)
    nst_ref[:, 0:D_CONV] = st1
    nst_ref[:, D_CONV:2 * D_CONV] = u

    q = proj(_C_Q, D_ATTN) * (HEAD_DIM ** -0.5)
    k = proj(_C_KV, D_KV)
    v = proj(_C_KV + D_KV, D_KV)

    pad = jnp.zeros((LANES - bb, D_KV), F32)
    kt = jnp.concatenate([k, pad], axis=0).T if bb < LANES else k.T
    vt = jnp.concatenate([v, pad], axis=0).T if bb < LANES else v.T
    last = lax.broadcasted_iota(jnp.int32, (D_KV, WINDOW), 1) == WINDOW - 1
    slab = layer if nk_ref.shape[0] > 1 else 0
    for other in range(nk_ref.shape[0]):
        if other != slab:
            nk_ref[other] = jnp.zeros(nk_ref.shape[1:], F32)
            nv_ref[other] = jnp.zeros(nv_ref.shape[1:], F32)
    for r in range(bb):
        nk_ref[slab, r] = jnp.where(last, kt[:, r:r + 1], pltpu.roll(kc_ref[0, r], WINDOW - 1, 1))
        nv_ref[slab, r] = jnp.where(last, vt[:, r:r + 1], pltpu.roll(vc_ref[0, r], WINDOW - 1, 1))

    lane = lax.broadcasted_iota(jnp.int32, (bb, LANES), 1)
    low = lane < HEAD_DIM
    for h in range(N_HEADS):
        kvh = h // GROUP
        qt = q[:, (h // 2) * LANES:(h // 2 + 1) * LANES]
        if (h % 2) != kvh:
            qt = pltpu.roll(qt, HEAD_DIM, 1)
        keep = low if kvh == 0 else jnp.logical_not(low)
        qm_scr[:, h, :] = jnp.where(keep, qt, 0.0)
        kn_scr[:, h, :] = k
        vn_scr[:, h, :] = v
    qm = qm_scr[...]

    s = _bdot(qm.astype(BF16), kc_ref[0].astype(BF16), 1)
    s_new = jnp.sum(qm * kn_scr[...], axis=-1, keepdims=True)
    hrow = lax.broadcasted_iota(jnp.int32, (1, N_HEADS, 1), 1)
    sink = jnp.full((1, N_HEADS, 1), sink_ref[layer, N_HEADS - 1], F32)
    for h in range(N_HEADS - 1):
        sink = jnp.where(hrow == h, sink_ref[layer, h], sink)
    m = jnp.maximum(jnp.maximum(jnp.max(s, axis=-1, keepdims=True), s_new), sink)
    p = jnp.exp(s - m)
    p_new = jnp.exp(s_new - m)
    denom = jnp.sum(p, axis=-1, keepdims=True) + p_new + jnp.exp(sink - m)
    o_scr[...] = (_bdot(p.astype(BF16), vc_ref[0].astype(BF16), 2) + p_new * vn_scr[...]) / denom

    for t in range(D_ATTN // LANES):
        oa = o_scr[:, 2 * t, :]
        ob = o_scr[:, 2 * t + 1, :]
        if (2 * t) // GROUP == 0:
            z = jnp.where(low, oa, pltpu.roll(ob, HEAD_DIM, 1))
        else:
            z = jnp.where(low, pltpu.roll(oa, HEAD_DIM, 1), ob)
        zbuf[:, D_CONV + t * LANES:D_CONV + (t + 1) * LANES] = z.astype(BF16)

    y = _dot(zbuf[...], wout_ref[0])
    o_ref[...] = _layer_norm(ALPHA * x + y, g_ref[layer, ln:ln + 1, :], b_ref[layer, ln:ln + 1, :])


def _mix_sample(x, sinks, st, kc, vc, prev_windows, win, cw, wout, g, b, layer, ln, bb):
    nb = x.shape[0]
    kern = functools.partial(_mix_sample_kernel, bb=bb, layer=layer, ln=ln)
    row = lambda i, s: (i, 0)
    win_blk = pl.BlockSpec((1, bb, D_KV, WINDOW), lambda i, s: (layer, i, 0, 0))
    in_specs = [
        pl.BlockSpec((bb, D_MODEL), row),
        pl.BlockSpec((bb, (CONV_W - 1) * D_CONV), row),
        win_blk,
        win_blk,
        _weight((D_MODEL, IN_COLS)),
        _resident((DEPTH, CONV_W, D_CONV)),
        _weight((D_CONV + D_ATTN, D_MODEL)),
    ] + _ln_specs()
    args = [sinks, x, st, kc, vc, win, cw, wout, g, b]
    aliases = {}
    out_win_blk = win_blk
    if prev_windows is None:
        out_win_blk = pl.BlockSpec((DEPTH, bb, D_KV, WINDOW), lambda i, s: (0, i, 0, 0))
    else:
        n_in = len(args)
        in_specs += [pl.BlockSpec(memory_space=pl.ANY)] * 2
        args += list(prev_windows)
        aliases = {n_in: 2, n_in + 1: 3}
    grid_spec = pltpu.PrefetchScalarGridSpec(
        num_scalar_prefetch=1,
        grid=(nb // bb,),
        in_specs=in_specs,
        out_specs=[pl.BlockSpec((bb, D_MODEL), row), pl.BlockSpec((bb, (CONV_W - 1) * D_CONV), row),
                   out_win_blk, out_win_blk],
        scratch_shapes=[pltpu.VMEM((bb, D_CONV + D_ATTN), BF16)]
        + [pltpu.VMEM((bb, N_HEADS, LANES), F32)] * 4,
    )
    return pl.pallas_call(
        kern,
        grid_spec=grid_spec,
        out_shape=[
            jax.ShapeDtypeStruct((nb, D_MODEL), F32),
            jax.ShapeDtypeStruct((nb, (CONV_W - 1) * D_CONV), F32),
            jax.ShapeDtypeStruct((DEPTH, nb, D_KV, WINDOW), F32),
            jax.ShapeDtypeStruct((DEPTH, nb, D_KV, WINDOW), F32),
        ],
        input_output_aliases=aliases,
        compiler_params=_params("arbitrary"),
        name="mix_sample",
    )(*args)


_MEM_HALVES = MEM_HEAD_DIM // LANES
_MEM_ROWS = _MEM_HALVES * MEM_HEADS
_CROSS_SAMPLE_UNROLL = 4


def _cross_sample_kernel(x_ref, mk_ref, mv_ref, wq_ref, wo_ref, g_ref, b_ref, o_ref, q8, o8, obuf,
                         *, bb, layer, ln):
    x = x_ref[...]
    q = _dot(x.astype(BF16), wq_ref[0]) * (MEM_HEAD_DIM ** -0.5)
    for half in range(_MEM_HALVES):
        for h in range(MEM_HEADS):
            c0 = h * MEM_HEAD_DIM + half * LANES
            q8[:, half * MEM_HEADS + h, :] = q[:, c0:c0 + LANES]

    def one_sequence(r, carry):
        kq = mk_ref[0, r].reshape(MEM_LEN, _MEM_ROWS, LANES) * q8[r][None]
        kq = kq + pltpu.roll(kq, MEM_HEADS, 1)
        s = jnp.sum(kq, axis=-1, keepdims=True)
        m = jnp.max(s, axis=0, keepdims=True)
        p = jnp.exp(s - m)
        denom = jnp.sum(p, axis=0, keepdims=True)
        pv = p * mv_ref[0, r].reshape(MEM_LEN, _MEM_ROWS, LANES)
        o8[r] = jnp.sum(pv, axis=0) / denom[0]
        return carry

    lax.fori_loop(0, bb, one_sequence, 0, unroll=_CROSS_SAMPLE_UNROLL)
    for half in range(_MEM_HALVES):
        for h in range(MEM_HEADS):
            c0 = h * MEM_HEAD_DIM + half * LANES
            obuf[:, c0:c0 + LANES] = o8[:, half * MEM_HEADS + h, :].astype(BF16)
    y = _dot(obuf[...], wo_ref[0])
    o_ref[...] = _layer_norm(ALPHA * x + y, g_ref[layer, ln:ln + 1, :], b_ref[layer, ln:ln + 1, :])


def _cross_sample(x, mk, mv, wq, wo, g, b, layer, ln, bb):
    nb = x.shape[0]
    kern = functools.partial(_cross_sample_kernel, bb=bb, layer=layer, ln=ln)
    xblk = pl.BlockSpec((bb, D_MODEL), lambda i: (i, 0))
    mblk = pl.BlockSpec((1, bb, MEM_LEN * _MEM_ROWS, LANES), lambda i: (layer, i, 0, 0))
    return pl.pallas_call(
        kern,
        grid=(nb // bb,),
        in_specs=[xblk, mblk, mblk, _weight((D_MODEL, D_MODEL)), _weight((D_MODEL, D_MODEL))] + _ln_specs(),
        out_specs=xblk,
        out_shape=jax.ShapeDtypeStruct((nb, D_MODEL), F32),
        scratch_shapes=[pltpu.VMEM((bb, _MEM_ROWS, LANES), F32),
                        pltpu.VMEM((bb, _MEM_ROWS, LANES), F32),
                        pltpu.VMEM((bb, D_MODEL), BF16)],
        compiler_params=_params("arbitrary"),
        name="cross_sample",
    )(x, mk, mv, wq, wo, g, b)


def _head_piece_cols():
    return [(half * MEM_HEADS + h, h * MEM_HEAD_DIM + half * LANES)
            for half in range(_MEM_HALVES) for h in range(MEM_HEADS)]


def _cross_q_kernel(x_ref, wq_ref, q8_ref):
    q = _dot(x_ref[...].astype(BF16), wq_ref[0]) * (MEM_HEAD_DIM ** -0.5)
    for row, col in _head_piece_cols():
        q8_ref[:, row, :] = q[:, col:col + LANES]


def _cross_q(x, wq):
    nb = x.shape[0]
    return pl.pallas_call(
        _cross_q_kernel,
        grid=(1,),
        in_specs=[_resident((nb, D_MODEL)), _weight((D_MODEL, D_MODEL))],
        out_specs=pl.BlockSpec((nb, _MEM_ROWS, LANES), lambda i: (0, 0, 0)),
        out_shape=jax.ShapeDtypeStruct((nb, _MEM_ROWS, LANES), F32),
        compiler_params=_params("arbitrary"),
        name="cross_sample_q",
    )(x, wq)


def _cross_out_kernel(x_ref, o8_ref, wo_ref, g_ref, b_ref, o_ref, obuf, *, layer, ln):
    for row, col in _head_piece_cols():
        obuf[:, col:col + LANES] = o8_ref[:, row, :].astype(BF16)
    y = _dot(obuf[...], wo_ref[0])
    o_ref[...] = _layer_norm(ALPHA * x_ref[...] + y, g_ref[layer, ln:ln + 1, :], b_ref[layer, ln:ln + 1, :])


def _cross_out(x, o8, wo, g, b, layer, ln):
    nb = x.shape[0]
    return pl.pallas_call(
        functools.partial(_cross_out_kernel, layer=layer, ln=ln),
        grid=(1,),
        in_specs=[_resident((nb, D_MODEL)), _resident((nb, _MEM_ROWS, LANES)), _weight((D_MODEL, D_MODEL))]
        + _ln_specs(),
        out_specs=pl.BlockSpec((nb, D_MODEL), lambda i: (0, 0)),
        out_shape=jax.ShapeDtypeStruct((nb, D_MODEL), F32),
        scratch_shapes=[pltpu.VMEM((nb, D_MODEL), BF16)],
        compiler_params=_params("arbitrary"),
        name="cross_sample_out",
    )(x, o8, wo, g, b)
SC_LANES = 16
SC_TOKENS = 16
_SC_ROWS = SC_TOKENS * _MEM_ROWS
_SC_CHUNKS = MEM_LEN // SC_TOKENS
_SC_PIECES = LANES // SC_LANES
_SC_NEG = -1e30


def _sc_cross_attention(q8, mk_tiles, mv_tiles, layer, seq0, nseq):
    total_seq = mk_tiles.shape[1]
    kflat = mk_tiles.reshape(-1, LANES)
    vflat = mv_tiles.reshape(-1, LANES)
    chunk0 = (layer * total_seq + seq0) * _SC_CHUNKS
    mesh = plsc.VectorSubcoreMesh(core_axis_name="c", subcore_axis_name="s")

    @pl.kernel(out_type=jax.ShapeDtypeStruct((nseq * _MEM_ROWS, LANES), F32), mesh=mesh,
               scratch_types=[pltpu.VMEM((MEM_HEADS, SC_LANES), F32), pltpu.VMEM((MEM_HEADS, SC_LANES), F32)],
               compiler_params=pltpu.CompilerParams(needs_layout_passes=False))
    def attend(q_hbm, k_hbm, v_hbm, o_hbm, m_ref, l_ref):
        def body(idx, q_vmem, k_vmem, v_vmem, o_vmem):
            chunk = idx[1]
            lane = lax.iota(jnp.int32, SC_LANES)

            @pl.when(chunk == 0)
            def _():
                for h in range(MEM_HEADS):
                    m_ref[h, :] = jnp.full((SC_LANES,), _SC_NEG, F32)
                    l_ref[h, :] = jnp.zeros((SC_LANES,), F32)
                for r in range(_MEM_ROWS):
                    for j in range(_SC_PIECES):
                        o_vmem[r, pl.ds(j * SC_LANES, SC_LANES)] = jnp.zeros((SC_LANES,), F32)

            def one_head(h, carry):
                acc = [jnp.zeros((SC_LANES,), F32) for _ in range(SC_TOKENS)]
                for half in range(_MEM_HALVES):
                    r = half * MEM_HEADS + h
                    for j in range(_SC_PIECES):
                        sl = pl.ds(j * SC_LANES, SC_LANES)
                        qv = q_vmem[r, sl]
                        for t in range(SC_TOKENS):
                            acc[t] = acc[t] + k_vmem[t * _MEM_ROWS + r, sl] * qv
                s = jnp.zeros((SC_LANES,), F32)
                for t in range(SC_TOKENS):
                    s = jnp.where(lane == t, jnp.sum(acc[t]), s)

                m_old = m_ref[h, :]
                m_new = jnp.maximum(m_old, jnp.max(s))
                alpha = jnp.exp(m_old - m_new)
                p = jnp.exp(s - m_new)
                l_ref[h, :] = l_ref[h, :] * alpha + jnp.sum(p)
                m_ref[h, :] = m_new

                pt = [jnp.full((SC_LANES,), p[t], F32) for t in range(SC_TOKENS)]
                for half in range(_MEM_HALVES):
                    r = half * MEM_HEADS + h
                    for j in range(_SC_PIECES):
                        sl = pl.ds(j * SC_LANES, SC_LANES)
                        o = o_vmem[r, sl] * alpha
                        for t in range(SC_TOKENS):
                            o = o + pt[t] * v_vmem[t * _MEM_ROWS + r, sl]
                        o_vmem[r, sl] = o
                return carry

            lax.fori_loop(0, MEM_HEADS, one_head, 0)

            @pl.when(chunk == _SC_CHUNKS - 1)
            def _():
                for h in range(MEM_HEADS):
                    inv = 1.0 / l_ref[h, :]
                    for half in range(_MEM_HALVES):
                        r = half * MEM_HEADS + h
                        for j in range(_SC_PIECES):
                            sl = pl.ds(j * SC_LANES, SC_LANES)
                            o_vmem[r, sl] = o_vmem[r, sl] * inv

        pltpu.emit_pipeline(
            body,
            grid=(nseq, _SC_CHUNKS),
            in_specs=[pl.BlockSpec((_MEM_ROWS, LANES), lambda s, c: (seq0 + s, 0)),
                      pl.BlockSpec((_SC_ROWS, LANES), lambda s, c: (chunk0 + s * _SC_CHUNKS + c, 0)),
                      pl.BlockSpec((_SC_ROWS, LANES), lambda s, c: (chunk0 + s * _SC_CHUNKS + c, 0))],
            out_specs=[pl.BlockSpec((_MEM_ROWS, LANES), lambda s, c: (s, 0))],
            core_axis_name=("c", "s"),
            dimension_semantics=(pltpu.PARALLEL, pltpu.ARBITRARY),
            _explicit_indices=True,
        )(q_hbm, k_hbm, v_hbm, o_hbm)

    return attend(q8, kflat, vflat)


TM_PROMPT = 1024
BB_MIX_SAMPLE = 32
BB_CROSS_SAMPLE = 8


def kernel(x_prompt, x_sample, mem_prompt, cache_win_k, cache_win_v, state_conv, cache_mem_k, cache_mem_v,
           ln_g, ln_b, ffn1_w_gu, ffn1_w_down, w_in, conv_w, attn_sinks, w_out,
           w_cq, w_mk, w_mv, w_co, ffn2_w_gu, ffn2_w_down):
    bsz, seq, _ = x_prompt.shape
    nsmp = x_sample.shape[0]
    yp = x_prompt
    ys = x_sample.reshape(nsmp, D_MODEL)
    mem2 = mem_prompt.reshape(bsz * MEM_LEN, D_MODEL)

    stacked = dict(ffn1=(ffn1_w_gu, ffn1_w_down), ffn2=(ffn2_w_gu, ffn2_w_down),
                   mix=(w_in, w_out), cross=(w_cq, w_co, w_mk, w_mv))
    wb = {("ffn1", 0): tuple(w[0:1].astype(BF16) for w in stacked["ffn1"])}

    def jobs(*groups):
        return [(w, layer) for name, layer in groups for w in stacked[name]]

    def keep(groups, casts):
        casts = list(casts)
        for name, layer in groups:
            wb[(name, layer)] = tuple(casts[:len(stacked[name])])
            del casts[:len(stacked[name])]

    def token_tiles(c):
        c = c.reshape(DEPTH, nsmp, MEM_LEN, MEM_HEADS, _MEM_HALVES, LANES)
        return c.transpose(0, 1, 2, 4, 3, 5).reshape(DEPTH, nsmp, MEM_LEN * _MEM_ROWS, LANES)

    mem_k_tiles, mem_v_tiles = token_tiles(cache_mem_k), token_tiles(cache_mem_v)

    def window_lanes(c):
        return c.transpose(0, 1, 3, 4, 2).reshape(DEPTH, nsmp, D_KV, WINDOW)

    win_k_lanes, win_v_lanes = window_lanes(cache_win_k), window_lanes(cache_win_v)
    new_windows = None

    wkp, wvp, cvp, mkp, mvp, cvs = [], [], [], [], [], []
    for l in range(DEPTH):
        more = l + 1 < DEPTH
        side = [("mix", 0), ("cross", 0)] if l == 0 else [("ffn2", l)]
        yp, ys, *casts = _ffn(yp.reshape(bsz * seq, D_MODEL), ys, *wb[("ffn1", l)], ln_g, ln_b, l, 0, TM_PROMPT,
                              jobs(*side))
        keep(side, casts)
        yp = yp.reshape(bsz, seq, D_MODEL)

        win, wout = wb[("mix", l)]
        side = [("ffn2", 0)] if l == 0 else []
        yp, nk, nv, ncv, *casts = _mix_prompt(yp, attn_sinks, win, conv_w, wout, ln_g, ln_b, l, 1, TM_PROMPT,
                                              jobs(*side))
        keep(side, casts)
        wkp.append(nk.reshape(bsz, WINDOW, N_KV_HEADS, HEAD_DIM))
        wvp.append(nv.reshape(bsz, WINDOW, N_KV_HEADS, HEAD_DIM))
        cvp.append(ncv)
        ys, nst, nks, nvs = _mix_sample(
            ys, attn_sinks, state_conv[l].reshape(nsmp, (CONV_W - 1) * D_CONV),
            win_k_lanes, win_v_lanes, new_windows, win, conv_w, wout, ln_g, ln_b, l, 1, BB_MIX_SAMPLE)
        new_windows = (nks, nvs)
        cvs.append(nst.reshape(nsmp, CONV_W - 1, D_CONV))

        wcq, wco, wmk, wmv = wb[("cross", l)]
        mk, mv, mkb, mvb = _memkv(mem2, wmk, wmv)
        mkp.append(mk.reshape(bsz, MEM_LEN, MEM_HEADS, MEM_HEAD_DIM))
        mvp.append(mv.reshape(bsz, MEM_LEN, MEM_HEADS, MEM_HEAD_DIM))
        side = [("ffn1", l + 1)] if more else []
        yp, *casts = _cross_prompt(yp, mkb.reshape(bsz, MEM_LEN, D_MODEL), mvb.reshape(bsz, MEM_LEN, D_MODEL),
                                   wcq, wco, ln_g, ln_b, l, 2, TM_PROMPT, jobs(*side))
        keep(side, casts)
        q8 = _cross_q(ys, wcq).reshape(nsmp * _MEM_ROWS, LANES)
        o8 = _sc_cross_attention(q8, mem_k_tiles, mem_v_tiles, l, 0, nsmp)
        ys = _cross_out(ys, o8.reshape(nsmp, _MEM_ROWS, LANES), wco, ln_g, ln_b, l, 2)

        side = [("mix", l + 1), ("cross", l + 1)] if more else []
        yp, ys, *casts = _ffn(yp.reshape(bsz * seq, D_MODEL), ys, *wb[("ffn2", l)], ln_g, ln_b, l, 3, TM_PROMPT,
                              jobs(*side))
        keep(side, casts)
        yp = yp.reshape(bsz, seq, D_MODEL)

    def window_rows(c):
        return c.reshape(DEPTH, nsmp, N_KV_HEADS, HEAD_DIM, WINDOW).transpose(0, 1, 4, 2, 3)

    return (yp, ys.reshape(nsmp, 1, D_MODEL),
            jnp.stack(wkp), jnp.stack(wvp), jnp.stack(cvp), jnp.stack(mkp), jnp.stack(mvp),
            window_rows(new_windows[0]), window_rows(new_windows[1]), jnp.stack(cvs))
```

```python
import functools

import jax
import jax.numpy as jnp
from jax import lax
from jax.experimental import pallas as pl
from jax.experimental.pallas import tpu as pltpu
from jax.experimental.pallas import tpu_sc as plsc

D_MODEL = 1024
DEPTH = 2
D_CONV = 512
CONV_W = 3
HEAD_DIM = 64
N_HEADS = 8
N_KV_HEADS = 2
GROUP = N_HEADS // N_KV_HEADS
WINDOW = 128
D_ATTN = N_HEADS * HEAD_DIM
D_KV = N_KV_HEADS * HEAD_DIM
IN_COLS = 3 * D_CONV + D_ATTN + 2 * D_KV
MEM_LEN = 256
MEM_HEADS = 4
MEM_HEAD_DIM = D_MODEL // MEM_HEADS
D_FF = 2816
N_LN = 4
ALPHA = (2.0 * DEPTH) ** 0.25
LN_EPS = 1e-5

_C_BG, _C_CG, _C_HC, _C_Q, _C_KV = 0, D_CONV, 2 * D_CONV, 3 * D_CONV, 3 * D_CONV + D_ATTN

LANES = 128
SUBLANES = 8
MXU_COLS = 256
VMEM_LIMIT_BYTES = 56 * 1024 * 1024

BF16 = jnp.bfloat16
F32 = jnp.float32


def _dot(a, b):
    return jnp.dot(a, b, preferred_element_type=F32)


def _dot_nt(a, b):
    return lax.dot_general(a, b, (((1,), (1,)), ((), ())), preferred_element_type=F32)


def _layer_norm(z, g, b):
    mu = jnp.mean(z, axis=-1, keepdims=True)
    d = z - mu
    var = jnp.mean(d * d, axis=-1, keepdims=True)
    return d * lax.rsqrt(var + LN_EPS) * g + b


def _resident(shape):
    zeros = (0,) * len(shape)
    return pl.BlockSpec(shape, lambda *_: zeros, pipeline_mode=pl.Buffered(1))


def _weight(shape):
    return _resident((1,) + shape)


def _cast_specs(jobs, n_steps, step_of):
    in_specs, out_specs, out_shapes, args = [], [], [], []
    for w, layer in jobs:
        _, k, n = w.shape
        rows = k // n_steps
        assert rows * n_steps == k and rows % (2 * SUBLANES) == 0, (w.shape, n_steps)
        in_specs.append(pl.BlockSpec((1, rows, n), lambda *idx, layer=layer: (layer, step_of(*idx), 0)))
        out_specs.append(pl.BlockSpec((1, rows, n), lambda *idx: (0, step_of(*idx), 0)))
        out_shapes.append(jax.ShapeDtypeStruct((1, k, n), BF16))
        args.append(w)
    return in_specs, out_specs, out_shapes, args


def _run_casts(src_refs, dst_refs):
    for src, dst in zip(src_refs, dst_refs, strict=True):
        dst[...] = src[...].astype(BF16)


def _ln_specs():
    return [_resident((DEPTH, N_LN, D_MODEL)), _resident((DEPTH, N_LN, D_MODEL))]


def _params(*sem):
    return pltpu.CompilerParams(dimension_semantics=sem, vmem_limit_bytes=VMEM_LIMIT_BYTES)


NORM_ROWS = 256


def _project_and_norm(lhs_ref, w, x_rows, scale, g, b, write_rows, rows):
    for r0 in range(0, rows, min(NORM_ROWS, rows)):
        r1 = r0 + min(NORM_ROWS, rows)
        y = _dot(lhs_ref[r0:r1, :], w)
        write_rows(r0, r1, _layer_norm(ALPHA * x_rows(r0, r1) + scale * y, g, b))


def _ffn_rows(x_ref, o_ref, wgu_ref, wd_ref, h_ref, g, b):
    rows = x_ref.shape[0]
    xb = x_ref[...].astype(BF16)
    for c in range(D_FF // MXU_COLS):
        lo = c * MXU_COLS
        gate = _dot(xb, wgu_ref[0, :, lo:lo + MXU_COLS])
        up = _dot(xb, wgu_ref[0, :, D_FF + lo:D_FF + lo + MXU_COLS])
        h_ref[0:rows, lo:lo + MXU_COLS] = (gate * jax.nn.sigmoid(gate) * up).astype(BF16)

    def write_rows(r0, r1, y):
        o_ref[r0:r1, :] = y

    _project_and_norm(h_ref, wd_ref[0], lambda r0, r1: x_ref[r0:r1, :], 0.5, g, b, write_rows, rows)


def _ffn_kernel(x_ref, xs_ref, wgu_ref, wd_ref, g_ref, b_ref, *rest, layer, ln, n_cast, n_tiles):
    cast_src, (o_ref, os_ref), rest = rest[:n_cast], rest[n_cast:n_cast + 2], rest[n_cast + 2:]
    cast_dst, h_ref = rest[:n_cast], rest[-1]
    g = g_ref[layer, ln:ln + 1, :]
    b = b_ref[layer, ln:ln + 1, :]
    step = pl.program_id(0)

    @pl.when(step < n_tiles)
    def _():
        _run_casts(cast_src, cast_dst)
        _ffn_rows(x_ref, o_ref, wgu_ref, wd_ref, h_ref, g, b)

    @pl.when(step == n_tiles)
    def _():
        _ffn_rows(xs_ref, os_ref, wgu_ref, wd_ref, h_ref, g, b)


def _ffn(x, xs, wgu, wd, g, b, layer, ln, tm, casts=()):
    m = x.shape[0]
    ns = xs.shape[0]
    n_tiles = m // tm
    tile = lambda i: (jnp.minimum(i, n_tiles - 1), 0)
    c_in, c_out, c_shapes, c_args = _cast_specs(casts, n_tiles, lambda i: jnp.minimum(i, n_tiles - 1))
    return pl.pallas_call(
        functools.partial(_ffn_kernel, layer=layer, ln=ln, n_cast=len(casts), n_tiles=n_tiles),
        grid=(n_tiles + 1,),
        in_specs=[
            pl.BlockSpec((tm, D_MODEL), tile),
            _resident((ns, D_MODEL)),
            _weight((D_MODEL, 2 * D_FF)),
            _weight((D_FF, D_MODEL)),
        ] + _ln_specs() + c_in,
        out_specs=[pl.BlockSpec((tm, D_MODEL), tile), pl.BlockSpec((ns, D_MODEL), lambda i: (0, 0))] + c_out,
        out_shape=[jax.ShapeDtypeStruct((m, D_MODEL), F32), jax.ShapeDtypeStruct((ns, D_MODEL), F32)] + c_shapes,
        scratch_shapes=[pltpu.VMEM((tm, D_FF), BF16)],
        compiler_params=_params("arbitrary"),
        name="ffn_ln",
    )(x, xs, wgu, wd, g, b, *c_args)


def _dup_halves(a):
    lane = lax.broadcasted_iota(jnp.int32, a.shape, 1)
    low = lane < HEAD_DIM
    rolled = pltpu.roll(a, HEAD_DIM, 1)
    return jnp.where(low, a, rolled), jnp.where(low, rolled, a)


def _sink_column(sink_ref, layer, first_head, rows_per_head, n_heads):
    rows = n_heads * rows_per_head
    r = lax.broadcasted_iota(jnp.int32, (rows, 1), 0)
    col = jnp.full((rows, 1), sink_ref[layer, first_head + n_heads - 1], F32)
    for g in range(n_heads - 2, -1, -1):
        col = jnp.where(r < (g + 1) * rows_per_head, sink_ref[layer, first_head + g], col)
    return col


def _mix_prompt_kernel(sink_ref, x_ref, win_ref, cw_ref, wout_ref, g_ref, b_ref, *rest, tm, layer, ln, n_cast):
    cast_src, rest = rest[:n_cast], rest[n_cast:]
    (o_ref, nk_ref, nv_ref, nc_ref), rest = rest[:4], rest[4:]
    cast_dst, (ubuf, qbuf, kd0, kd1, vd0, vd1, zbuf) = rest[:n_cast], rest[n_cast:]
    _run_casts(cast_src, cast_dst)
    seq_start = pl.program_id(1) == 0
    xb = x_ref[0].astype(BF16)

    def proj(col, width):
        return _dot(xb, win_ref[0, :, col:col + width])

    @pl.when(seq_start)
    def _():
        ubuf[0:SUBLANES, :] = jnp.zeros((SUBLANES, D_CONV), F32)
        for r in (kd0, kd1, vd0, vd1):
            r[0:WINDOW, :] = jnp.zeros((WINDOW, LANES), BF16)

    u = proj(_C_CG, D_CONV) * proj(_C_HC, D_CONV)
    ubuf[SUBLANES:SUBLANES + tm, :] = u
    conv = (cw_ref[layer, 0:1, :] * ubuf[SUBLANES - 2:SUBLANES - 2 + tm, :]
            + cw_ref[layer, 1:2, :] * ubuf[SUBLANES - 1:SUBLANES - 1 + tm, :]
            + cw_ref[layer, 2:3, :] * u)
    zbuf[:, 0:D_CONV] = (proj(_C_BG, D_CONV) * conv).astype(BF16)
    nc_ref[0] = u[tm - (CONV_W - 1):tm, :]
    ubuf[0:SUBLANES, :] = u[tm - SUBLANES:tm, :]

    qbuf[...] = (proj(_C_Q, D_ATTN) * (HEAD_DIM ** -0.5)).astype(BF16)
    k = proj(_C_KV, D_KV)
    v = proj(_C_KV + D_KV, D_KV)
    nk_ref[0] = k[tm - WINDOW:tm, :]
    nv_ref[0] = v[tm - WINDOW:tm, :]
    ka, kb = _dup_halves(k)
    va, vb = _dup_halves(v)
    kd0[WINDOW:WINDOW + tm, :] = ka.astype(BF16)
    kd1[WINDOW:WINDOW + tm, :] = kb.astype(BF16)
    vd0[WINDOW:WINDOW + tm, :] = va.astype(BF16)
    vd1[WINDOW:WINDOW + tm, :] = vb.astype(BF16)

    rows = GROUP * WINDOW
    a_idx = lax.broadcasted_iota(jnp.int32, (rows, 2 * WINDOW), 0) % WINDOW
    c_idx = lax.broadcasted_iota(jnp.int32, (rows, 2 * WINDOW), 1)
    band = (c_idx >= a_idx) & (c_idx <= a_idx + WINDOW)
    first_lb = jnp.where(seq_start, WINDOW, 0)
    low = lax.broadcasted_iota(jnp.int32, (WINDOW, LANES), 1) < HEAD_DIM
    zero = jnp.zeros((WINDOW, LANES), BF16)

    for n in range(tm // WINDOW):
        r0 = n * WINDOW
        mask = (band & (c_idx >= first_lb)) if n == 0 else band
        for j, (kd, vd) in enumerate(((kd0, vd0), (kd1, vd1))):
            qa = qbuf[r0:r0 + WINDOW, (2 * j) * LANES:(2 * j + 1) * LANES]
            qb = qbuf[r0:r0 + WINDOW, (2 * j + 1) * LANES:(2 * j + 2) * LANES]
            qs = jnp.concatenate([jnp.where(low, qa, zero), jnp.where(low, zero, qa),
                                  jnp.where(low, qb, zero), jnp.where(low, zero, qb)], axis=0)
            s = _dot_nt(qs, kd[r0:r0 + 2 * WINDOW, :])
            s = jnp.where(mask, s, -jnp.inf)
            sink = _sink_column(sink_ref, layer, GROUP * j, WINDOW, GROUP)
            m = jnp.maximum(jnp.max(s, axis=-1, keepdims=True), sink)
            p = jnp.exp(s - m)
            denom = jnp.sum(p, axis=-1, keepdims=True) + jnp.exp(sink - m)
            o = _dot(p.astype(BF16), vd[r0:r0 + 2 * WINDOW, :]) / denom
            za = jnp.where(low, o[0:WINDOW], o[WINDOW:2 * WINDOW])
            zb = jnp.where(low, o[2 * WINDOW:3 * WINDOW], o[3 * WINDOW:4 * WINDOW])
            c0 = D_CONV + (2 * j) * LANES
            zbuf[r0:r0 + WINDOW, c0:c0 + LANES] = za.astype(BF16)
            zbuf[r0:r0 + WINDOW, c0 + LANES:c0 + 2 * LANES] = zb.astype(BF16)

    for r in (kd0, kd1, vd0, vd1):
        r[0:WINDOW, :] = r[tm:tm + WINDOW, :]

    def write_rows(r0, r1, y):
        o_ref[0, r0:r1, :] = y

    _project_and_norm(zbuf, wout_ref[0], lambda r0, r1: x_ref[0, r0:r1, :], 1.0,
                      g_ref[layer, ln:ln + 1, :], b_ref[layer, ln:ln + 1, :], write_rows, tm)


def _mix_prompt(x, sinks, win, cw, wout, g, b, layer, ln, tm, casts=()):
    bsz, seq, _ = x.shape
    tps = seq // tm
    kern = functools.partial(_mix_prompt_kernel, tm=tm, layer=layer, ln=ln, n_cast=len(casts))
    per_seq = lambda bi, i, s: (bi, 0, 0)
    c_in, c_out, c_shapes, c_args = _cast_specs(casts, bsz * tps, lambda bi, i, *_: bi * tps + i)
    grid_spec = pltpu.PrefetchScalarGridSpec(
        num_scalar_prefetch=1,
        grid=(bsz, tps),
        in_specs=[
            pl.BlockSpec((1, tm, D_MODEL), lambda bi, i, s: (bi, i, 0)),
            _weight((D_MODEL, IN_COLS)),
            _resident((DEPTH, CONV_W, D_CONV)),
            _weight((D_CONV + D_ATTN, D_MODEL)),
        ] + _ln_specs() + c_in,
        out_specs=[
            pl.BlockSpec((1, tm, D_MODEL), lambda bi, i, s: (bi, i, 0)),
            pl.BlockSpec((1, WINDOW, D_KV), per_seq),
            pl.BlockSpec((1, WINDOW, D_KV), per_seq),
            pl.BlockSpec((1, CONV_W - 1, D_CONV), per_seq),
        ] + c_out,
        scratch_shapes=[
            pltpu.VMEM((tm + SUBLANES, D_CONV), F32),
            pltpu.VMEM((tm, D_ATTN), BF16),
            pltpu.VMEM((tm + WINDOW, LANES), BF16),
            pltpu.VMEM((tm + WINDOW, LANES), BF16),
            pltpu.VMEM((tm + WINDOW, LANES), BF16),
            pltpu.VMEM((tm + WINDOW, LANES), BF16),
            pltpu.VMEM((tm, D_CONV + D_ATTN), BF16),
        ],
    )
    return pl.pallas_call(
        kern,
        grid_spec=grid_spec,
        out_shape=[
            jax.ShapeDtypeStruct((bsz, seq, D_MODEL), F32),
            jax.ShapeDtypeStruct((bsz, WINDOW, D_KV), F32),
            jax.ShapeDtypeStruct((bsz, WINDOW, D_KV), F32),
            jax.ShapeDtypeStruct((bsz, CONV_W - 1, D_CONV), F32),
        ] + c_shapes,
        compiler_params=_params("arbitrary", "arbitrary"),
        name="mix_prompt",
    )(sinks, x, win, cw, wout, g, b, *c_args)


def _memkv_kernel(m_ref, wk_ref, wv_ref, *rest, layer):
    k_ref, v_ref, kb_ref, vb_ref = rest[-4:]
    mb = m_ref[...].astype(BF16)
    k = _dot(mb, wk_ref[0])
    v = _dot(mb, wv_ref[0])
    kb_ref[...] = k.astype(BF16)
    vb_ref[...] = v.astype(BF16)
    slab = layer if k_ref.shape[0] > 1 else 0
    for other in range(k_ref.shape[0]):
        if other != slab:
            k_ref[other] = jnp.zeros(k_ref.shape[1:], F32)
            v_ref[other] = jnp.zeros(v_ref.shape[1:], F32)
    for row, col in _head_piece_cols():
        k_ref[slab, 0, :, row, :] = k[:, col:col + LANES]
        v_ref[slab, 0, :, row, :] = v[:, col:col + LANES]


def _memkv(mem, wk, wv, layer, prev):
    m = mem.shape[0]
    nb = m // MEM_LEN
    blk = pl.BlockSpec((MEM_LEN, D_MODEL), lambda i: (i, 0))
    in_specs = [blk, _weight((D_MODEL, D_MODEL)), _weight((D_MODEL, D_MODEL))]
    args = [mem, wk, wv]
    aliases = {}
    if prev is None:
        tile_blk = pl.BlockSpec((DEPTH, 1, MEM_LEN, _MEM_ROWS, LANES), lambda i: (0, i, 0, 0, 0))
    else:
        tile_blk = pl.BlockSpec((1, 1, MEM_LEN, _MEM_ROWS, LANES), lambda i: (layer, i, 0, 0, 0))
        in_specs += [pl.BlockSpec(memory_space=pl.ANY)] * 2
        args += list(prev)
        aliases = {3: 0, 4: 1}
    tiles = jax.ShapeDtypeStruct((DEPTH, nb, MEM_LEN, _MEM_ROWS, LANES), F32)
    return pl.pallas_call(
        functools.partial(_memkv_kernel, layer=layer),
        grid=(nb,),
        in_specs=in_specs,
        out_specs=[tile_blk, tile_blk, blk, blk],
        out_shape=[tiles, tiles, jax.ShapeDtypeStruct((m, D_MODEL), BF16), jax.ShapeDtypeStruct((m, D_MODEL), BF16)],
        input_output_aliases=aliases,
        compiler_params=_params("arbitrary"),
        name="mem_kv",
    )(*args)


def _cross_prompt_kernel(x_ref, mk_ref, mv_ref, wq_ref, wo_ref, g_ref, b_ref, *rest, layer, ln, n_cast):
    cast_src, o_ref, cast_dst, obuf = rest[:n_cast], rest[n_cast], rest[n_cast + 1:2 * n_cast + 1], rest[-1]
    _run_casts(cast_src, cast_dst)
    q = _dot(x_ref[0].astype(BF16), wq_ref[0]) * (MEM_HEAD_DIM ** -0.5)
    for h in range(MEM_HEADS):
        c0 = h * MEM_HEAD_DIM
        s = _dot_nt(q[:, c0:c0 + MEM_HEAD_DIM].astype(BF16), mk_ref[0, :, c0:c0 + MEM_HEAD_DIM])
        m = jnp.max(s, axis=-1, keepdims=True)
        p = jnp.exp(s - m)
        denom = jnp.sum(p, axis=-1, keepdims=True)
        o = _dot(p.astype(BF16), mv_ref[0, :, c0:c0 + MEM_HEAD_DIM]) / denom
        obuf[:, c0:c0 + MEM_HEAD_DIM] = o.astype(BF16)

    def write_rows(r0, r1, y):
        o_ref[0, r0:r1, :] = y

    _project_and_norm(obuf, wo_ref[0], lambda r0, r1: x_ref[0, r0:r1, :], 1.0,
                      g_ref[layer, ln:ln + 1, :], b_ref[layer, ln:ln + 1, :], write_rows, x_ref.shape[1])


def _cross_prompt(x, mk, mv, wq, wo, g, b, layer, ln, tm, casts=()):
    bsz, seq, _ = x.shape
    tps = seq // tm
    xblk = pl.BlockSpec((1, tm, D_MODEL), lambda bi, i: (bi, i, 0))
    mblk = pl.BlockSpec((1, MEM_LEN, D_MODEL), lambda bi, i: (bi, 0, 0))
    c_in, c_out, c_shapes, c_args = _cast_specs(casts, bsz * tps, lambda bi, i: bi * tps + i)
    return pl.pallas_call(
        functools.partial(_cross_prompt_kernel, layer=layer, ln=ln, n_cast=len(casts)),
        grid=(bsz, tps),
        in_specs=[xblk, mblk, mblk, _weight((D_MODEL, D_MODEL)), _weight((D_MODEL, D_MODEL))] + _ln_specs() + c_in,
        out_specs=[xblk] + c_out,
        out_shape=[jax.ShapeDtypeStruct((bsz, seq, D_MODEL), F32)] + c_shapes,
        scratch_shapes=[pltpu.VMEM((tm, D_MODEL), BF16)],
        compiler_params=_params("arbitrary", "arbitrary"),
        name="cross_prompt",
    )(x, mk, mv, wq, wo, g, b, *c_args)


def _bdot(a, b, contract_b):
    return lax.dot_general(a, b, (((2,), (contract_b,)), ((0,), (0,))), preferred_element_type=F32)


def _mix_sample_kernel(sink_ref, x_ref, st_ref, kc_ref, vc_ref, win_ref, cw_ref, wout_ref, g_ref, b_ref,
                       *rest, bb, layer, ln):
    o_ref, nst_ref, nk_ref, nv_ref, zbuf, qm_scr, kn_scr, vn_scr, o_scr = rest[-9:]
    x = x_ref[...]
    xb = x.astype(BF16)

    def proj(col, width):
        return _dot(xb, win_ref[0, :, col:col + width])

    u = proj(_C_CG, D_CONV) * proj(_C_HC, D_CONV)
    st0 = st_ref[:, 0:D_CONV]
    st1 = st_ref[:, D_CONV:2 * D_CONV]
    conv = cw_ref[layer, 0:1, :] * st0 + cw_ref[layer, 1:2, :] * st1 + cw_ref[layer, 2:3, :] * u
    zbuf[:, 0:D_CONV] = (proj(_C_BG, D_CONV) * conv).astype(BF16)
    nst_ref[:, 0:D_CONV] = st1
    nst_ref[:, D_CONV:2 * D_CONV] = u

    q = proj(_C_Q, D_ATTN) * (HEAD_DIM ** -0.5)
    k = proj(_C_KV, D_KV)
    v = proj(_C_KV + D_KV, D_KV)

    pad = jnp.zeros((LANES - bb, D_KV), F32)
    kt = jnp.concatenate([k, pad], axis=0).T if bb < LANES else k.T
    vt = jnp.concatenate([v, pad], axis=0).T if bb < LANES else v.T
    last = lax.broadcasted_iota(jnp.int32, (D_KV, WINDOW), 1) == WINDOW - 1
    slab = layer if nk_ref.shape[0] > 1 else 0
    for other in range(nk_ref.shape[0]):
        if other != slab:
            nk_ref[other] = jnp.zeros(nk_ref.shape[1:], F32)
            nv_ref[other] = jnp.zeros(nv_ref.shape[1:], F32)
    for r in range(bb):
        nk_ref[slab, r] = jnp.where(last, kt[:, r:r + 1], pltpu.roll(kc_ref[0, r], WINDOW - 1, 1))
        nv_ref[slab, r] = jnp.where(last, vt[:, r:r + 1], pltpu.roll(vc_ref[0, r], WINDOW - 1, 1))

    lane = lax.broadcasted_iota(jnp.int32, (bb, LANES), 1)
    low = lane < HEAD_DIM
    for h in range(N_HEADS):
        kvh = h // GROUP
        qt = q[:, (h // 2) * LANES:(h // 2 + 1) * LANES]
        if (h % 2) != kvh:
            qt = pltpu.roll(qt, HEAD_DIM, 1)
        keep = low if kvh == 0 else jnp.logical_not(low)
        qm_scr[:, h, :] = jnp.where(keep, qt, 0.0)
        kn_scr[:, h, :] = k
        vn_scr[:, h, :] = v
    qm = qm_scr[...]

    s = _bdot(qm.astype(BF16), kc_ref[0].astype(BF16), 1)
    s_new = jnp.sum(qm * kn_scr[...], axis=-1, keepdims=True)
    hrow = lax.broadcasted_iota(jnp.int32, (1, N_HEADS, 1), 1)
    sink = jnp.full((1, N_HEADS, 1), sink_ref[layer, N_HEADS - 1], F32)
    for h in range(N_HEADS - 1):
        sink = jnp.where(hrow == h, sink_ref[layer, h], sink)
    m = jnp.maximum(jnp.maximum(jnp.max(s, axis=-1, keepdims=True), s_new), sink)
    p = jnp.exp(s - m)
    p_new = jnp.exp(s_new - m)
    denom = jnp.sum(p, axis=-1, keepdims=True) + p_new + jnp.exp(sink - m)
    o_scr[...] = (_bdot(p.astype(BF16), vc_ref[0].astype(BF16), 2) + p_new * vn_scr[...]) / denom

    for t in range(D_ATTN // LANES):
        oa = o_scr[:, 2 * t, :]
        ob = o_scr[:, 2 * t + 1, :]
        if (2 * t) // GROUP == 0:
            z = jnp.where(low, oa, pltpu.roll(ob, HEAD_DIM, 1))
        else:
            z = jnp.where(low, pltpu.roll(oa, HEAD_DIM, 1), ob)
        zbuf[:, D_CONV + t * LANES:D_CONV + (t + 1) * LANES] = z.astype(BF16)

    y = _dot(zbuf[...], wout_ref[0])
    o_ref[...] = _layer_norm(ALPHA * x + y, g_ref[layer, ln:ln + 1, :], b_ref[layer, ln:ln + 1, :])


def _mix_sample(x, sinks, st, kc, vc, prev_windows, win, cw, wout, g, b, layer, ln, bb):
    nb = x.shape[0]
    kern = functools.partial(_mix_sample_kernel, bb=bb, layer=layer, ln=ln)
    row = lambda i, s: (i, 0)
    win_blk = pl.BlockSpec((1, bb, D_KV, WINDOW), lambda i, s: (layer, i, 0, 0))
    in_specs = [
        pl.BlockSpec((bb, D_MODEL), row),
        pl.BlockSpec((bb, (CONV_W - 1) * D_CONV), row),
        win_blk,
        win_blk,
        _weight((D_MODEL, IN_COLS)),
        _resident((DEPTH, CONV_W, D_CONV)),
        _weight((D_CONV + D_ATTN, D_MODEL)),
    ] + _ln_specs()
    args = [sinks, x, st, kc, vc, win, cw, wout, g, b]
    aliases = {}
    out_win_blk = win_blk
    if prev_windows is None:
        out_win_blk = pl.BlockSpec((DEPTH, bb, D_KV, WINDOW), lambda i, s: (0, i, 0, 0))
    else:
        n_in = len(args)
        in_specs += [pl.BlockSpec(memory_space=pl.ANY)] * 2
        args += list(prev_windows)
        aliases = {n_in: 2, n_in + 1: 3}
    grid_spec = pltpu.PrefetchScalarGridSpec(
        num_scalar_prefetch=1,
        grid=(nb // bb,),
        in_specs=in_specs,
        out_specs=[pl.BlockSpec((bb, D_MODEL), row), pl.BlockSpec((bb, (CONV_W - 1) * D_CONV), row),
                   out_win_blk, out_win_blk],
        scratch_shapes=[pltpu.VMEM((bb, D_CONV + D_ATTN), BF16)]
        + [pltpu.VMEM((bb, N_HEADS, LANES), F32)] * 4,
    )
    return pl.pallas_call(
        kern,
        grid_spec=grid_spec,
        out_shape=[
            jax.ShapeDtypeStruct((nb, D_MODEL), F32),
            jax.ShapeDtypeStruct((nb, (CONV_W - 1) * D_CONV), F32),
            jax.ShapeDtypeStruct((DEPTH, nb, D_KV, WINDOW), F32),
            jax.ShapeDtypeStruct((DEPTH, nb, D_KV, WINDOW), F32),
        ],
        input_output_aliases=aliases,
        compiler_params=_params("arbitrary"),
        name="mix_sample",
    )(*args)


_MEM_HALVES = MEM_HEAD_DIM // LANES
_MEM_ROWS = _MEM_HALVES * MEM_HEADS
_CROSS_SAMPLE_UNROLL = 4


def _cross_sample_kernel(x_ref, mk_ref, mv_ref, wq_ref, wo_ref, g_ref, b_ref, o_ref, q8, o8, obuf,
                         *, bb, layer, ln):
    x = x_ref[...]
    q = _dot(x.astype(BF16), wq_ref[0]) * (MEM_HEAD_DIM ** -0.5)
    for half in range(_MEM_HALVES):
        for h in range(MEM_HEADS):
            c0 = h * MEM_HEAD_DIM + half * LANES
            q8[:, half * MEM_HEADS + h, :] = q[:, c0:c0 + LANES]

    def one_sequence(r, carry):
        kq = mk_ref[0, r].reshape(MEM_LEN, _MEM_ROWS, LANES) * q8[r][None]
        kq = kq + pltpu.roll(kq, MEM_HEADS, 1)
        s = jnp.sum(kq, axis=-1, keepdims=True)
        m = jnp.max(s, axis=0, keepdims=True)
        p = jnp.exp(s - m)
        denom = jnp.sum(p, axis=0, keepdims=True)
        pv = p * mv_ref[0, r].reshape(MEM_LEN, _MEM_ROWS, LANES)
        o8[r] = jnp.sum(pv, axis=0) / denom[0]
        return carry

    lax.fori_loop(0, bb, one_sequence, 0, unroll=_CROSS_SAMPLE_UNROLL)
    for half in range(_MEM_HALVES):
        for h in range(MEM_HEADS):
            c0 = h * MEM_HEAD_DIM + half * LANES
            obuf[:, c0:c0 + LANES] = o8[:, half * MEM_HEADS + h, :].astype(BF16)
    y = _dot(obuf[...], wo_ref[0])
    o_ref[...] = _layer_norm(ALPHA * x + y, g_ref[layer, ln:ln + 1, :], b_ref[layer, ln:ln + 1, :])


def _cross_sample(x, mk, mv, wq, wo, g, b, layer, ln, bb):
    nb = x.shape[0]
    kern = functools.partial(_cross_sample_kernel, bb=bb, layer=layer, ln=ln)
    xblk = pl.BlockSpec((bb, D_MODEL), lambda i: (i, 0))
    mblk = pl.BlockSpec((1, bb, MEM_LEN * _MEM_ROWS, LANES), lambda i: (layer, i, 0, 0))
    return pl.pallas_call(
        kern,
        grid=(nb // bb,),
        in_specs=[xblk, mblk, mblk, _weight((D_MODEL, D_MODEL)), _weight((D_MODEL, D_MODEL))] + _ln_specs(),
        out_specs=xblk,
        out_shape=jax.ShapeDtypeStruct((nb, D_MODEL), F32),
        scratch_shapes=[pltpu.VMEM((bb, _MEM_ROWS, LANES), F32),
                        pltpu.VMEM((bb, _MEM_ROWS, LANES), F32),
                        pltpu.VMEM((bb, D_MODEL), BF16)],
        compiler_params=_params("arbitrary"),
        name="cross_sample",
    )(x, mk, mv, wq, wo, g, b)


def _head_piece_cols():
    return [(half * MEM_HEADS + h, h * MEM_HEAD_DIM + half * LANES)
            for half in range(_MEM_HALVES) for h in range(MEM_HEADS)]


def _cross_q_kernel(x_ref, wq_ref, q8_ref):
    q = _dot(x_ref[...].astype(BF16), wq_ref[0]) * (MEM_HEAD_DIM ** -0.5)
    for row, col in _head_piece_cols():
        q8_ref[:, row, :] = q[:, col:col + LANES]


def _cross_q(x, wq):
    nb = x.shape[0]
    return pl.pallas_call(
        _cross_q_kernel,
        grid=(1,),
        in_specs=[_resident((nb, D_MODEL)), _weight((D_MODEL, D_MODEL))],
        out_specs=pl.BlockSpec((nb, _MEM_ROWS, LANES), lambda i: (0, 0, 0)),
        out_shape=jax.ShapeDtypeStruct((nb, _MEM_ROWS, LANES), F32),
        compiler_params=_params("arbitrary"),
        name="cross_sample_q",
    )(x, wq)


def _cross_out_kernel(x_ref, o8_ref, wo_ref, g_ref, b_ref, o_ref, obuf, *, layer, ln):
    for row, col in _head_piece_cols():
        obuf[:, col:col + LANES] = o8_ref[:, row, :].astype(BF16)
    y = _dot(obuf[...], wo_ref[0])
    o_ref[...] = _layer_norm(ALPHA * x_ref[...] + y, g_ref[layer, ln:ln + 1, :], b_ref[layer, ln:ln + 1, :])


def _cross_out(x, o8, wo, g, b, layer, ln):
    nb = x.shape[0]
    return pl.pallas_call(
        functools.partial(_cross_out_kernel, layer=layer, ln=ln),
        grid=(1,),
        in_specs=[_resident((nb, D_MODEL)), _resident((nb, _MEM_ROWS, LANES)), _weight((D_MODEL, D_MODEL))]
        + _ln_specs(),
        out_specs=pl.BlockSpec((nb, D_MODEL), lambda i: (0, 0)),
        out_shape=jax.ShapeDtypeStruct((nb, D_MODEL), F32),
        scratch_shapes=[pltpu.VMEM((nb, D_MODEL), BF16)],
        compiler_params=_params("arbitrary"),
        name="cross_sample_out",
    )(x, o8, wo, g, b)
SC_LANES = 16
SC_TOKENS = 16
_SC_ROWS = SC_TOKENS * _MEM_ROWS
_SC_CHUNKS = MEM_LEN // SC_TOKENS
_SC_PIECES = LANES // SC_LANES


def _sc_cross_attention(q8, mk_tiles, mv_tiles, layer, seq0, nseq):
    total_seq = mk_tiles.shape[1]
    kflat = mk_tiles.reshape(-1, LANES)
    vflat = mv_tiles.reshape(-1, LANES)
    chunk0 = (layer * total_seq + seq0) * _SC_CHUNKS
    mesh = plsc.VectorSubcoreMesh(core_axis_name="c", subcore_axis_name="s")

    @pl.kernel(out_type=jax.ShapeDtypeStruct((nseq * _MEM_ROWS, LANES), F32), mesh=mesh,
               scratch_types=[pltpu.VMEM((MEM_HEADS, SC_LANES), F32), pltpu.VMEM((MEM_HEADS, SC_LANES), F32)],
               compiler_params=pltpu.CompilerParams(needs_layout_passes=False))
    def attend(q_hbm, k_hbm, v_hbm, o_hbm, m_ref, l_ref):
        def body(idx, q_vmem, k_vmem, v_vmem, o_vmem):
            chunk = idx[1]
            lane = lax.iota(jnp.int32, SC_LANES)
            first = jnp.full((SC_LANES,), chunk, jnp.int32) == 0

            @pl.when(chunk == 0)
            def _():
                for h in range(MEM_HEADS):
                    m_ref[h, :] = jnp.zeros((SC_LANES,), F32)
                    l_ref[h, :] = jnp.zeros((SC_LANES,), F32)
                for r in range(_MEM_ROWS):
                    for j in range(_SC_PIECES):
                        o_vmem[r, pl.ds(j * SC_LANES, SC_LANES)] = jnp.zeros((SC_LANES,), F32)

            def one_head(h, carry):
                acc = [jnp.zeros((SC_LANES,), F32) for _ in range(SC_TOKENS)]
                for half in range(_MEM_HALVES):
                    r = half * MEM_HEADS + h
                    for j in range(_SC_PIECES):
                        sl = pl.ds(j * SC_LANES, SC_LANES)
                        qv = q_vmem[r, sl]
                        for t in range(SC_TOKENS):
                            acc[t] = acc[t] + k_vmem[t * _MEM_ROWS + r, sl] * qv
                s = jnp.zeros((SC_LANES,), F32)
                for t in range(SC_TOKENS):
                    s = jnp.where(lane == t, jnp.sum(acc[t]), s)

                m_old = m_ref[h, :]
                c_max = jnp.full((SC_LANES,), jnp.max(s), F32)
                m_new = jnp.where(first, c_max, jnp.maximum(m_old, c_max))
                alpha = jnp.where(first, 0.0, jnp.exp(m_old - m_new))
                p = jnp.exp(s - m_new)
                l_ref[h, :] = l_ref[h, :] * alpha + jnp.sum(p)
                m_ref[h, :] = m_new

                pt = [jnp.full((SC_LANES,), p[t], F32) for t in range(SC_TOKENS)]
                for half in range(_MEM_HALVES):
                    r = half * MEM_HEADS + h
                    for j in range(_SC_PIECES):
                        sl = pl.ds(j * SC_LANES, SC_LANES)
                        o = o_vmem[r, sl] * alpha
                        for t in range(SC_TOKENS):
                            o = o + pt[t] * v_vmem[t * _MEM_ROWS + r, sl]
                        o_vmem[r, sl] = o
                return carry

            lax.fori_loop(0, MEM_HEADS, one_head, 0)

            @pl.when(chunk == _SC_CHUNKS - 1)
            def _():
                for h in range(MEM_HEADS):
                    inv = 1.0 / l_ref[h, :]
                    for half in range(_MEM_HALVES):
                        r = half * MEM_HEADS + h
                        for j in range(_SC_PIECES):
                            sl = pl.ds(j * SC_LANES, SC_LANES)
                            o_vmem[r, sl] = o_vmem[r, sl] * inv

        pltpu.emit_pipeline(
            body,
            grid=(nseq, _SC_CHUNKS),
            in_specs=[pl.BlockSpec((_MEM_ROWS, LANES), lambda s, c: (seq0 + s, 0)),
                      pl.BlockSpec((_SC_ROWS, LANES), lambda s, c: (chunk0 + s * _SC_CHUNKS + c, 0)),
                      pl.BlockSpec((_SC_ROWS, LANES), lambda s, c: (chunk0 + s * _SC_CHUNKS + c, 0))],
            out_specs=[pl.BlockSpec((_MEM_ROWS, LANES), lambda s, c: (s, 0))],
            core_axis_name=("c", "s"),
            dimension_semantics=(pltpu.PARALLEL, pltpu.ARBITRARY),
            _explicit_indices=True,
        )(q_hbm, k_hbm, v_hbm, o_hbm)

    return attend(q8, kflat, vflat)


TM_PROMPT = 1024
BB_MIX_SAMPLE = 32
BB_CROSS_SAMPLE = 8


def kernel(x_prompt, x_sample, mem_prompt, cache_win_k, cache_win_v, state_conv, cache_mem_k, cache_mem_v,
           ln_g, ln_b, ffn1_w_gu, ffn1_w_down, w_in, conv_w, attn_sinks, w_out,
           w_cq, w_mk, w_mv, w_co, ffn2_w_gu, ffn2_w_down):
    bsz, seq, _ = x_prompt.shape
    nsmp = x_sample.shape[0]
    yp = x_prompt
    ys = x_sample.reshape(nsmp, D_MODEL)
    mem2 = mem_prompt.reshape(bsz * MEM_LEN, D_MODEL)

    stacked = dict(ffn1=(ffn1_w_gu, ffn1_w_down), ffn2=(ffn2_w_gu, ffn2_w_down),
                   mix=(w_in, w_out), cross=(w_cq, w_co, w_mk, w_mv))
    wb = {("ffn1", 0): tuple(w[0:1].astype(BF16) for w in stacked["ffn1"])}

    def jobs(*groups):
        return [(w, layer) for name, layer in groups for w in stacked[name]]

    def keep(groups, casts):
        casts = list(casts)
        for name, layer in groups:
            wb[(name, layer)] = tuple(casts[:len(stacked[name])])
            del casts[:len(stacked[name])]

    def token_tiles(c):
        c = c.reshape(DEPTH, nsmp, MEM_LEN, MEM_HEADS, _MEM_HALVES, LANES)
        return c.transpose(0, 1, 2, 4, 3, 5).reshape(DEPTH, nsmp, MEM_LEN * _MEM_ROWS, LANES)

    mem_k_tiles, mem_v_tiles = token_tiles(cache_mem_k), token_tiles(cache_mem_v)

    def window_lanes(c):
        return c.transpose(0, 1, 3, 4, 2).reshape(DEPTH, nsmp, D_KV, WINDOW)

    win_k_lanes, win_v_lanes = window_lanes(cache_win_k), window_lanes(cache_win_v)
    new_windows = None
    mem_tiles = None

    wkp, wvp, cvp, cvs = [], [], [], []
    for l in range(DEPTH):
        more = l + 1 < DEPTH
        side = [("mix", 0), ("cross", 0)] if l == 0 else [("ffn2", l)]
        yp, ys, *casts = _ffn(yp.reshape(bsz * seq, D_MODEL), ys, *wb[("ffn1", l)], ln_g, ln_b, l, 0, TM_PROMPT,
                              jobs(*side))
        keep(side, casts)
        yp = yp.reshape(bsz, seq, D_MODEL)

        win, wout = wb[("mix", l)]
        side = [("ffn2", 0)] if l == 0 else []
        yp, nk, nv, ncv, *casts = _mix_prompt(yp, attn_sinks, win, conv_w, wout, ln_g, ln_b, l, 1, TM_PROMPT,
                                              jobs(*side))
        keep(side, casts)
        wkp.append(nk.reshape(bsz, WINDOW, N_KV_HEADS, HEAD_DIM))
        wvp.append(nv.reshape(bsz, WINDOW, N_KV_HEADS, HEAD_DIM))
        cvp.append(ncv)
        ys, nst, nks, nvs = _mix_sample(
            ys, attn_sinks, state_conv[l].reshape(nsmp, (CONV_W - 1) * D_CONV),
            win_k_lanes, win_v_lanes, new_windows, win, conv_w, wout, ln_g, ln_b, l, 1, BB_MIX_SAMPLE)
        new_windows = (nks, nvs)
        cvs.append(nst.reshape(nsmp, CONV_W - 1, D_CONV))

        wcq, wco, wmk, wmv = wb[("cross", l)]
        mk_all, mv_all, mkb, mvb = _memkv(mem2, wmk, wmv, l, mem_tiles)
        mem_tiles = (mk_all, mv_all)
        side = [("ffn1", l + 1)] if more else []
        yp, *casts = _cross_prompt(yp, mkb.reshape(bsz, MEM_LEN, D_MODEL), mvb.reshape(bsz, MEM_LEN, D_MODEL),
                                   wcq, wco, ln_g, ln_b, l, 2, TM_PROMPT, jobs(*side))
        keep(side, casts)
        q8 = _cross_q(ys, wcq).reshape(nsmp * _MEM_ROWS, LANES)
        o8 = _sc_cross_attention(q8, mem_k_tiles, mem_v_tiles, l, 0, nsmp)
        ys = _cross_out(ys, o8.reshape(nsmp, _MEM_ROWS, LANES), wco, ln_g, ln_b, l, 2)

        side = [("mix", l + 1), ("cross", l + 1)] if more else []
        yp, ys, *casts = _ffn(yp.reshape(bsz * seq, D_MODEL), ys, *wb[("ffn2", l)], ln_g, ln_b, l, 3, TM_PROMPT,
                              jobs(*side))
        keep(side, casts)
        yp = yp.reshape(bsz, seq, D_MODEL)

    def window_rows(c):
        return c.reshape(DEPTH, nsmp, N_KV_HEADS, HEAD_DIM, WINDOW).transpose(0, 1, 4, 2, 3)

    def token_rows(c):
        c = c.reshape(DEPTH, bsz, MEM_LEN, _MEM_HALVES, MEM_HEADS, LANES)
        return c.transpose(0, 1, 2, 4, 3, 5).reshape(DEPTH, bsz, MEM_LEN, MEM_HEADS, MEM_HEAD_DIM)

    return (yp, ys.reshape(nsmp, 1, D_MODEL),
            jnp.stack(wkp), jnp.stack(wvp), jnp.stack(cvp), token_rows(mem_tiles[0]), token_rows(mem_tiles[1]),
            window_rows(new_windows[0]), window_rows(new_windows[1]), jnp.stack(cvs))
```

```python
import functools

import jax
import jax.numpy as jnp
from jax import lax
from jax.experimental import pallas as pl
from jax.experimental.pallas import tpu as pltpu
from jax.experimental.pallas import tpu_sc as plsc

D_MODEL = 1024
DEPTH = 2
D_CONV = 512
CONV_W = 3
HEAD_DIM = 64
N_HEADS = 8
N_KV_HEADS = 2
GROUP = N_HEADS // N_KV_HEADS
WINDOW = 128
D_ATTN = N_HEADS * HEAD_DIM
D_KV = N_KV_HEADS * HEAD_DIM
IN_COLS = 3 * D_CONV + D_ATTN + 2 * D_KV
MEM_LEN = 256
MEM_HEADS = 4
MEM_HEAD_DIM = D_MODEL // MEM_HEADS
D_FF = 2816
N_LN = 4
ALPHA = (2.0 * DEPTH) ** 0.25
LN_EPS = 1e-5

_C_BG, _C_CG, _C_HC, _C_Q, _C_KV = 0, D_CONV, 2 * D_CONV, 3 * D_CONV, 3 * D_CONV + D_ATTN

LANES = 128
SUBLANES = 8
MXU_COLS = 256
VMEM_LIMIT_BYTES = 56 * 1024 * 1024

BF16 = jnp.bfloat16
F32 = jnp.float32


def _dot(a, b):
    return jnp.dot(a, b, preferred_element_type=F32)


def _dot_nt(a, b):
    return lax.dot_general(a, b, (((1,), (1,)), ((), ())), preferred_element_type=F32)


def _layer_norm(z, g, b):
    mu = jnp.mean(z, axis=-1, keepdims=True)
    d = z - mu
    var = jnp.mean(d * d, axis=-1, keepdims=True)
    return d * lax.rsqrt(var + LN_EPS) * g + b


def _resident(shape):
    zeros = (0,) * len(shape)
    return pl.BlockSpec(shape, lambda *_: zeros, pipeline_mode=pl.Buffered(1))


def _weight(shape):
    return _resident((1,) + shape)


def _cast_specs(jobs, n_steps, step_of):
    in_specs, out_specs, out_shapes, args = [], [], [], []
    for w, layer in jobs:
        _, k, n = w.shape
        rows = k // n_steps
        assert rows * n_steps == k and rows % (2 * SUBLANES) == 0, (w.shape, n_steps)
        in_specs.append(pl.BlockSpec((1, rows, n), lambda *idx, layer=layer: (layer, step_of(*idx), 0)))
        out_specs.append(pl.BlockSpec((1, rows, n), lambda *idx: (0, step_of(*idx), 0)))
        out_shapes.append(jax.ShapeDtypeStruct((1, k, n), BF16))
        args.append(w)
    return in_specs, out_specs, out_shapes, args


def _run_casts(src_refs, dst_refs):
    for src, dst in zip(src_refs, dst_refs, strict=True):
        dst[...] = src[...].astype(BF16)


def _ln_specs():
    return [_resident((DEPTH, N_LN, D_MODEL)), _resident((DEPTH, N_LN, D_MODEL))]


def _params(*sem):
    return pltpu.CompilerParams(dimension_semantics=sem, vmem_limit_bytes=VMEM_LIMIT_BYTES)


NORM_ROWS = 256


def _project_and_norm(lhs_ref, w, x_rows, scale, g, b, write_rows, rows):
    for r0 in range(0, rows, min(NORM_ROWS, rows)):
        r1 = r0 + min(NORM_ROWS, rows)
        y = _dot(lhs_ref[r0:r1, :], w)
        write_rows(r0, r1, _layer_norm(ALPHA * x_rows(r0, r1) + scale * y, g, b))


def _swiglu(gate, up):
    return (gate * jax.nn.sigmoid(gate) * up).astype(BF16)


def _ffn_rows(x_ref, o_ref, wg_ref, wu_ref, wd_ref, h_ref, g, b):
    rows = x_ref.shape[0]
    xb = x_ref[...].astype(BF16)
    for c in range(D_FF // MXU_COLS):
        lo = c * MXU_COLS
        h_ref[0:rows, lo:lo + MXU_COLS] = _swiglu(_dot(xb, wg_ref[0, :, lo:lo + MXU_COLS]),
                                                  _dot(xb, wu_ref[0, :, lo:lo + MXU_COLS]))

    def write_rows(r0, r1, y):
        o_ref[r0:r1, :] = y

    _project_and_norm(h_ref, wd_ref[0], lambda r0, r1: x_ref[r0:r1, :], 0.5, g, b, write_rows, rows)


def _ffn_kernel(x_ref, xs_ref, wg_ref, wu_ref, wd_ref, g_ref, b_ref, *rest,
                layer, ln, n_cast, n_tiles, cross_ln, n_first):
    n_pre = (0 if cross_ln is None else 2) + (1 if n_first else 0)
    pre, rest = rest[:n_pre], rest[n_pre:]
    cast_src, (o_ref, os_ref), rest = rest[:n_cast], rest[n_cast:n_cast + 2], rest[n_cast + 2:]
    cast_dst, scratch = rest[:n_cast], rest[n_cast:]
    h_ref = scratch[0]
    g = g_ref[layer, ln:ln + 1, :]
    b = b_ref[layer, ln:ln + 1, :]
    step = pl.program_id(0)

    if n_first:
        @pl.when(step < n_first)
        def _():
            _run_casts(cast_src, cast_dst)
            o_ref[...] = pre[-1][...]

    @pl.when((step < n_tiles) & (step >= n_first))
    def _():
        _run_casts(cast_src, cast_dst)
        _ffn_rows(x_ref, o_ref, wg_ref, wu_ref, wd_ref, h_ref, g, b)

    @pl.when(step == n_tiles)
    def _():
        if cross_ln is None:
            _ffn_rows(xs_ref, os_ref, wg_ref, wu_ref, wd_ref, h_ref, g, b)
        else:
            o8_ref, wo_ref = pre[:2]
            obuf, xs_scr = scratch[1:]
            for row, col in _head_piece_cols():
                obuf[:, col:col + LANES] = o8_ref[:, row, :].astype(BF16)
            xs_scr[...] = _layer_norm(ALPHA * xs_ref[...] + _dot(obuf[...], wo_ref[0]),
                                      g_ref[layer, cross_ln:cross_ln + 1, :], b_ref[layer, cross_ln:cross_ln + 1, :])
            _ffn_rows(xs_scr, os_ref, wg_ref, wu_ref, wd_ref, h_ref, g, b)


def _ffn(x, xs, wgu, wd, g, b, layer, ln, tm, casts=(), sample_cross=None, first_tile=None):
    m = x.shape[0]
    ns = xs.shape[0]
    n_tiles = m // tm
    tile = lambda i: (jnp.minimum(i, n_tiles - 1), 0)
    c_in, c_out, c_shapes, c_args = _cast_specs(casts, n_tiles, lambda i: jnp.minimum(i, n_tiles - 1))
    half = lambda j: pl.BlockSpec((1, D_MODEL, D_FF), lambda i: (0, 0, j), pipeline_mode=pl.Buffered(1))
    if isinstance(wgu, tuple):
        w_args, w_specs = list(wgu), [half(0), half(0)]
    else:
        w_args, w_specs = [wgu, wgu], [half(0), half(1)]
    pre_specs, pre_args, pre_scratch, cross_ln = [], [], [], None
    if sample_cross is not None:
        o8, wo, cross_ln = sample_cross
        pre_specs = [_resident((ns, _MEM_ROWS, LANES)), _weight((D_MODEL, D_MODEL))]
        pre_args = [o8, wo]
        pre_scratch = [pltpu.VMEM((ns, D_MODEL), BF16), pltpu.VMEM((ns, D_MODEL), F32)]
    n_first = 0
    if first_tile is not None:
        n_first = first_tile.shape[0] // tm
        pre_specs.append(pl.BlockSpec((tm, D_MODEL), lambda i: (jnp.minimum(i, n_first - 1), 0),
                                      pipeline_mode=pl.Buffered(1)))
        pre_args.append(first_tile)
    return pl.pallas_call(
        functools.partial(_ffn_kernel, layer=layer, ln=ln, n_cast=len(casts), n_tiles=n_tiles, cross_ln=cross_ln,
                          n_first=n_first),
        grid=(n_tiles + 1,),
        in_specs=[pl.BlockSpec((tm, D_MODEL), tile), _resident((ns, D_MODEL))] + w_specs
        + [_weight((D_FF, D_MODEL))] + _ln_specs() + pre_specs + c_in,
        out_specs=[pl.BlockSpec((tm, D_MODEL), tile), pl.BlockSpec((ns, D_MODEL), lambda i: (0, 0))] + c_out,
        out_shape=[jax.ShapeDtypeStruct((m, D_MODEL), F32), jax.ShapeDtypeStruct((ns, D_MODEL), F32)] + c_shapes,
        scratch_shapes=[pltpu.VMEM((tm, D_FF), BF16)] + pre_scratch,
        compiler_params=_params("arbitrary"),
        name="ffn_ln",
    )(x, xs, *w_args, wd, g, b, *pre_args, *c_args)


def _ffn_first_tile_kernel(x_ref, wg_ref, wu_ref, wd_ref, g_ref, b_ref, o_ref, wgb_ref, wub_ref, wdb_ref,
                           xb_scr, h_scr, wd_scr, *, layer, ln):
    c = pl.program_id(0)
    n_chunks = h_scr.shape[0]

    @pl.when(c == 0)
    def _():
        xb_scr[...] = x_ref[...].astype(BF16)

    wg = wg_ref[0].astype(BF16)
    wu = wu_ref[0].astype(BF16)
    wd = wd_ref[0].astype(BF16)
    wgb_ref[0] = wg
    wub_ref[0] = wu
    wdb_ref[0] = wd
    wd_scr[c] = wd
    xb = xb_scr[...]
    h_scr[c] = _swiglu(_dot(xb, wg), _dot(xb, wu))

    @pl.when(c == n_chunks - 1)
    def _():
        rows = x_ref.shape[0]
        g = g_ref[layer, ln:ln + 1, :]
        b = b_ref[layer, ln:ln + 1, :]
        for r0 in range(0, rows, NORM_ROWS):
            r1 = r0 + NORM_ROWS
            y = _dot(h_scr[0, r0:r1, :], wd_scr[0])
            for k in range(1, n_chunks):
                y = y + _dot(h_scr[k, r0:r1, :], wd_scr[k])
            o_ref[r0:r1, :] = _layer_norm(ALPHA * x_ref[r0:r1, :] + 0.5 * y, g, b)


def _ffn_first_tile(x, wgu32, wd32, g, b, layer, ln, tm):
    n_chunks = D_FF // MXU_COLS
    return pl.pallas_call(
        functools.partial(_ffn_first_tile_kernel, layer=layer, ln=ln),
        grid=(n_chunks,),
        in_specs=[_resident((tm, D_MODEL)),
                  pl.BlockSpec((1, D_MODEL, MXU_COLS), lambda c: (layer, 0, c)),
                  pl.BlockSpec((1, D_MODEL, MXU_COLS), lambda c: (layer, 0, n_chunks + c)),
                  pl.BlockSpec((1, MXU_COLS, D_MODEL), lambda c: (layer, c, 0))] + _ln_specs(),
        out_specs=[pl.BlockSpec((tm, D_MODEL), lambda c: (0, 0)),
                   pl.BlockSpec((1, D_MODEL, MXU_COLS), lambda c: (0, 0, c)),
                   pl.BlockSpec((1, D_MODEL, MXU_COLS), lambda c: (0, 0, c)),
                   pl.BlockSpec((1, MXU_COLS, D_MODEL), lambda c: (0, c, 0))],
        out_shape=[jax.ShapeDtypeStruct((tm, D_MODEL), F32),
                   jax.ShapeDtypeStruct((1, D_MODEL, D_FF), BF16),
                   jax.ShapeDtypeStruct((1, D_MODEL, D_FF), BF16),
                   jax.ShapeDtypeStruct((1, D_FF, D_MODEL), BF16)],
        scratch_shapes=[pltpu.VMEM((tm, D_MODEL), BF16),
                        pltpu.VMEM((n_chunks, tm, MXU_COLS), BF16),
                        pltpu.VMEM((n_chunks, MXU_COLS, D_MODEL), BF16)],
        compiler_params=_params("arbitrary"),
        name="ffn_first_tile",
    )(x, wgu32, wgu32, wd32, g, b)


def _dup_halves(a):
    lane = lax.broadcasted_iota(jnp.int32, a.shape, 1)
    low = lane < HEAD_DIM
    rolled = pltpu.roll(a, HEAD_DIM, 1)
    return jnp.where(low, a, rolled), jnp.where(low, rolled, a)


def _sink_column(sink_ref, layer, first_head, rows_per_head, n_heads):
    rows = n_heads * rows_per_head
    r = lax.broadcasted_iota(jnp.int32, (rows, 1), 0)
    col = jnp.full((rows, 1), sink_ref[layer, first_head + n_heads - 1], F32)
    for g in range(n_heads - 2, -1, -1):
        col = jnp.where(r < (g + 1) * rows_per_head, sink_ref[layer, first_head + g], col)
    return col


def _mix_prompt_kernel(sink_ref, x_ref, win_ref, cw_ref, wout_ref, g_ref, b_ref, *rest, tm, layer, ln, n_cast):
    cast_src, rest = rest[:n_cast], rest[n_cast:]
    (o_ref, nk_ref, nv_ref, nc_ref), rest = rest[:4], rest[4:]
    cast_dst, (ubuf, qbuf, kd0, kd1, vd0, vd1, zbuf) = rest[:n_cast], rest[n_cast:]
    _run_casts(cast_src, cast_dst)
    seq_start = pl.program_id(1) == 0
    xb = x_ref[0].astype(BF16)

    def proj(col, width):
        return _dot(xb, win_ref[0, :, col:col + width])

    @pl.when(seq_start)
    def _():
        ubuf[0:SUBLANES, :] = jnp.zeros((SUBLANES, D_CONV), F32)
        for r in (kd0, kd1, vd0, vd1):
            r[0:WINDOW, :] = jnp.zeros((WINDOW, LANES), BF16)

    u = proj(_C_CG, D_CONV) * proj(_C_HC, D_CONV)
    ubuf[SUBLANES:SUBLANES + tm, :] = u
    conv = (cw_ref[layer, 0:1, :] * ubuf[SUBLANES - 2:SUBLANES - 2 + tm, :]
            + cw_ref[layer, 1:2, :] * ubuf[SUBLANES - 1:SUBLANES - 1 + tm, :]
            + cw_ref[layer, 2:3, :] * u)
    zbuf[:, 0:D_CONV] = (proj(_C_BG, D_CONV) * conv).astype(BF16)
    nc_ref[0] = u[tm - (CONV_W - 1):tm, :]
    ubuf[0:SUBLANES, :] = u[tm - SUBLANES:tm, :]

    qbuf[...] = (proj(_C_Q, D_ATTN) * (HEAD_DIM ** -0.5)).astype(BF16)
    k = proj(_C_KV, D_KV)
    v = proj(_C_KV + D_KV, D_KV)
    nk_ref[0] = k[tm - WINDOW:tm, :]
    nv_ref[0] = v[tm - WINDOW:tm, :]
    ka, kb = _dup_halves(k)
    va, vb = _dup_halves(v)
    kd0[WINDOW:WINDOW + tm, :] = ka.astype(BF16)
    kd1[WINDOW:WINDOW + tm, :] = kb.astype(BF16)
    vd0[WINDOW:WINDOW + tm, :] = va.astype(BF16)
    vd1[WINDOW:WINDOW + tm, :] = vb.astype(BF16)

    rows = GROUP * WINDOW
    a_idx = lax.broadcasted_iota(jnp.int32, (rows, 2 * WINDOW), 0) % WINDOW
    c_idx = lax.broadcasted_iota(jnp.int32, (rows, 2 * WINDOW), 1)
    band = (c_idx >= a_idx) & (c_idx <= a_idx + WINDOW)
    first_lb = jnp.where(seq_start, WINDOW, 0)
    low = lax.broadcasted_iota(jnp.int32, (WINDOW, LANES), 1) < HEAD_DIM
    zero = jnp.zeros((WINDOW, LANES), BF16)

    for n in range(tm // WINDOW):
        r0 = n * WINDOW
        mask = (band & (c_idx >= first_lb)) if n == 0 else band
        for j, (kd, vd) in enumerate(((kd0, vd0), (kd1, vd1))):
            qa = qbuf[r0:r0 + WINDOW, (2 * j) * LANES:(2 * j + 1) * LANES]
            qb = qbuf[r0:r0 + WINDOW, (2 * j + 1) * LANES:(2 * j + 2) * LANES]
            qs = jnp.concatenate([jnp.where(low, qa, zero), jnp.where(low, zero, qa),
                                  jnp.where(low, qb, zero), jnp.where(low, zero, qb)], axis=0)
            s = _dot_nt(qs, kd[r0:r0 + 2 * WINDOW, :])
            s = jnp.where(mask, s, -jnp.inf)
            sink = _sink_column(sink_ref, layer, GROUP * j, WINDOW, GROUP)
            m = jnp.maximum(jnp.max(s, axis=-1, keepdims=True), sink)
            p = jnp.exp(s - m)
            denom = jnp.sum(p, axis=-1, keepdims=True) + jnp.exp(sink - m)
            o = _dot(p.astype(BF16), vd[r0:r0 + 2 * WINDOW, :]) / denom
            za = jnp.where(low, o[0:WINDOW], o[WINDOW:2 * WINDOW])
            zb = jnp.where(low, o[2 * WINDOW:3 * WINDOW], o[3 * WINDOW:4 * WINDOW])
            c0 = D_CONV + (2 * j) * LANES
            zbuf[r0:r0 + WINDOW, c0:c0 + LANES] = za.astype(BF16)
            zbuf[r0:r0 + WINDOW, c0 + LANES:c0 + 2 * LANES] = zb.astype(BF16)

    for r in (kd0, kd1, vd0, vd1):
        r[0:WINDOW, :] = r[tm:tm + WINDOW, :]

    def write_rows(r0, r1, y):
        o_ref[0, r0:r1, :] = y

    _project_and_norm(zbuf, wout_ref[0], lambda r0, r1: x_ref[0, r0:r1, :], 1.0,
                      g_ref[layer, ln:ln + 1, :], b_ref[layer, ln:ln + 1, :], write_rows, tm)


def _mix_prompt(x, sinks, win, cw, wout, g, b, layer, ln, tm, casts=()):
    bsz, seq, _ = x.shape
    tps = seq // tm
    kern = functools.partial(_mix_prompt_kernel, tm=tm, layer=layer, ln=ln, n_cast=len(casts))
    per_seq = lambda bi, i, s: (bi, 0, 0)
    c_in, c_out, c_shapes, c_args = _cast_specs(casts, bsz * tps, lambda bi, i, *_: bi * tps + i)
    grid_spec = pltpu.PrefetchScalarGridSpec(
        num_scalar_prefetch=1,
        grid=(bsz, tps),
        in_specs=[
            pl.BlockSpec((1, tm, D_MODEL), lambda bi, i, s: (bi, i, 0)),
            _weight((D_MODEL, IN_COLS)),
            _resident((DEPTH, CONV_W, D_CONV)),
            _weight((D_CONV + D_ATTN, D_MODEL)),
        ] + _ln_specs() + c_in,
        out_specs=[
            pl.BlockSpec((1, tm, D_MODEL), lambda bi, i, s: (bi, i, 0)),
            pl.BlockSpec((1, WINDOW, D_KV), per_seq),
            pl.BlockSpec((1, WINDOW, D_KV), per_seq),
            pl.BlockSpec((1, CONV_W - 1, D_CONV), per_seq),
        ] + c_out,
        scratch_shapes=[
            pltpu.VMEM((tm + SUBLANES, D_CONV), F32),
            pltpu.VMEM((tm, D_ATTN), BF16),
            pltpu.VMEM((tm + WINDOW, LANES), BF16),
            pltpu.VMEM((tm + WINDOW, LANES), BF16),
            pltpu.VMEM((tm + WINDOW, LANES), BF16),
            pltpu.VMEM((tm + WINDOW, LANES), BF16),
            pltpu.VMEM((tm, D_CONV + D_ATTN), BF16),
        ],
    )
    return pl.pallas_call(
        kern,
        grid_spec=grid_spec,
        out_shape=[
            jax.ShapeDtypeStruct((bsz, seq, D_MODEL), F32),
            jax.ShapeDtypeStruct((bsz, WINDOW, D_KV), F32),
            jax.ShapeDtypeStruct((bsz, WINDOW, D_KV), F32),
            jax.ShapeDtypeStruct((bsz, CONV_W - 1, D_CONV), F32),
        ] + c_shapes,
        compiler_params=_params("arbitrary", "arbitrary"),
        name="mix_prompt",
    )(sinks, x, win, cw, wout, g, b, *c_args)


def _cross_prompt_kernel(x_ref, mem_ref, wq_ref, wo_ref, wk_ref, wv_ref, g_ref, b_ref, *rest,
                         layer, ln, n_cast, n_prev):
    rest = rest[n_prev:]
    cast_src, (o_ref, k_ref, v_ref), rest = rest[:n_cast], rest[n_cast:n_cast + 3], rest[n_cast + 3:]
    cast_dst, (obuf, kb, vb) = rest[:n_cast], rest[n_cast:]
    _run_casts(cast_src, cast_dst)

    @pl.when(pl.program_id(1) == 0)
    def _():
        mb = mem_ref[0].astype(BF16)
        k = _dot(mb, wk_ref[0])
        v = _dot(mb, wv_ref[0])
        kb[...] = k.astype(BF16)
        vb[...] = v.astype(BF16)
        slab = layer if k_ref.shape[0] > 1 else 0
        for other in range(k_ref.shape[0]):
            if other != slab:
                k_ref[other] = jnp.zeros(k_ref.shape[1:], F32)
                v_ref[other] = jnp.zeros(v_ref.shape[1:], F32)
        for row, col in _head_piece_cols():
            k_ref[slab, 0, :, row, :] = k[:, col:col + LANES]
            v_ref[slab, 0, :, row, :] = v[:, col:col + LANES]

    q = _dot(x_ref[0].astype(BF16), wq_ref[0]) * (MEM_HEAD_DIM ** -0.5)
    for h in range(MEM_HEADS):
        c0 = h * MEM_HEAD_DIM
        s = _dot_nt(q[:, c0:c0 + MEM_HEAD_DIM].astype(BF16), kb[:, c0:c0 + MEM_HEAD_DIM])
        m = jnp.max(s, axis=-1, keepdims=True)
        p = jnp.exp(s - m)
        denom = jnp.sum(p, axis=-1, keepdims=True)
        o = _dot(p.astype(BF16), vb[:, c0:c0 + MEM_HEAD_DIM]) / denom
        obuf[:, c0:c0 + MEM_HEAD_DIM] = o.astype(BF16)

    def write_rows(r0, r1, y):
        o_ref[0, r0:r1, :] = y

    _project_and_norm(obuf, wo_ref[0], lambda r0, r1: x_ref[0, r0:r1, :], 1.0,
                      g_ref[layer, ln:ln + 1, :], b_ref[layer, ln:ln + 1, :], write_rows, x_ref.shape[1])


def _cross_prompt(x, mem, wq, wo, wk, wv, g, b, layer, ln, tm, prev, casts=()):
    bsz, seq, _ = x.shape
    tps = seq // tm
    xblk = pl.BlockSpec((1, tm, D_MODEL), lambda bi, i: (bi, i, 0))
    c_in, c_out, c_shapes, c_args = _cast_specs(casts, bsz * tps, lambda bi, i: bi * tps + i)
    in_specs = [xblk, pl.BlockSpec((1, MEM_LEN, D_MODEL), lambda bi, i: (bi, 0, 0))] \
        + [_weight((D_MODEL, D_MODEL))] * 4 + _ln_specs()
    args = [x, mem, wq, wo, wk, wv, g, b]
    aliases = {}
    if prev is None:
        tile_blk = pl.BlockSpec((DEPTH, 1, MEM_LEN, _MEM_ROWS, LANES), lambda bi, i: (0, bi, 0, 0, 0))
    else:
        tile_blk = pl.BlockSpec((1, 1, MEM_LEN, _MEM_ROWS, LANES), lambda bi, i: (layer, bi, 0, 0, 0))
        aliases = {len(args): 1, len(args) + 1: 2}
        in_specs += [pl.BlockSpec(memory_space=pl.ANY)] * 2
        args += list(prev)
    tiles = jax.ShapeDtypeStruct((DEPTH, bsz, MEM_LEN, _MEM_ROWS, LANES), F32)
    return pl.pallas_call(
        functools.partial(_cross_prompt_kernel, layer=layer, ln=ln, n_cast=len(casts), n_prev=len(aliases)),
        grid=(bsz, tps),
        in_specs=in_specs + c_in,
        out_specs=[xblk, tile_blk, tile_blk] + c_out,
        out_shape=[jax.ShapeDtypeStruct((bsz, seq, D_MODEL), F32), tiles, tiles] + c_shapes,
        scratch_shapes=[pltpu.VMEM((tm, D_MODEL), BF16), pltpu.VMEM((MEM_LEN, D_MODEL), BF16),
                        pltpu.VMEM((MEM_LEN, D_MODEL), BF16)],
        input_output_aliases=aliases,
        compiler_params=_params("arbitrary", "arbitrary"),
        name="cross_prompt",
    )(*args, *c_args)


def _bdot(a, b, contract_b):
    return lax.dot_general(a, b, (((2,), (contract_b,)), ((0,), (0,))), preferred_element_type=F32)


def _mix_sample_kernel(sink_ref, x_ref, st_ref, kc_ref, vc_ref, win_ref, cw_ref, wout_ref, wq_ref, g_ref, b_ref,
                       *rest, bb, layer, ln):
    o_ref, nst_ref, nk_ref, nv_ref, q8_ref, zbuf, qm_scr, kn_scr, vn_scr, o_scr = rest[-10:]
    x = x_ref[...]
    xb = x.astype(BF16)

    def proj(col, width):
        return _dot(xb, win_ref[0, :, col:col + width])

    u = proj(_C_CG, D_CONV) * proj(_C_HC, D_CONV)
    st0 = st_ref[:, 0:D_CONV]
    st1 = st_ref[:, D_CONV:2 * D_CONV]
    conv = cw_ref[layer, 0:1, :] * st0 + cw_ref[layer, 1:2, :] * st1 + cw_ref[layer, 2:3, :] * u
    zbuf[:, 0:D_CONV] = (proj(_C_BG, D_CONV) * conv).astype(BF16)
    nst_ref[:, 0:D_CONV] = st1
    nst_ref[:, D_CONV:2 * D_CONV] = u

    q = proj(_C_Q, D_ATTN) * (HEAD_DIM ** -0.5)
    k = proj(_C_KV, D_KV)
    v = proj(_C_KV + D_KV, D_KV)

    pad = jnp.zeros((LANES - bb, D_KV), F32)
    kt = jnp.concatenate([k, pad], axis=0).T if bb < LANES else k.T
    vt = jnp.concatenate([v, pad], axis=0).T if bb < LANES else v.T
    last = lax.broadcasted_iota(jnp.int32, (D_KV, WINDOW), 1) == WINDOW - 1
    slab = layer if nk_ref.shape[0] > 1 else 0
    for other in range(nk_ref.shape[0]):
        if other != slab:
            nk_ref[other] = jnp.zeros(nk_ref.shape[1:], F32)
            nv_ref[other] = jnp.zeros(nv_ref.shape[1:], F32)
    for r in range(bb):
        nk_ref[slab, r] = jnp.where(last, kt[:, r:r + 1], pltpu.roll(kc_ref[0, r], WINDOW - 1, 1))
        nv_ref[slab, r] = jnp.where(last, vt[:, r:r + 1], pltpu.roll(vc_ref[0, r], WINDOW - 1, 1))

    lane = lax.broadcasted_iota(jnp.int32, (bb, LANES), 1)
    low = lane < HEAD_DIM
    for h in range(N_HEADS):
        kvh = h // GROUP
        qt = q[:, (h // 2) * LANES:(h // 2 + 1) * LANES]
        if (h % 2) != kvh:
            qt = pltpu.roll(qt, HEAD_DIM, 1)
        keep = low if kvh == 0 else jnp.logical_not(low)
        qm_scr[:, h, :] = jnp.where(keep, qt, 0.0)
        kn_scr[:, h, :] = k
        vn_scr[:, h, :] = v
    qm = qm_scr[...]

    s = _bdot(qm.astype(BF16), kc_ref[0].astype(BF16), 1)
    s_new = jnp.sum(qm * kn_scr[...], axis=-1, keepdims=True)
    hrow = lax.broadcasted_iota(jnp.int32, (1, N_HEADS, 1), 1)
    sink = jnp.full((1, N_HEADS, 1), sink_ref[layer, N_HEADS - 1], F32)
    for h in range(N_HEADS - 1):
        sink = jnp.where(hrow == h, sink_ref[layer, h], sink)
    m = jnp.maximum(jnp.maximum(jnp.max(s, axis=-1, keepdims=True), s_new), sink)
    p = jnp.exp(s - m)
    p_new = jnp.exp(s_new - m)
    denom = jnp.sum(p, axis=-1, keepdims=True) + p_new + jnp.exp(sink - m)
    o_scr[...] = (_bdot(p.astype(BF16), vc_ref[0].astype(BF16), 2) + p_new * vn_scr[...]) / denom

    for t in range(D_ATTN // LANES):
        oa = o_scr[:, 2 * t, :]
        ob = o_scr[:, 2 * t + 1, :]
        if (2 * t) // GROUP == 0:
            z = jnp.where(low, oa, pltpu.roll(ob, HEAD_DIM, 1))
        else:
            z = jnp.where(low, pltpu.roll(oa, HEAD_DIM, 1), ob)
        zbuf[:, D_CONV + t * LANES:D_CONV + (t + 1) * LANES] = z.astype(BF16)

    y = _dot(zbuf[...], wout_ref[0])
    out = _layer_norm(ALPHA * x + y, g_ref[layer, ln:ln + 1, :], b_ref[layer, ln:ln + 1, :])
    o_ref[...] = out
    q = _dot(out.astype(BF16), wq_ref[0]) * (MEM_HEAD_DIM ** -0.5)
    for row, col in _head_piece_cols():
        q8_ref[:, row, :] = q[:, col:col + LANES]


def _mix_sample(x, sinks, st, kc, vc, prev_windows, win, cw, wout, wq, g, b, layer, ln, bb):
    nb = x.shape[0]
    kern = functools.partial(_mix_sample_kernel, bb=bb, layer=layer, ln=ln)
    row = lambda i, s: (i, 0)
    win_blk = pl.BlockSpec((1, bb, D_KV, WINDOW), lambda i, s: (layer, i, 0, 0))
    in_specs = [
        pl.BlockSpec((bb, D_MODEL), row),
        pl.BlockSpec((bb, (CONV_W - 1) * D_CONV), row),
        win_blk,
        win_blk,
        _weight((D_MODEL, IN_COLS)),
        _resident((DEPTH, CONV_W, D_CONV)),
        _weight((D_CONV + D_ATTN, D_MODEL)),
        _weight((D_MODEL, D_MODEL)),
    ] + _ln_specs()
    args = [sinks, x, st, kc, vc, win, cw, wout, wq, g, b]
    aliases = {}
    out_win_blk = win_blk
    if prev_windows is None:
        out_win_blk = pl.BlockSpec((DEPTH, bb, D_KV, WINDOW), lambda i, s: (0, i, 0, 0))
    else:
        n_in = len(args)
        in_specs += [pl.BlockSpec(memory_space=pl.ANY)] * 2
        args += list(prev_windows)
        aliases = {n_in: 2, n_in + 1: 3}
    grid_spec = pltpu.PrefetchScalarGridSpec(
        num_scalar_prefetch=1,
        grid=(nb // bb,),
        in_specs=in_specs,
        out_specs=[pl.BlockSpec((bb, D_MODEL), row), pl.BlockSpec((bb, (CONV_W - 1) * D_CONV), row),
                   out_win_blk, out_win_blk, pl.BlockSpec((bb, _MEM_ROWS, LANES), lambda i, s: (i, 0, 0))],
        scratch_shapes=[pltpu.VMEM((bb, D_CONV + D_ATTN), BF16)]
        + [pltpu.VMEM((bb, N_HEADS, LANES), F32)] * 4,
    )
    return pl.pallas_call(
        kern,
        grid_spec=grid_spec,
        out_shape=[
            jax.ShapeDtypeStruct((nb, D_MODEL), F32),
            jax.ShapeDtypeStruct((nb, (CONV_W - 1) * D_CONV), F32),
            jax.ShapeDtypeStruct((DEPTH, nb, D_KV, WINDOW), F32),
            jax.ShapeDtypeStruct((DEPTH, nb, D_KV, WINDOW), F32),
            jax.ShapeDtypeStruct((nb, _MEM_ROWS, LANES), F32),
        ],
        input_output_aliases=aliases,
        compiler_params=_params("arbitrary"),
        name="mix_sample",
    )(*args)


_MEM_HALVES = MEM_HEAD_DIM // LANES
_MEM_ROWS = _MEM_HALVES * MEM_HEADS


def _head_piece_cols():
    return [(half * MEM_HEADS + h, h * MEM_HEAD_DIM + half * LANES)
            for half in range(_MEM_HALVES) for h in range(MEM_HEADS)]


SC_LANES = 16
SC_TOKENS = 16
_SC_ROWS = SC_TOKENS * _MEM_ROWS
_SC_CHUNKS = MEM_LEN // SC_TOKENS
_SC_PIECES = LANES // SC_LANES


def _sc_cross_attention(q8, mk_tiles, mv_tiles, layer, seq0, nseq):
    total_seq = mk_tiles.shape[1]
    kflat = mk_tiles.reshape(-1, LANES)
    vflat = mv_tiles.reshape(-1, LANES)
    chunk0 = (layer * total_seq + seq0) * _SC_CHUNKS
    mesh = plsc.VectorSubcoreMesh(core_axis_name="c", subcore_axis_name="s")

    @pl.kernel(out_type=jax.ShapeDtypeStruct((nseq * _MEM_ROWS, LANES), F32), mesh=mesh,
               scratch_types=[pltpu.VMEM((MEM_HEADS, SC_LANES), F32), pltpu.VMEM((MEM_HEADS, SC_LANES), F32)],
               compiler_params=pltpu.CompilerParams(needs_layout_passes=False))
    def attend(q_hbm, k_hbm, v_hbm, o_hbm, m_ref, l_ref):
        def body(idx, q_vmem, k_vmem, v_vmem, o_vmem):
            chunk = idx[1]
            lane = lax.iota(jnp.int32, SC_LANES)
            first = jnp.full((SC_LANES,), chunk, jnp.int32) == 0

            @pl.when(chunk == 0)
            def _():
                for h in range(MEM_HEADS):
                    m_ref[h, :] = jnp.zeros((SC_LANES,), F32)
                    l_ref[h, :] = jnp.zeros((SC_LANES,), F32)
                for r in range(_MEM_ROWS):
                    for j in range(_SC_PIECES):
                        o_vmem[r, pl.ds(j * SC_LANES, SC_LANES)] = jnp.zeros((SC_LANES,), F32)

            def one_head(h, carry):
                acc = [jnp.zeros((SC_LANES,), F32) for _ in range(SC_TOKENS)]
                for half in range(_MEM_HALVES):
                    r = half * MEM_HEADS + h
                    for j in range(_SC_PIECES):
                        sl = pl.ds(j * SC_LANES, SC_LANES)
                        qv = q_vmem[r, sl]
                        for t in range(SC_TOKENS):
                            acc[t] = acc[t] + k_vmem[t * _MEM_ROWS + r, sl] * qv
                s = jnp.zeros((SC_LANES,), F32)
                for t in range(SC_TOKENS):
                    s = jnp.where(lane == t, jnp.sum(acc[t]), s)

                m_old = m_ref[h, :]
                c_max = jnp.full((SC_LANES,), jnp.max(s), F32)
                m_new = jnp.where(first, c_max, jnp.maximum(m_old, c_max))
                alpha = jnp.where(first, 0.0, jnp.exp(m_old - m_new))
                p = jnp.exp(s - m_new)
                l_ref[h, :] = l_ref[h, :] * alpha + jnp.sum(p)
                m_ref[h, :] = m_new

                pt = [jnp.full((SC_LANES,), p[t], F32) for t in range(SC_TOKENS)]
                for half in range(_MEM_HALVES):
                    r = half * MEM_HEADS + h
                    for j in range(_SC_PIECES):
                        sl = pl.ds(j * SC_LANES, SC_LANES)
                        o = o_vmem[r, sl] * alpha
                        for t in range(SC_TOKENS):
                            o = o + pt[t] * v_vmem[t * _MEM_ROWS + r, sl]
                        o_vmem[r, sl] = o
                return carry

            lax.fori_loop(0, MEM_HEADS, one_head, 0)

            @pl.when(chunk == _SC_CHUNKS - 1)
            def _():
                for h in range(MEM_HEADS):
                    inv = 1.0 / l_ref[h, :]
                    for half in range(_MEM_HALVES):
                        r = half * MEM_HEADS + h
                        for j in range(_SC_PIECES):
                            sl = pl.ds(j * SC_LANES, SC_LANES)
                            o_vmem[r, sl] = o_vmem[r, sl] * inv

        pltpu.emit_pipeline(
            body,
            grid=(nseq, _SC_CHUNKS),
            in_specs=[pl.BlockSpec((_MEM_ROWS, LANES), lambda s, c: (seq0 + s, 0)),
                      pl.BlockSpec((_SC_ROWS, LANES), lambda s, c: (chunk0 + s * _SC_CHUNKS + c, 0)),
                      pl.BlockSpec((_SC_ROWS, LANES), lambda s, c: (chunk0 + s * _SC_CHUNKS + c, 0))],
            out_specs=[pl.BlockSpec((_MEM_ROWS, LANES), lambda s, c: (s, 0))],
            core_axis_name=("c", "s"),
            dimension_semantics=(pltpu.PARALLEL, pltpu.ARBITRARY),
            _explicit_indices=True,
        )(q_hbm, k_hbm, v_hbm, o_hbm)

    return attend(q8, kflat, vflat)


TM_PROMPT = 1024
FIRST_TILES = 2
BB_MIX_SAMPLE = 32


def kernel(x_prompt, x_sample, mem_prompt, cache_win_k, cache_win_v, state_conv, cache_mem_k, cache_mem_v,
           ln_g, ln_b, ffn1_w_gu, ffn1_w_down, w_in, conv_w, attn_sinks, w_out,
           w_cq, w_mk, w_mv, w_co, ffn2_w_gu, ffn2_w_down):
    bsz, seq, _ = x_prompt.shape
    nsmp = x_sample.shape[0]
    yp = x_prompt
    ys = x_sample.reshape(nsmp, D_MODEL)

    stacked = dict(ffn1=(ffn1_w_gu, ffn1_w_down), ffn2=(ffn2_w_gu, ffn2_w_down),
                   mix=(w_in, w_out), cross=(w_cq, w_co, w_mk, w_mv))
    first_tile, wg0, wu0, wd0 = _ffn_first_tile(x_prompt.reshape(bsz * seq, D_MODEL), ffn1_w_gu, ffn1_w_down,
                                                ln_g, ln_b, 0, 0, FIRST_TILES * TM_PROMPT)
    wb = {("ffn1", 0): ((wg0, wu0), wd0)}

    def jobs(*groups):
        return [(w, layer) for name, layer in groups for w in stacked[name]]

    def keep(groups, casts):
        casts = list(casts)
        for name, layer in groups:
            wb[(name, layer)] = tuple(casts[:len(stacked[name])])
            del casts[:len(stacked[name])]

    def token_tiles(c):
        c = c.reshape(DEPTH, nsmp, MEM_LEN, MEM_HEADS, _MEM_HALVES, LANES)
        return c.transpose(0, 1, 2, 4, 3, 5).reshape(DEPTH, nsmp, MEM_LEN * _MEM_ROWS, LANES)

    mem_k_tiles, mem_v_tiles = token_tiles(cache_mem_k), token_tiles(cache_mem_v)

    def window_lanes(c):
        return c.transpose(0, 1, 3, 4, 2).reshape(DEPTH, nsmp, D_KV, WINDOW)

    win_k_lanes, win_v_lanes = window_lanes(cache_win_k), window_lanes(cache_win_v)
    new_windows = None
    mem_tiles = None

    wkp, wvp, cvp, cvs = [], [], [], []
    for l in range(DEPTH):
        more = l + 1 < DEPTH
        side = [("mix", 0), ("cross", 0)] if l == 0 else [("ffn2", l)]
        yp, ys, *casts = _ffn(yp.reshape(bsz * seq, D_MODEL), ys, *wb[("ffn1", l)], ln_g, ln_b, l, 0, TM_PROMPT,
                              jobs(*side), first_tile=first_tile if l == 0 else None)
        keep(side, casts)
        yp = yp.reshape(bsz, seq, D_MODEL)

        win, wout = wb[("mix", l)]
        side = [("ffn2", 0)] if l == 0 else []
        yp, nk, nv, ncv, *casts = _mix_prompt(yp, attn_sinks, win, conv_w, wout, ln_g, ln_b, l, 1, TM_PROMPT,
                                              jobs(*side))
        keep(side, casts)
        wkp.append(nk.reshape(bsz, WINDOW, N_KV_HEADS, HEAD_DIM))
        wvp.append(nv.reshape(bsz, WINDOW, N_KV_HEADS, HEAD_DIM))
        cvp.append(ncv)
        wcq, wco, wmk, wmv = wb[("cross", l)]
        ys, nst, nks, nvs, q8 = _mix_sample(
            ys, attn_sinks, state_conv[l].reshape(nsmp, (CONV_W - 1) * D_CONV),
            win_k_lanes, win_v_lanes, new_windows, win, conv_w, wout, wcq, ln_g, ln_b, l, 1, BB_MIX_SAMPLE)
        new_windows = (nks, nvs)
        cvs.append(nst.reshape(nsmp, CONV_W - 1, D_CONV))

        o8 = _sc_cross_attention(q8.reshape(nsmp * _MEM_ROWS, LANES), mem_k_tiles, mem_v_tiles, l, 0, nsmp)
        side = [("ffn1", l + 1)] if more else []
        yp, mk_all, mv_all, *casts = _cross_prompt(yp, mem_prompt, wcq, wco, wmk, wmv, ln_g, ln_b, l, 2, TM_PROMPT,
                                                   mem_tiles, jobs(*side))
        mem_tiles = (mk_all, mv_all)
        keep(side, casts)

        side = [("mix", l + 1), ("cross", l + 1)] if more else []
        yp, ys, *casts = _ffn(yp.reshape(bsz * seq, D_MODEL), ys, *wb[("ffn2", l)], ln_g, ln_b, l, 3, TM_PROMPT,
                              jobs(*side), sample_cross=(o8.reshape(nsmp, _MEM_ROWS, LANES), wco, 2))
        keep(side, casts)
        yp = yp.reshape(bsz, seq, D_MODEL)

    def window_rows(c):
        return c.reshape(DEPTH, nsmp, N_KV_HEADS, HEAD_DIM, WINDOW).transpose(0, 1, 4, 2, 3)

    def token_rows(c):
        c = c.reshape(DEPTH, bsz, MEM_LEN, _MEM_HALVES, MEM_HEADS, LANES)
        return c.transpose(0, 1, 2, 4, 3, 5).reshape(DEPTH, bsz, MEM_LEN, MEM_HEADS, MEM_HEAD_DIM)

    return (yp, ys.reshape(nsmp, 1, D_MODEL),
            jnp.stack(wkp), jnp.stack(wvp), jnp.stack(cvp), token_rows(mem_tiles[0]), token_rows(mem_tiles[1]),
            window_rows(new_windows[0]), window_rows(new_windows[1]), jnp.stack(cvs))
```

```python
import functools

import jax
import jax.numpy as jnp
from jax import lax
from jax.experimental import pallas as pl
from jax.experimental.pallas import tpu as pltpu
from jax.experimental.pallas import tpu_sc as plsc

D_MODEL = 1024
DEPTH = 2
D_CONV = 512
CONV_W = 3
HEAD_DIM = 64
N_HEADS = 8
N_KV_HEADS = 2
GROUP = N_HEADS // N_KV_HEADS
WINDOW = 128
D_ATTN = N_HEADS * HEAD_DIM
D_KV = N_KV_HEADS * HEAD_DIM
IN_COLS = 3 * D_CONV + D_ATTN + 2 * D_KV
MEM_LEN = 256
MEM_HEADS = 4
MEM_HEAD_DIM = D_MODEL // MEM_HEADS
D_FF = 2816
N_LN = 4
ALPHA = (2.0 * DEPTH) ** 0.25
LN_EPS = 1e-5

_C_BG, _C_CG, _C_HC, _C_Q, _C_KV = 0, D_CONV, 2 * D_CONV, 3 * D_CONV, 3 * D_CONV + D_ATTN

LANES = 128
SUBLANES = 8
MXU_COLS = 256
VMEM_LIMIT_BYTES = 56 * 1024 * 1024

BF16 = jnp.bfloat16
F32 = jnp.float32


def _dot(a, b):
    return jnp.dot(a, b, preferred_element_type=F32)


def _dot_nt(a, b):
    return lax.dot_general(a, b, (((1,), (1,)), ((), ())), preferred_element_type=F32)


def _layer_norm(z, g, b):
    mu = jnp.mean(z, axis=-1, keepdims=True)
    d = z - mu
    var = jnp.mean(d * d, axis=-1, keepdims=True)
    return d * lax.rsqrt(var + LN_EPS) * g + b


def _resident(shape):
    zeros = (0,) * len(shape)
    return pl.BlockSpec(shape, lambda *_: zeros, pipeline_mode=pl.Buffered(1))


def _weight(shape):
    return _resident((1,) + shape)


def _cast_specs(jobs, n_steps, step_of):
    in_specs, out_specs, out_shapes, args = [], [], [], []
    for w, layer in jobs:
        _, k, n = w.shape
        rows = k // n_steps
        assert rows * n_steps == k and rows % (2 * SUBLANES) == 0, (w.shape, n_steps)
        in_specs.append(pl.BlockSpec((1, rows, n), lambda *idx, layer=layer: (layer, step_of(*idx), 0)))
        out_specs.append(pl.BlockSpec((1, rows, n), lambda *idx: (0, step_of(*idx), 0)))
        out_shapes.append(jax.ShapeDtypeStruct((1, k, n), BF16))
        args.append(w)
    return in_specs, out_specs, out_shapes, args


def _run_casts(src_refs, dst_refs):
    for src, dst in zip(src_refs, dst_refs, strict=True):
        dst[...] = src[...].astype(BF16)


def _ln_specs():
    return [_resident((DEPTH, N_LN, D_MODEL)), _resident((DEPTH, N_LN, D_MODEL))]


def _params(*sem):
    return pltpu.CompilerParams(dimension_semantics=sem, vmem_limit_bytes=VMEM_LIMIT_BYTES)


NORM_ROWS = 256


def _project_and_norm(lhs_ref, w, x_rows, scale, g, b, write_rows, rows):
    for r0 in range(0, rows, min(NORM_ROWS, rows)):
        r1 = r0 + min(NORM_ROWS, rows)
        y = _dot(lhs_ref[r0:r1, :], w)
        write_rows(r0, r1, _layer_norm(ALPHA * x_rows(r0, r1) + scale * y, g, b))


def _swiglu(gate, up):
    return (gate * jax.nn.sigmoid(gate) * up).astype(BF16)


def _ffn_rows(x_ref, o_ref, wg_ref, wu_ref, wd_ref, h_ref, g, b):
    rows = x_ref.shape[0]
    xb = x_ref[...].astype(BF16)
    for c in range(D_FF // MXU_COLS):
        lo = c * MXU_COLS
        h_ref[0:rows, lo:lo + MXU_COLS] = _swiglu(_dot(xb, wg_ref[0, :, lo:lo + MXU_COLS]),
                                                  _dot(xb, wu_ref[0, :, lo:lo + MXU_COLS]))

    def write_rows(r0, r1, y):
        o_ref[r0:r1, :] = y

    _project_and_norm(h_ref, wd_ref[0], lambda r0, r1: x_ref[r0:r1, :], 0.5, g, b, write_rows, rows)


def _ffn_kernel(x_ref, xs_ref, wg_ref, wu_ref, wd_ref, g_ref, b_ref, *rest,
                layer, ln, n_cast, n_tiles, cross_ln, has_first):
    n_pre = (0 if cross_ln is None else 2) + (1 if has_first else 0)
    pre, rest = rest[:n_pre], rest[n_pre:]
    cast_src, (o_ref, os_ref), rest = rest[:n_cast], rest[n_cast:n_cast + 2], rest[n_cast + 2:]
    cast_dst, scratch = rest[:n_cast], rest[n_cast:]
    h_ref = scratch[0]
    g = g_ref[layer, ln:ln + 1, :]
    b = b_ref[layer, ln:ln + 1, :]
    step = pl.program_id(0)

    if has_first:
        @pl.when(step == 0)
        def _():
            _run_casts(cast_src, cast_dst)
            o_ref[...] = pre[-1][...]

    @pl.when((step < n_tiles) & (step > 0) if has_first else step < n_tiles)
    def _():
        _run_casts(cast_src, cast_dst)
        _ffn_rows(x_ref, o_ref, wg_ref, wu_ref, wd_ref, h_ref, g, b)

    @pl.when(step == n_tiles)
    def _():
        if cross_ln is None:
            _ffn_rows(xs_ref, os_ref, wg_ref, wu_ref, wd_ref, h_ref, g, b)
        else:
            o8_ref, wo_ref = pre[:2]
            obuf, xs_scr = scratch[1:]
            for row, col in _head_piece_cols():
                obuf[:, col:col + LANES] = o8_ref[:, row, :].astype(BF16)
            xs_scr[...] = _layer_norm(ALPHA * xs_ref[...] + _dot(obuf[...], wo_ref[0]),
                                      g_ref[layer, cross_ln:cross_ln + 1, :], b_ref[layer, cross_ln:cross_ln + 1, :])
            _ffn_rows(xs_scr, os_ref, wg_ref, wu_ref, wd_ref, h_ref, g, b)


def _ffn(x, xs, wgu, wd, g, b, layer, ln, tm, casts=(), sample_cross=None, first_tile=None):
    m = x.shape[0]
    ns = xs.shape[0]
    n_tiles = m // tm
    tile = lambda i: (jnp.minimum(i, n_tiles - 1), 0)
    c_in, c_out, c_shapes, c_args = _cast_specs(casts, n_tiles, lambda i: jnp.minimum(i, n_tiles - 1))
    half = lambda j: pl.BlockSpec((1, D_MODEL, D_FF), lambda i: (0, 0, j), pipeline_mode=pl.Buffered(1))
    if isinstance(wgu, tuple):
        w_args, w_specs = list(wgu), [half(0), half(0)]
    else:
        w_args, w_specs = [wgu, wgu], [half(0), half(1)]
    pre_specs, pre_args, pre_scratch, cross_ln = [], [], [], None
    if sample_cross is not None:
        o8, wo, cross_ln = sample_cross
        pre_specs = [_resident((ns, _MEM_ROWS, LANES)), _weight((D_MODEL, D_MODEL))]
        pre_args = [o8, wo]
        pre_scratch = [pltpu.VMEM((ns, D_MODEL), BF16), pltpu.VMEM((ns, D_MODEL), F32)]
    if first_tile is not None:
        pre_specs.append(_resident((tm, D_MODEL)))
        pre_args.append(first_tile)
    return pl.pallas_call(
        functools.partial(_ffn_kernel, layer=layer, ln=ln, n_cast=len(casts), n_tiles=n_tiles, cross_ln=cross_ln,
                          has_first=first_tile is not None),
        grid=(n_tiles + 1,),
        in_specs=[pl.BlockSpec((tm, D_MODEL), tile), _resident((ns, D_MODEL))] + w_specs
        + [_weight((D_FF, D_MODEL))] + _ln_specs() + pre_specs + c_in,
        out_specs=[pl.BlockSpec((tm, D_MODEL), tile), pl.BlockSpec((ns, D_MODEL), lambda i: (0, 0))] + c_out,
        out_shape=[jax.ShapeDtypeStruct((m, D_MODEL), F32), jax.ShapeDtypeStruct((ns, D_MODEL), F32)] + c_shapes,
        scratch_shapes=[pltpu.VMEM((tm, D_FF), BF16)] + pre_scratch,
        compiler_params=_params("arbitrary"),
        name="ffn_ln",
    )(x, xs, *w_args, wd, g, b, *pre_args, *c_args)


def _ffn_first_tile_kernel(x_ref, wg_ref, wu_ref, wd_ref, g_ref, b_ref, o_ref, wgb_ref, wub_ref, wdb_ref,
                           xb_scr, h_scr, wd_scr, *, layer, ln):
    c = pl.program_id(0)
    n_chunks = h_scr.shape[0]

    @pl.when(c == 0)
    def _():
        xb_scr[...] = x_ref[...].astype(BF16)

    wg = wg_ref[0].astype(BF16)
    wu = wu_ref[0].astype(BF16)
    wd = wd_ref[0].astype(BF16)
    wgb_ref[0] = wg
    wub_ref[0] = wu
    wdb_ref[0] = wd
    wd_scr[c] = wd
    xb = xb_scr[...]
    h_scr[c] = _swiglu(_dot(xb, wg), _dot(xb, wu))

    @pl.when(c == n_chunks - 1)
    def _():
        rows = x_ref.shape[0]
        g = g_ref[layer, ln:ln + 1, :]
        b = b_ref[layer, ln:ln + 1, :]
        for r0 in range(0, rows, NORM_ROWS):
            r1 = r0 + NORM_ROWS
            y = _dot(h_scr[0, r0:r1, :], wd_scr[0])
            for k in range(1, n_chunks):
                y = y + _dot(h_scr[k, r0:r1, :], wd_scr[k])
            o_ref[r0:r1, :] = _layer_norm(ALPHA * x_ref[r0:r1, :] + 0.5 * y, g, b)


def _ffn_first_tile(x, wgu32, wd32, g, b, layer, ln, tm):
    n_chunks = D_FF // MXU_COLS
    return pl.pallas_call(
        functools.partial(_ffn_first_tile_kernel, layer=layer, ln=ln),
        grid=(n_chunks,),
        in_specs=[_resident((tm, D_MODEL)),
                  pl.BlockSpec((1, D_MODEL, MXU_COLS), lambda c: (layer, 0, c)),
                  pl.BlockSpec((1, D_MODEL, MXU_COLS), lambda c: (layer, 0, n_chunks + c)),
                  pl.BlockSpec((1, MXU_COLS, D_MODEL), lambda c: (layer, c, 0))] + _ln_specs(),
        out_specs=[pl.BlockSpec((tm, D_MODEL), lambda c: (0, 0)),
                   pl.BlockSpec((1, D_MODEL, MXU_COLS), lambda c: (0, 0, c)),
                   pl.BlockSpec((1, D_MODEL, MXU_COLS), lambda c: (0, 0, c)),
                   pl.BlockSpec((1, MXU_COLS, D_MODEL), lambda c: (0, c, 0))],
        out_shape=[jax.ShapeDtypeStruct((tm, D_MODEL), F32),
                   jax.ShapeDtypeStruct((1, D_MODEL, D_FF), BF16),
                   jax.ShapeDtypeStruct((1, D_MODEL, D_FF), BF16),
                   jax.ShapeDtypeStruct((1, D_FF, D_MODEL), BF16)],
        scratch_shapes=[pltpu.VMEM((tm, D_MODEL), BF16),
                        pltpu.VMEM((n_chunks, tm, MXU_COLS), BF16),
                        pltpu.VMEM((n_chunks, MXU_COLS, D_MODEL), BF16)],
        compiler_params=_params("arbitrary"),
        name="ffn_first_tile",
    )(x, wgu32, wgu32, wd32, g, b)


def _dup_halves(a):
    lane = lax.broadcasted_iota(jnp.int32, a.shape, 1)
    low = lane < HEAD_DIM
    rolled = pltpu.roll(a, HEAD_DIM, 1)
    return jnp.where(low, a, rolled), jnp.where(low, rolled, a)


def _sink_column(sink_ref, layer, first_head, rows_per_head, n_heads):
    rows = n_heads * rows_per_head
    r = lax.broadcasted_iota(jnp.int32, (rows, 1), 0)
    col = jnp.full((rows, 1), sink_ref[layer, first_head + n_heads - 1], F32)
    for g in range(n_heads - 2, -1, -1):
        col = jnp.where(r < (g + 1) * rows_per_head, sink_ref[layer, first_head + g], col)
    return col


def _mix_prompt_kernel(sink_ref, x_ref, win_ref, cw_ref, wout_ref, g_ref, b_ref, *rest, tm, layer, ln, n_cast):
    cast_src, rest = rest[:n_cast], rest[n_cast:]
    (o_ref, nk_ref, nv_ref, nc_ref), rest = rest[:4], rest[4:]
    cast_dst, (ubuf, qbuf, kd0, kd1, vd0, vd1, zbuf) = rest[:n_cast], rest[n_cast:]
    seq_start = pl.program_id(1) == 0

    @pl.when(seq_start)
    def _():
        ubuf[0:SUBLANES, :] = jnp.zeros((SUBLANES, D_CONV), F32)
        for r in (kd0, kd1, vd0, vd1):
            r[0:WINDOW, :] = jnp.zeros((WINDOW, LANES), BF16)

    _run_casts(cast_src, cast_dst)
    xb = x_ref[0].astype(BF16)

    def proj(col, width):
        return _dot(xb, win_ref[0, :, col:col + width])

    u = proj(_C_CG, D_CONV) * proj(_C_HC, D_CONV)
    ubuf[SUBLANES:SUBLANES + tm, :] = u
    conv = (cw_ref[layer, 0:1, :] * ubuf[SUBLANES - 2:SUBLANES - 2 + tm, :]
            + cw_ref[layer, 1:2, :] * ubuf[SUBLANES - 1:SUBLANES - 1 + tm, :]
            + cw_ref[layer, 2:3, :] * u)
    zbuf[:, 0:D_CONV] = (proj(_C_BG, D_CONV) * conv).astype(BF16)
    nc_ref[0] = u[tm - (CONV_W - 1):tm, :]
    ubuf[0:SUBLANES, :] = u[tm - SUBLANES:tm, :]

    qbuf[...] = (proj(_C_Q, D_ATTN) * (HEAD_DIM ** -0.5)).astype(BF16)
    k = proj(_C_KV, D_KV)
    v = proj(_C_KV + D_KV, D_KV)
    nk_ref[0] = k[tm - WINDOW:tm, :]
    nv_ref[0] = v[tm - WINDOW:tm, :]
    ka, kb = _dup_halves(k)
    va, vb = _dup_halves(v)
    kd0[WINDOW:WINDOW + tm, :] = ka.astype(BF16)
    kd1[WINDOW:WINDOW + tm, :] = kb.astype(BF16)
    vd0[WINDOW:WINDOW + tm, :] = va.astype(BF16)
    vd1[WINDOW:WINDOW + tm, :] = vb.astype(BF16)

    rows = GROUP * WINDOW
    a_idx = lax.broadcasted_iota(jnp.int32, (rows, 2 * WINDOW), 0) % WINDOW
    c_idx = lax.broadcasted_iota(jnp.int32, (rows, 2 * WINDOW), 1)
    band = (c_idx >= a_idx) & (c_idx <= a_idx + WINDOW)
    first_lb = jnp.where(seq_start, WINDOW, 0)
    low = lax.broadcasted_iota(jnp.int32, (WINDOW, LANES), 1) < HEAD_DIM
    zero = jnp.zeros((WINDOW, LANES), BF16)

    for n in range(tm // WINDOW):
        r0 = n * WINDOW
        mask = (band & (c_idx >= first_lb)) if n == 0 else band
        for j, (kd, vd) in enumerate(((kd0, vd0), (kd1, vd1))):
            qa = qbuf[r0:r0 + WINDOW, (2 * j) * LANES:(2 * j + 1) * LANES]
            qb = qbuf[r0:r0 + WINDOW, (2 * j + 1) * LANES:(2 * j + 2) * LANES]
            qs = jnp.concatenate([jnp.where(low, qa, zero), jnp.where(low, zero, qa),
                                  jnp.where(low, qb, zero), jnp.where(low, zero, qb)], axis=0)
            s = _dot_nt(qs, kd[r0:r0 + 2 * WINDOW, :])
            s = jnp.where(mask, s, -jnp.inf)
            sink = _sink_column(sink_ref, layer, GROUP * j, WINDOW, GROUP)
            m = jnp.maximum(jnp.max(s, axis=-1, keepdims=True), sink)
            p = jnp.exp(s - m)
            denom = jnp.sum(p, axis=-1, keepdims=True) + jnp.exp(sink - m)
            o = _dot(p.astype(BF16), vd[r0:r0 + 2 * WINDOW, :]) / denom
            za = jnp.where(low, o[0:WINDOW], o[WINDOW:2 * WINDOW])
            zb = jnp.where(low, o[2 * WINDOW:3 * WINDOW], o[3 * WINDOW:4 * WINDOW])
            c0 = D_CONV + (2 * j) * LANES
            zbuf[r0:r0 + WINDOW, c0:c0 + LANES] = za.astype(BF16)
            zbuf[r0:r0 + WINDOW, c0 + LANES:c0 + 2 * LANES] = zb.astype(BF16)

    for r in (kd0, kd1, vd0, vd1):
        r[0:WINDOW, :] = r[tm:tm + WINDOW, :]

    def write_rows(r0, r1, y):
        o_ref[0, r0:r1, :] = y

    _project_and_norm(zbuf, wout_ref[0], lambda r0, r1: x_ref[0, r0:r1, :], 1.0,
                      g_ref[layer, ln:ln + 1, :], b_ref[layer, ln:ln + 1, :], write_rows, tm)


def _mix_prompt(x, sinks, win, cw, wout, g, b, layer, ln, tm, casts=()):
    bsz, seq, _ = x.shape
    tps = seq // tm
    kern = functools.partial(_mix_prompt_kernel, tm=tm, layer=layer, ln=ln, n_cast=len(casts))
    per_seq = lambda bi, i, s: (bi, 0, 0)
    c_in, c_out, c_shapes, c_args = _cast_specs(casts, bsz * tps, lambda bi, i, *_: bi * tps + i)
    grid_spec = pltpu.PrefetchScalarGridSpec(
        num_scalar_prefetch=1,
        grid=(bsz, tps),
        in_specs=[
            pl.BlockSpec((1, tm, D_MODEL), lambda bi, i, s: (bi, i, 0)),
            _weight((D_MODEL, IN_COLS)),
            _resident((DEPTH, CONV_W, D_CONV)),
            _weight((D_CONV + D_ATTN, D_MODEL)),
        ] + _ln_specs() + c_in,
        out_specs=[
            pl.BlockSpec((1, tm, D_MODEL), lambda bi, i, s: (bi, i, 0)),
            pl.BlockSpec((1, WINDOW, D_KV), per_seq),
            pl.BlockSpec((1, WINDOW, D_KV), per_seq),
            pl.BlockSpec((1, CONV_W - 1, D_CONV), per_seq),
        ] + c_out,
        scratch_shapes=[
            pltpu.VMEM((tm + SUBLANES, D_CONV), F32),
            pltpu.VMEM((tm, D_ATTN), BF16),
            pltpu.VMEM((tm + WINDOW, LANES), BF16),
            pltpu.VMEM((tm + WINDOW, LANES), BF16),
            pltpu.VMEM((tm + WINDOW, LANES), BF16),
            pltpu.VMEM((tm + WINDOW, LANES), BF16),
            pltpu.VMEM((tm, D_CONV + D_ATTN), BF16),
        ],
    )
    return pl.pallas_call(
        kern,
        grid_spec=grid_spec,
        out_shape=[
            jax.ShapeDtypeStruct((bsz, seq, D_MODEL), F32),
            jax.ShapeDtypeStruct((bsz, WINDOW, D_KV), F32),
            jax.ShapeDtypeStruct((bsz, WINDOW, D_KV), F32),
            jax.ShapeDtypeStruct((bsz, CONV_W - 1, D_CONV), F32),
        ] + c_shapes,
        compiler_params=_params("arbitrary", "arbitrary"),
        name="mix_prompt",
    )(sinks, x, win, cw, wout, g, b, *c_args)


def _memkv_kernel(m_ref, wk_ref, wv_ref, *rest, layer):
    k_ref, v_ref, kb_ref, vb_ref = rest[-4:]
    mb = m_ref[...].astype(BF16)
    k = _dot(mb, wk_ref[0])
    v = _dot(mb, wv_ref[0])
    kb_ref[...] = k.astype(BF16)
    vb_ref[...] = v.astype(BF16)
    slab = layer if k_ref.shape[0] > 1 else 0
    for other in range(k_ref.shape[0]):
        if other != slab:
            k_ref[other] = jnp.zeros(k_ref.shape[1:], F32)
            v_ref[other] = jnp.zeros(v_ref.shape[1:], F32)
    for row, col in _head_piece_cols():
        k_ref[slab, 0, :, row, :] = k[:, col:col + LANES]
        v_ref[slab, 0, :, row, :] = v[:, col:col + LANES]


def _memkv(mem, wk, wv, layer, prev):
    m = mem.shape[0]
    nb = m // MEM_LEN
    blk = pl.BlockSpec((MEM_LEN, D_MODEL), lambda i: (i, 0))
    in_specs = [blk, _weight((D_MODEL, D_MODEL)), _weight((D_MODEL, D_MODEL))]
    args = [mem, wk, wv]
    aliases = {}
    if prev is None:
        tile_blk = pl.BlockSpec((DEPTH, 1, MEM_LEN, _MEM_ROWS, LANES), lambda i: (0, i, 0, 0, 0))
    else:
        tile_blk = pl.BlockSpec((1, 1, MEM_LEN, _MEM_ROWS, LANES), lambda i: (layer, i, 0, 0, 0))
        in_specs += [pl.BlockSpec(memory_space=pl.ANY)] * 2
        args += list(prev)
        aliases = {3: 0, 4: 1}
    tiles = jax.ShapeDtypeStruct((DEPTH, nb, MEM_LEN, _MEM_ROWS, LANES), F32)
    return pl.pallas_call(
        functools.partial(_memkv_kernel, layer=layer),
        grid=(nb,),
        in_specs=in_specs,
        out_specs=[tile_blk, tile_blk, blk, blk],
        out_shape=[tiles, tiles, jax.ShapeDtypeStruct((m, D_MODEL), BF16), jax.ShapeDtypeStruct((m, D_MODEL), BF16)],
        input_output_aliases=aliases,
        compiler_params=_params("arbitrary"),
        name="mem_kv",
    )(*args)


def _cross_prompt_kernel(x_ref, mk_ref, mv_ref, wq_ref, wo_ref, g_ref, b_ref, *rest, layer, ln, n_cast):
    cast_src, o_ref, cast_dst, obuf = rest[:n_cast], rest[n_cast], rest[n_cast + 1:2 * n_cast + 1], rest[-1]
    _run_casts(cast_src, cast_dst)
    q = _dot(x_ref[0].astype(BF16), wq_ref[0]) * (MEM_HEAD_DIM ** -0.5)
    for h in range(MEM_HEADS):
        c0 = h * MEM_HEAD_DIM
        s = _dot_nt(q[:, c0:c0 + MEM_HEAD_DIM].astype(BF16), mk_ref[0, :, c0:c0 + MEM_HEAD_DIM])
        m = jnp.max(s, axis=-1, keepdims=True)
        p = jnp.exp(s - m)
        denom = jnp.sum(p, axis=-1, keepdims=True)
        o = _dot(p.astype(BF16), mv_ref[0, :, c0:c0 + MEM_HEAD_DIM]) / denom
        obuf[:, c0:c0 + MEM_HEAD_DIM] = o.astype(BF16)

    def write_rows(r0, r1, y):
        o_ref[0, r0:r1, :] = y

    _project_and_norm(obuf, wo_ref[0], lambda r0, r1: x_ref[0, r0:r1, :], 1.0,
                      g_ref[layer, ln:ln + 1, :], b_ref[layer, ln:ln + 1, :], write_rows, x_ref.shape[1])


def _cross_prompt(x, mk, mv, wq, wo, g, b, layer, ln, tm, casts=()):
    bsz, seq, _ = x.shape
    tps = seq // tm
    xblk = pl.BlockSpec((1, tm, D_MODEL), lambda bi, i: (bi, i, 0))
    mblk = pl.BlockSpec((1, MEM_LEN, D_MODEL), lambda bi, i: (bi, 0, 0))
    c_in, c_out, c_shapes, c_args = _cast_specs(casts, bsz * tps, lambda bi, i: bi * tps + i)
    return pl.pallas_call(
        functools.partial(_cross_prompt_kernel, layer=layer, ln=ln, n_cast=len(casts)),
        grid=(bsz, tps),
        in_specs=[xblk, mblk, mblk, _weight((D_MODEL, D_MODEL)), _weight((D_MODEL, D_MODEL))] + _ln_specs() + c_in,
        out_specs=[xblk] + c_out,
        out_shape=[jax.ShapeDtypeStruct((bsz, seq, D_MODEL), F32)] + c_shapes,
        scratch_shapes=[pltpu.VMEM((tm, D_MODEL), BF16)],
        compiler_params=_params("arbitrary", "arbitrary"),
        name="cross_prompt",
    )(x, mk, mv, wq, wo, g, b, *c_args)


def _bdot(a, b, contract_b):
    return lax.dot_general(a, b, (((2,), (contract_b,)), ((0,), (0,))), preferred_element_type=F32)


def _mix_sample_kernel(sink_ref, x_ref, st_ref, kc_ref, vc_ref, win_ref, cw_ref, wout_ref, wq_ref, g_ref, b_ref,
                       *rest, bb, layer, ln):
    o_ref, nst_ref, nk_ref, nv_ref, q8_ref, zbuf, qm_scr, kn_scr, vn_scr, o_scr = rest[-10:]
    x = x_ref[...]
    xb = x.astype(BF16)

    def proj(col, width):
        return _dot(xb, win_ref[0, :, col:col + width])

    u = proj(_C_CG, D_CONV) * proj(_C_HC, D_CONV)
    st0 = st_ref[:, 0:D_CONV]
    st1 = st_ref[:, D_CONV:2 * D_CONV]
    conv = cw_ref[layer, 0:1, :] * st0 + cw_ref[layer, 1:2, :] * st1 + cw_ref[layer, 2:3, :] * u
    zbuf[:, 0:D_CONV] = (proj(_C_BG, D_CONV) * conv).astype(BF16)
    nst_ref[:, 0:D_CONV] = st1
    nst_ref[:, D_CONV:2 * D_CONV] = u

    q = proj(_C_Q, D_ATTN) * (HEAD_DIM ** -0.5)
    k = proj(_C_KV, D_KV)
    v = proj(_C_KV + D_KV, D_KV)

    pad = jnp.zeros((LANES - bb, D_KV), F32)
    kt = jnp.concatenate([k, pad], axis=0).T if bb < LANES else k.T
    vt = jnp.concatenate([v, pad], axis=0).T if bb < LANES else v.T
    last = lax.broadcasted_iota(jnp.int32, (D_KV, WINDOW), 1) == WINDOW - 1
    slab = layer if nk_ref.shape[0] > 1 else 0
    for other in range(nk_ref.shape[0]):
        if other != slab:
            nk_ref[other] = jnp.zeros(nk_ref.shape[1:], F32)
            nv_ref[other] = jnp.zeros(nv_ref.shape[1:], F32)
    for r in range(bb):
        nk_ref[slab, r] = jnp.where(last, kt[:, r:r + 1], pltpu.roll(kc_ref[0, r], WINDOW - 1, 1))
        nv_ref[slab, r] = jnp.where(last, vt[:, r:r + 1], pltpu.roll(vc_ref[0, r], WINDOW - 1, 1))

    lane = lax.broadcasted_iota(jnp.int32, (bb, LANES), 1)
    low = lane < HEAD_DIM
    for h in range(N_HEADS):
        kvh = h // GROUP
        qt = q[:, (h // 2) * LANES:(h // 2 + 1) * LANES]
        if (h % 2) != kvh:
            qt = pltpu.roll(qt, HEAD_DIM, 1)
        keep = low if kvh == 0 else jnp.logical_not(low)
        qm_scr[:, h, :] = jnp.where(keep, qt, 0.0)
        kn_scr[:, h, :] = k
        vn_scr[:, h, :] = v
    qm = qm_scr[...]

    s = _bdot(qm.astype(BF16), kc_ref[0].astype(BF16), 1)
    s_new = jnp.sum(qm * kn_scr[...], axis=-1, keepdims=True)
    hrow = lax.broadcasted_iota(jnp.int32, (1, N_HEADS, 1), 1)
    sink = jnp.full((1, N_HEADS, 1), sink_ref[layer, N_HEADS - 1], F32)
    for h in range(N_HEADS - 1):
        sink = jnp.where(hrow == h, sink_ref[layer, h], sink)
    m = jnp.maximum(jnp.maximum(jnp.max(s, axis=-1, keepdims=True), s_new), sink)
    p = jnp.exp(s - m)
    p_new = jnp.exp(s_new - m)
    denom = jnp.sum(p, axis=-1, keepdims=True) + p_new + jnp.exp(sink - m)
    o_scr[...] = (_bdot(p.astype(BF16), vc_ref[0].astype(BF16), 2) + p_new * vn_scr[...]) / denom

    for t in range(D_ATTN // LANES):
        oa = o_scr[:, 2 * t, :]
        ob = o_scr[:, 2 * t + 1, :]
        if (2 * t) // GROUP == 0:
            z = jnp.where(low, oa, pltpu.roll(ob, HEAD_DIM, 1))
        else:
            z = jnp.where(low, pltpu.roll(oa, HEAD_DIM, 1), ob)
        zbuf[:, D_CONV + t * LANES:D_CONV + (t + 1) * LANES] = z.astype(BF16)

    y = _dot(zbuf[...], wout_ref[0])
    out = _layer_norm(ALPHA * x + y, g_ref[layer, ln:ln + 1, :], b_ref[layer, ln:ln + 1, :])
    o_ref[...] = out
    q = _dot(out.astype(BF16), wq_ref[0]) * (MEM_HEAD_DIM ** -0.5)
    for row, col in _head_piece_cols():
        q8_ref[:, row, :] = q[:, col:col + LANES]


def _mix_sample(x, sinks, st, kc, vc, prev_windows, win, cw, wout, wq, g, b, layer, ln, bb):
    nb = x.shape[0]
    kern = functools.partial(_mix_sample_kernel, bb=bb, layer=layer, ln=ln)
    row = lambda i, s: (i, 0)
    win_blk = pl.BlockSpec((1, bb, D_KV, WINDOW), lambda i, s: (layer, i, 0, 0))
    in_specs = [
        pl.BlockSpec((bb, D_MODEL), row),
        pl.BlockSpec((bb, (CONV_W - 1) * D_CONV), row),
        win_blk,
        win_blk,
        _weight((D_MODEL, IN_COLS)),
        _resident((DEPTH, CONV_W, D_CONV)),
        _weight((D_CONV + D_ATTN, D_MODEL)),
        _weight((D_MODEL, D_MODEL)),
    ] + _ln_specs()
    args = [sinks, x, st, kc, vc, win, cw, wout, wq, g, b]
    aliases = {}
    out_win_blk = win_blk
    if prev_windows is None:
        out_win_blk = pl.BlockSpec((DEPTH, bb, D_KV, WINDOW), lambda i, s: (0, i, 0, 0))
    else:
        n_in = len(args)
        in_specs += [pl.BlockSpec(memory_space=pl.ANY)] * 2
        args += list(prev_windows)
        aliases = {n_in: 2, n_in + 1: 3}
    grid_spec = pltpu.PrefetchScalarGridSpec(
        num_scalar_prefetch=1,
        grid=(nb // bb,),
        in_specs=in_specs,
        out_specs=[pl.BlockSpec((bb, D_MODEL), row), pl.BlockSpec((bb, (CONV_W - 1) * D_CONV), row),
                   out_win_blk, out_win_blk, pl.BlockSpec((bb, _MEM_ROWS, LANES), lambda i, s: (i, 0, 0))],
        scratch_shapes=[pltpu.VMEM((bb, D_CONV + D_ATTN), BF16)]
        + [pltpu.VMEM((bb, N_HEADS, LANES), F32)] * 4,
    )
    return pl.pallas_call(
        kern,
        grid_spec=grid_spec,
        out_shape=[
            jax.ShapeDtypeStruct((nb, D_MODEL), F32),
            jax.ShapeDtypeStruct((nb, (CONV_W - 1) * D_CONV), F32),
            jax.ShapeDtypeStruct((DEPTH, nb, D_KV, WINDOW), F32),
            jax.ShapeDtypeStruct((DEPTH, nb, D_KV, WINDOW), F32),
            jax.ShapeDtypeStruct((nb, _MEM_ROWS, LANES), F32),
        ],
        input_output_aliases=aliases,
        compiler_params=_params("arbitrary"),
        name="mix_sample",
    )(*args)


_MEM_HALVES = MEM_HEAD_DIM // LANES
_MEM_ROWS = _MEM_HALVES * MEM_HEADS


def _head_piece_cols():
    return [(half * MEM_HEADS + h, h * MEM_HEAD_DIM + half * LANES)
            for half in range(_MEM_HALVES) for h in range(MEM_HEADS)]


SC_LANES = 16
SC_TOKENS = 16
_SC_ROWS = SC_TOKENS * _MEM_ROWS
_SC_CHUNKS = MEM_LEN // SC_TOKENS
_SC_PIECES = LANES // SC_LANES


def _sc_cross_attention(q8, mk_tiles, mv_tiles, layer, seq0, nseq):
    total_seq = mk_tiles.shape[1]
    kflat = mk_tiles.reshape(-1, LANES)
    vflat = mv_tiles.reshape(-1, LANES)
    chunk0 = (layer * total_seq + seq0) * _SC_CHUNKS
    mesh = plsc.VectorSubcoreMesh(core_axis_name="c", subcore_axis_name="s")

    @pl.kernel(out_type=jax.ShapeDtypeStruct((nseq * _MEM_ROWS, LANES), F32), mesh=mesh,
               scratch_types=[pltpu.VMEM((MEM_HEADS, SC_LANES), F32), pltpu.VMEM((MEM_HEADS, SC_LANES), F32)],
               compiler_params=pltpu.CompilerParams(needs_layout_passes=False))
    def attend(q_hbm, k_hbm, v_hbm, o_hbm, m_ref, l_ref):
        def body(idx, q_vmem, k_vmem, v_vmem, o_vmem):
            chunk = idx[1]
            lane = lax.iota(jnp.int32, SC_LANES)
            first = jnp.full((SC_LANES,), chunk, jnp.int32) == 0

            @pl.when(chunk == 0)
            def _():
                for h in range(MEM_HEADS):
                    m_ref[h, :] = jnp.zeros((SC_LANES,), F32)
                    l_ref[h, :] = jnp.zeros((SC_LANES,), F32)
                for r in range(_MEM_ROWS):
                    for j in range(_SC_PIECES):
                        o_vmem[r, pl.ds(j * SC_LANES, SC_LANES)] = jnp.zeros((SC_LANES,), F32)

            def one_head(h, carry):
                acc = [jnp.zeros((SC_LANES,), F32) for _ in range(SC_TOKENS)]
                for half in range(_MEM_HALVES):
                    r = half * MEM_HEADS + h
                    for j in range(_SC_PIECES):
                        sl = pl.ds(j * SC_LANES, SC_LANES)
                        qv = q_vmem[r, sl]
                        for t in range(SC_TOKENS):
                            acc[t] = acc[t] + k_vmem[t * _MEM_ROWS + r, sl] * qv
                s = jnp.zeros((SC_LANES,), F32)
                for t in range(SC_TOKENS):
                    s = jnp.where(lane == t, jnp.sum(acc[t]), s)

                m_old = m_ref[h, :]
                c_max = jnp.full((SC_LANES,), jnp.max(s), F32)
                m_new = jnp.where(first, c_max, jnp.maximum(m_old, c_max))
                alpha = jnp.where(first, 0.0, jnp.exp(m_old - m_new))
                p = jnp.exp(s - m_new)
                l_ref[h, :] = l_ref[h, :] * alpha + jnp.sum(p)
                m_ref[h, :] = m_new

                pt = [jnp.full((SC_LANES,), p[t], F32) for t in range(SC_TOKENS)]
                for half in range(_MEM_HALVES):
                    r = half * MEM_HEADS + h
                    for j in range(_SC_PIECES):
                        sl = pl.ds(j * SC_LANES, SC_LANES)
                        o = o_vmem[r, sl] * alpha
                        for t in range(SC_TOKENS):
                            o = o + pt[t] * v_vmem[t * _MEM_ROWS + r, sl]
                        o_vmem[r, sl] = o
                return carry

            lax.fori_loop(0, MEM_HEADS, one_head, 0)

            @pl.when(chunk == _SC_CHUNKS - 1)
            def _():
                for h in range(MEM_HEADS):
                    inv = 1.0 / l_ref[h, :]
                    for half in range(_MEM_HALVES):
                        r = half * MEM_HEADS + h
                        for j in range(_SC_PIECES):
                            sl = pl.ds(j * SC_LANES, SC_LANES)
                            o_vmem[r, sl] = o_vmem[r, sl] * inv

        pltpu.emit_pipeline(
            body,
            grid=(nseq, _SC_CHUNKS),
            in_specs=[pl.BlockSpec((_MEM_ROWS, LANES), lambda s, c: (seq0 + s, 0)),
                      pl.BlockSpec((_SC_ROWS, LANES), lambda s, c: (chunk0 + s * _SC_CHUNKS + c, 0)),
                      pl.BlockSpec((_SC_ROWS, LANES), lambda s, c: (chunk0 + s * _SC_CHUNKS + c, 0))],
            out_specs=[pl.BlockSpec((_MEM_ROWS, LANES), lambda s, c: (s, 0))],
            core_axis_name=("c", "s"),
            dimension_semantics=(pltpu.PARALLEL, pltpu.ARBITRARY),
            _explicit_indices=True,
        )(q_hbm, k_hbm, v_hbm, o_hbm)

    return attend(q8, kflat, vflat)


TM_PROMPT = 1024
BB_MIX_SAMPLE = 32


def kernel(x_prompt, x_sample, mem_prompt, cache_win_k, cache_win_v, state_conv, cache_mem_k, cache_mem_v,
           ln_g, ln_b, ffn1_w_gu, ffn1_w_down, w_in, conv_w, attn_sinks, w_out,
           w_cq, w_mk, w_mv, w_co, ffn2_w_gu, ffn2_w_down):
    bsz, seq, _ = x_prompt.shape
    nsmp = x_sample.shape[0]
    yp = x_prompt
    ys = x_sample.reshape(nsmp, D_MODEL)
    mem2 = mem_prompt.reshape(bsz * MEM_LEN, D_MODEL)

    stacked = dict(ffn1=(ffn1_w_gu, ffn1_w_down), ffn2=(ffn2_w_gu, ffn2_w_down),
                   mix=(w_in, w_out), cross=(w_cq, w_co, w_mk, w_mv))
    first_tile, wg0, wu0, wd0 = _ffn_first_tile(x_prompt.reshape(bsz * seq, D_MODEL), ffn1_w_gu, ffn1_w_down,
                                                ln_g, ln_b, 0, 0, TM_PROMPT)
    wb = {("ffn1", 0): ((wg0, wu0), wd0)}

    def jobs(*groups):
        return [(w, layer) for name, layer in groups for w in stacked[name]]

    def keep(groups, casts):
        casts = list(casts)
        for name, layer in groups:
            wb[(name, layer)] = tuple(casts[:len(stacked[name])])
            del casts[:len(stacked[name])]

    def token_tiles(c):
        c = c.reshape(DEPTH, nsmp, MEM_LEN, MEM_HEADS, _MEM_HALVES, LANES)
        return c.transpose(0, 1, 2, 4, 3, 5).reshape(DEPTH, nsmp, MEM_LEN * _MEM_ROWS, LANES)

    mem_k_tiles, mem_v_tiles = token_tiles(cache_mem_k), token_tiles(cache_mem_v)

    def window_lanes(c):
        return c.transpose(0, 1, 3, 4, 2).reshape(DEPTH, nsmp, D_KV, WINDOW)

    win_k_lanes, win_v_lanes = window_lanes(cache_win_k), window_lanes(cache_win_v)
    new_windows = None
    mem_tiles = None

    wkp, wvp, cvp, cvs = [], [], [], []
    for l in range(DEPTH):
        more = l + 1 < DEPTH
        side = [("mix", 0), ("cross", 0)] if l == 0 else [("ffn2", l)]
        yp, ys, *casts = _ffn(yp.reshape(bsz * seq, D_MODEL), ys, *wb[("ffn1", l)], ln_g, ln_b, l, 0, TM_PROMPT,
                              jobs(*side), first_tile=first_tile if l == 0 else None)
        keep(side, casts)
        yp = yp.reshape(bsz, seq, D_MODEL)

        win, wout = wb[("mix", l)]
        side = [("ffn2", 0)] if l == 0 else []
        yp, nk, nv, ncv, *casts = _mix_prompt(yp, attn_sinks, win, conv_w, wout, ln_g, ln_b, l, 1, TM_PROMPT,
                                              jobs(*side))
        keep(side, casts)
        wkp.append(nk.reshape(bsz, WINDOW, N_KV_HEADS, HEAD_DIM))
        wvp.append(nv.reshape(bsz, WINDOW, N_KV_HEADS, HEAD_DIM))
        cvp.append(ncv)
        wcq, wco, wmk, wmv = wb[("cross", l)]
        ys, nst, nks, nvs, q8 = _mix_sample(
            ys, attn_sinks, state_conv[l].reshape(nsmp, (CONV_W - 1) * D_CONV),
            win_k_lanes, win_v_lanes, new_windows, win, conv_w, wout, wcq, ln_g, ln_b, l, 1, BB_MIX_SAMPLE)
        new_windows = (nks, nvs)
        cvs.append(nst.reshape(nsmp, CONV_W - 1, D_CONV))

        o8 = _sc_cross_attention(q8.reshape(nsmp * _MEM_ROWS, LANES), mem_k_tiles, mem_v_tiles, l, 0, nsmp)
        mk_all, mv_all, mkb, mvb = _memkv(mem2, wmk, wmv, l, mem_tiles)
        mem_tiles = (mk_all, mv_all)
        side = [("ffn1", l + 1)] if more else []
        yp, *casts = _cross_prompt(yp, mkb.reshape(bsz, MEM_LEN, D_MODEL), mvb.reshape(bsz, MEM_LEN, D_MODEL),
                                   wcq, wco, ln_g, ln_b, l, 2, TM_PROMPT, jobs(*side))
        keep(side, casts)

        side = [("mix", l + 1), ("cross", l + 1)] if more else []
        yp, ys, *casts = _ffn(yp.reshape(bsz * seq, D_MODEL), ys, *wb[("ffn2", l)], ln_g, ln_b, l, 3, TM_PROMPT,
                              jobs(*side), sample_cross=(o8.reshape(nsmp, _MEM_ROWS, LANES), wco, 2))
        keep(side, casts)
        yp = yp.reshape(bsz, seq, D_MODEL)

    def window_rows(c):
        return c.reshape(DEPTH, nsmp, N_KV_HEADS, HEAD_DIM, WINDOW).transpose(0, 1, 4, 2, 3)

    def token_rows(c):
        c = c.reshape(DEPTH, bsz, MEM_LEN, _MEM_HALVES, MEM_HEADS, LANES)
        return c.transpose(0, 1, 2, 4, 3, 5).reshape(DEPTH, bsz, MEM_LEN, MEM_HEADS, MEM_HEAD_DIM)

    return (yp, ys.reshape(nsmp, 1, D_MODEL),
            jnp.stack(wkp), jnp.stack(wvp), jnp.stack(cvp), token_rows(mem_tiles[0]), token_rows(mem_tiles[1]),
            window_rows(new_windows[0]), window_rows(new_windows[1]), jnp.stack(cvs))
```

```python
import functools

import jax
import jax.numpy as jnp
from jax import lax
from jax.experimental import pallas as pl
from jax.experimental.pallas import tpu as pltpu
from jax.experimental.pallas import tpu_sc as plsc

D_MODEL = 1024
DEPTH = 2
D_CONV = 512
CONV_W = 3
HEAD_DIM = 64
N_HEADS = 8
N_KV_HEADS = 2
GROUP = N_HEADS // N_KV_HEADS
WINDOW = 128
D_ATTN = N_HEADS * HEAD_DIM
D_KV = N_KV_HEADS * HEAD_DIM
IN_COLS = 3 * D_CONV + D_ATTN + 2 * D_KV
MEM_LEN = 256
MEM_HEADS = 4
MEM_HEAD_DIM = D_MODEL // MEM_HEADS
D_FF = 2816
N_LN = 4
ALPHA = (2.0 * DEPTH) ** 0.25
LN_EPS = 1e-5
LOG2E = 1.4426950408889634

_C_BG, _C_CG, _C_HC, _C_Q, _C_KV = 0, D_CONV, 2 * D_CONV, 3 * D_CONV, 3 * D_CONV + D_ATTN

LANES = 128
SUBLANES = 8
MXU_COLS = 256
VMEM_LIMIT_BYTES = 56 * 1024 * 1024

BF16 = jnp.bfloat16
F32 = jnp.float32


def _dot(a, b):
    return jnp.dot(a, b, preferred_element_type=F32)


def _dot_nt(a, b):
    return lax.dot_general(a, b, (((1,), (1,)), ((), ())), preferred_element_type=F32)


def _layer_norm(z, g, b):
    mu = jnp.mean(z, axis=-1, keepdims=True)
    d = z - mu
    var = jnp.mean(d * d, axis=-1, keepdims=True)
    return d * lax.rsqrt(var + LN_EPS) * g + b


def _resident(shape):
    zeros = (0,) * len(shape)
    return pl.BlockSpec(shape, lambda *_: zeros, pipeline_mode=pl.Buffered(1))


def _weight(shape):
    return _resident((1,) + shape)


def _cast_specs(jobs, n_steps, step_of):
    in_specs, out_specs, out_shapes, args = [], [], [], []
    for w, layer in jobs:
        _, k, n = w.shape
        rows = k // n_steps
        assert rows * n_steps == k and rows % (2 * SUBLANES) == 0, (w.shape, n_steps)
        in_specs.append(pl.BlockSpec((1, rows, n), lambda *idx, layer=layer: (layer, step_of(*idx), 0)))
        out_specs.append(pl.BlockSpec((1, rows, n), lambda *idx: (0, step_of(*idx), 0)))
        out_shapes.append(jax.ShapeDtypeStruct((1, k, n), BF16))
        args.append(w)
    return in_specs, out_specs, out_shapes, args


def _run_casts(src_refs, dst_refs):
    for src, dst in zip(src_refs, dst_refs, strict=True):
        dst[...] = src[...].astype(BF16)


def _ln_specs():
    return [_resident((DEPTH, N_LN, D_MODEL)), _resident((DEPTH, N_LN, D_MODEL))]


def _params(*sem):
    return pltpu.CompilerParams(dimension_semantics=sem, vmem_limit_bytes=VMEM_LIMIT_BYTES)


NORM_ROWS = 256


def _project_and_norm(lhs_ref, w, x_rows, scale, g, b, write_rows, rows):
    for r0 in range(0, rows, min(NORM_ROWS, rows)):
        r1 = r0 + min(NORM_ROWS, rows)
        y = _dot(lhs_ref[r0:r1, :], w)
        write_rows(r0, r1, _layer_norm(ALPHA * x_rows(r0, r1) + scale * y, g, b))


def _swiglu(gate, up):
    return (gate * jax.nn.sigmoid(gate) * up).astype(BF16)


def _ffn_rows(x_ref, o_ref, wg_ref, wu_ref, wd_ref, h_ref, g, b):
    rows = x_ref.shape[0]
    xb = x_ref[...].astype(BF16)
    for c in range(D_FF // MXU_COLS):
        lo = c * MXU_COLS
        h_ref[0:rows, lo:lo + MXU_COLS] = _swiglu(_dot(xb, wg_ref[0, :, lo:lo + MXU_COLS]),
                                                  _dot(xb, wu_ref[0, :, lo:lo + MXU_COLS]))

    def write_rows(r0, r1, y):
        o_ref[r0:r1, :] = y

    _project_and_norm(h_ref, wd_ref[0], lambda r0, r1: x_ref[r0:r1, :], 0.5, g, b, write_rows, rows)


def _ffn_kernel(x_ref, xs_ref, wg_ref, wu_ref, wd_ref, g_ref, b_ref, *rest,
                layer, ln, n_cast, n_tiles, cross_ln, has_first):
    n_pre = (0 if cross_ln is None else 2) + (1 if has_first else 0)
    pre, rest = rest[:n_pre], rest[n_pre:]
    cast_src, (o_ref, os_ref), rest = rest[:n_cast], rest[n_cast:n_cast + 2], rest[n_cast + 2:]
    cast_dst, scratch = rest[:n_cast], rest[n_cast:]
    h_ref = scratch[0]
    g = g_ref[layer, ln:ln + 1, :]
    b = b_ref[layer, ln:ln + 1, :]
    step = pl.program_id(0)

    if has_first:
        @pl.when(step == 0)
        def _():
            _run_casts(cast_src, cast_dst)
            o_ref[...] = pre[-1][...]

    @pl.when((step < n_tiles) & (step > 0) if has_first else step < n_tiles)
    def _():
        _run_casts(cast_src, cast_dst)
        _ffn_rows(x_ref, o_ref, wg_ref, wu_ref, wd_ref, h_ref, g, b)

    @pl.when(step == n_tiles)
    def _():
        if cross_ln is None:
            _ffn_rows(xs_ref, os_ref, wg_ref, wu_ref, wd_ref, h_ref, g, b)
        else:
            o8_ref, wo_ref = pre[:2]
            obuf, xs_scr = scratch[1:]
            for row, col in _head_piece_cols():
                obuf[:, col:col + LANES] = o8_ref[:, row, :].astype(BF16)
            xs_scr[...] = _layer_norm(ALPHA * xs_ref[...] + _dot(obuf[...], wo_ref[0]),
                                      g_ref[layer, cross_ln:cross_ln + 1, :], b_ref[layer, cross_ln:cross_ln + 1, :])
            _ffn_rows(xs_scr, os_ref, wg_ref, wu_ref, wd_ref, h_ref, g, b)


def _ffn(x, xs, wgu, wd, g, b, layer, ln, tm, casts=(), sample_cross=None, first_tile=None):
    m = x.shape[0]
    ns = xs.shape[0]
    n_tiles = m // tm
    tile = lambda i: (jnp.minimum(i, n_tiles - 1), 0)
    c_in, c_out, c_shapes, c_args = _cast_specs(casts, n_tiles, lambda i: jnp.minimum(i, n_tiles - 1))
    half = lambda j: pl.BlockSpec((1, D_MODEL, D_FF), lambda i: (0, 0, j), pipeline_mode=pl.Buffered(1))
    if isinstance(wgu, tuple):
        w_args, w_specs = list(wgu), [half(0), half(0)]
    else:
        w_args, w_specs = [wgu, wgu], [half(0), half(1)]
    pre_specs, pre_args, pre_scratch, cross_ln = [], [], [], None
    if sample_cross is not None:
        o8, wo, cross_ln = sample_cross
        pre_specs = [_resident((ns, _MEM_ROWS, LANES)), _weight((D_MODEL, D_MODEL))]
        pre_args = [o8, wo]
        pre_scratch = [pltpu.VMEM((ns, D_MODEL), BF16), pltpu.VMEM((ns, D_MODEL), F32)]
    if first_tile is not None:
        pre_specs.append(_resident((tm, D_MODEL)))
        pre_args.append(first_tile)
    return pl.pallas_call(
        functools.partial(_ffn_kernel, layer=layer, ln=ln, n_cast=len(casts), n_tiles=n_tiles, cross_ln=cross_ln,
                          has_first=first_tile is not None),
        grid=(n_tiles + 1,),
        in_specs=[pl.BlockSpec((tm, D_MODEL), tile), _resident((ns, D_MODEL))] + w_specs
        + [_weight((D_FF, D_MODEL))] + _ln_specs() + pre_specs + c_in,
        out_specs=[pl.BlockSpec((tm, D_MODEL), tile), pl.BlockSpec((ns, D_MODEL), lambda i: (0, 0))] + c_out,
        out_shape=[jax.ShapeDtypeStruct((m, D_MODEL), F32), jax.ShapeDtypeStruct((ns, D_MODEL), F32)] + c_shapes,
        scratch_shapes=[pltpu.VMEM((tm, D_FF), BF16)] + pre_scratch,
        compiler_params=_params("arbitrary"),
        name="ffn_ln",
    )(x, xs, *w_args, wd, g, b, *pre_args, *c_args)


def _ffn_first_tile_kernel(x_ref, wg_ref, wu_ref, wd_ref, g_ref, b_ref, o_ref, wgb_ref, wub_ref, wdb_ref,
                           xb_scr, h_scr, wd_scr, *, layer, ln):
    c = pl.program_id(0)
    n_chunks = h_scr.shape[0]

    @pl.when(c == 0)
    def _():
        xb_scr[...] = x_ref[...].astype(BF16)

    wg = wg_ref[0].astype(BF16)
    wu = wu_ref[0].astype(BF16)
    wd = wd_ref[0].astype(BF16)
    wgb_ref[0] = wg
    wub_ref[0] = wu
    wdb_ref[0] = wd
    wd_scr[c] = wd
    xb = xb_scr[...]
    h_scr[c] = _swiglu(_dot(xb, wg), _dot(xb, wu))

    @pl.when(c == n_chunks - 1)
    def _():
        rows = x_ref.shape[0]
        g = g_ref[layer, ln:ln + 1, :]
        b = b_ref[layer, ln:ln + 1, :]
        for r0 in range(0, rows, NORM_ROWS):
            r1 = r0 + NORM_ROWS
            y = _dot(h_scr[0, r0:r1, :], wd_scr[0])
            for k in range(1, n_chunks):
                y = y + _dot(h_scr[k, r0:r1, :], wd_scr[k])
            o_ref[r0:r1, :] = _layer_norm(ALPHA * x_ref[r0:r1, :] + 0.5 * y, g, b)


def _ffn_first_tile(x, wgu32, wd32, g, b, layer, ln, tm):
    n_chunks = D_FF // MXU_COLS
    return pl.pallas_call(
        functools.partial(_ffn_first_tile_kernel, layer=layer, ln=ln),
        grid=(n_chunks,),
        in_specs=[_resident((tm, D_MODEL)),
                  pl.BlockSpec((1, D_MODEL, MXU_COLS), lambda c: (layer, 0, c)),
                  pl.BlockSpec((1, D_MODEL, MXU_COLS), lambda c: (layer, 0, n_chunks + c)),
                  pl.BlockSpec((1, MXU_COLS, D_MODEL), lambda c: (layer, c, 0))] + _ln_specs(),
        out_specs=[pl.BlockSpec((tm, D_MODEL), lambda c: (0, 0)),
                   pl.BlockSpec((1, D_MODEL, MXU_COLS), lambda c: (0, 0, c)),
                   pl.BlockSpec((1, D_MODEL, MXU_COLS), lambda c: (0, 0, c)),
                   pl.BlockSpec((1, MXU_COLS, D_MODEL), lambda c: (0, c, 0))],
        out_shape=[jax.ShapeDtypeStruct((tm, D_MODEL), F32),
                   jax.ShapeDtypeStruct((1, D_MODEL, D_FF), BF16),
                   jax.ShapeDtypeStruct((1, D_MODEL, D_FF), BF16),
                   jax.ShapeDtypeStruct((1, D_FF, D_MODEL), BF16)],
        scratch_shapes=[pltpu.VMEM((tm, D_MODEL), BF16),
                        pltpu.VMEM((n_chunks, tm, MXU_COLS), BF16),
                        pltpu.VMEM((n_chunks, MXU_COLS, D_MODEL), BF16)],
        compiler_params=_params("arbitrary"),
        name="ffn_first_tile",
    )(x, wgu32, wgu32, wd32, g, b)


def _dup_halves(a):
    lane = lax.broadcasted_iota(jnp.int32, a.shape, 1)
    low = lane < HEAD_DIM
    rolled = pltpu.roll(a, HEAD_DIM, 1)
    return jnp.where(low, a, rolled), jnp.where(low, rolled, a)


def _sink_column(sink_ref, layer, first_head, rows_per_head, n_heads):
    rows = n_heads * rows_per_head
    r = lax.broadcasted_iota(jnp.int32, (rows, 1), 0)
    col = jnp.full((rows, 1), sink_ref[layer, first_head + n_heads - 1], F32)
    for g in range(n_heads - 2, -1, -1):
        col = jnp.where(r < (g + 1) * rows_per_head, sink_ref[layer, first_head + g], col)
    return col


def _mix_prompt_kernel(sink_ref, x_ref, win_ref, cw_ref, wout_ref, g_ref, b_ref, *rest, tm, layer, ln, n_cast):
    cast_src, rest = rest[:n_cast], rest[n_cast:]
    (o_ref, nk_ref, nv_ref, nc_ref), rest = rest[:4], rest[4:]
    cast_dst, (ubuf, qbuf, kd0, kd1, vd0, vd1, zbuf) = rest[:n_cast], rest[n_cast:]
    seq_start = pl.program_id(1) == 0

    @pl.when(seq_start)
    def _():
        ubuf[0:SUBLANES, :] = jnp.zeros((SUBLANES, D_CONV), F32)
        for r in (kd0, kd1, vd0, vd1):
            r[0:WINDOW, :] = jnp.zeros((WINDOW, LANES), BF16)

    _run_casts(cast_src, cast_dst)
    xb = x_ref[0].astype(BF16)

    def proj(col, width):
        return _dot(xb, win_ref[0, :, col:col + width])

    u = proj(_C_CG, D_CONV) * proj(_C_HC, D_CONV)
    ubuf[SUBLANES:SUBLANES + tm, :] = u
    conv = (cw_ref[layer, 0:1, :] * ubuf[SUBLANES - 2:SUBLANES - 2 + tm, :]
            + cw_ref[layer, 1:2, :] * ubuf[SUBLANES - 1:SUBLANES - 1 + tm, :]
            + cw_ref[layer, 2:3, :] * u)
    zbuf[:, 0:D_CONV] = (proj(_C_BG, D_CONV) * conv).astype(BF16)
    nc_ref[0] = u[tm - (CONV_W - 1):tm, :]
    ubuf[0:SUBLANES, :] = u[tm - SUBLANES:tm, :]

    qbuf[...] = (proj(_C_Q, D_ATTN) * (HEAD_DIM ** -0.5 * LOG2E)).astype(BF16)
    k = proj(_C_KV, D_KV)
    v = proj(_C_KV + D_KV, D_KV)
    nk_ref[0] = k[tm - WINDOW:tm, :]
    nv_ref[0] = v[tm - WINDOW:tm, :]
    ka, kb = _dup_halves(k)
    va, vb = _dup_halves(v)
    kd0[WINDOW:WINDOW + tm, :] = ka.astype(BF16)
    kd1[WINDOW:WINDOW + tm, :] = kb.astype(BF16)
    vd0[WINDOW:WINDOW + tm, :] = va.astype(BF16)
    vd1[WINDOW:WINDOW + tm, :] = vb.astype(BF16)

    rows = GROUP * WINDOW
    a_idx = lax.broadcasted_iota(jnp.int32, (rows, 2 * WINDOW), 0) % WINDOW
    c_idx = lax.broadcasted_iota(jnp.int32, (rows, 2 * WINDOW), 1)
    band = (c_idx >= a_idx) & (c_idx <= a_idx + WINDOW)
    first_lb = jnp.where(seq_start, WINDOW, 0)
    low = lax.broadcasted_iota(jnp.int32, (WINDOW, LANES), 1) < HEAD_DIM
    zero = jnp.zeros((WINDOW, LANES), BF16)

    for n in range(tm // WINDOW):
        r0 = n * WINDOW
        mask = (band & (c_idx >= first_lb)) if n == 0 else band
        for j, (kd, vd) in enumerate(((kd0, vd0), (kd1, vd1))):
            qa = qbuf[r0:r0 + WINDOW, (2 * j) * LANES:(2 * j + 1) * LANES]
            qb = qbuf[r0:r0 + WINDOW, (2 * j + 1) * LANES:(2 * j + 2) * LANES]
            qs = jnp.concatenate([jnp.where(low, qa, zero), jnp.where(low, zero, qa),
                                  jnp.where(low, qb, zero), jnp.where(low, zero, qb)], axis=0)
            s = _dot_nt(qs, kd[r0:r0 + 2 * WINDOW, :])
            s = jnp.where(mask, s, -jnp.inf)
            sink = _sink_column(sink_ref, layer, GROUP * j, WINDOW, GROUP) * LOG2E
            m = jnp.maximum(jnp.max(s, axis=-1, keepdims=True), sink)
            p = jnp.exp2(s - m)
            denom = jnp.sum(p, axis=-1, keepdims=True) + jnp.exp2(sink - m)
            o = _dot(p.astype(BF16), vd[r0:r0 + 2 * WINDOW, :]) / denom
            za = jnp.where(low, o[0:WINDOW], o[WINDOW:2 * WINDOW])
            zb = jnp.where(low, o[2 * WINDOW:3 * WINDOW], o[3 * WINDOW:4 * WINDOW])
            c0 = D_CONV + (2 * j) * LANES
            zbuf[r0:r0 + WINDOW, c0:c0 + LANES] = za.astype(BF16)
            zbuf[r0:r0 + WINDOW, c0 + LANES:c0 + 2 * LANES] = zb.astype(BF16)

    for r in (kd0, kd1, vd0, vd1):
        r[0:WINDOW, :] = r[tm:tm + WINDOW, :]

    def write_rows(r0, r1, y):
        o_ref[0, r0:r1, :] = y

    _project_and_norm(zbuf, wout_ref[0], lambda r0, r1: x_ref[0, r0:r1, :], 1.0,
                      g_ref[layer, ln:ln + 1, :], b_ref[layer, ln:ln + 1, :], write_rows, tm)


def _mix_prompt(x, sinks, win, cw, wout, g, b, layer, ln, tm, casts=()):
    bsz, seq, _ = x.shape
    tps = seq // tm
    kern = functools.partial(_mix_prompt_kernel, tm=tm, layer=layer, ln=ln, n_cast=len(casts))
    per_seq = lambda bi, i, s: (bi, 0, 0)
    c_in, c_out, c_shapes, c_args = _cast_specs(casts, bsz * tps, lambda bi, i, *_: bi * tps + i)
    grid_spec = pltpu.PrefetchScalarGridSpec(
        num_scalar_prefetch=1,
        grid=(bsz, tps),
        in_specs=[
            pl.BlockSpec((1, tm, D_MODEL), lambda bi, i, s: (bi, i, 0)),
            _weight((D_MODEL, IN_COLS)),
            _resident((DEPTH, CONV_W, D_CONV)),
            _weight((D_CONV + D_ATTN, D_MODEL)),
        ] + _ln_specs() + c_in,
        out_specs=[
            pl.BlockSpec((1, tm, D_MODEL), lambda bi, i, s: (bi, i, 0)),
            pl.BlockSpec((1, WINDOW, D_KV), per_seq),
            pl.BlockSpec((1, WINDOW, D_KV), per_seq),
            pl.BlockSpec((1, CONV_W - 1, D_CONV), per_seq),
        ] + c_out,
        scratch_shapes=[
            pltpu.VMEM((tm + SUBLANES, D_CONV), F32),
            pltpu.VMEM((tm, D_ATTN), BF16),
            pltpu.VMEM((tm + WINDOW, LANES), BF16),
            pltpu.VMEM((tm + WINDOW, LANES), BF16),
            pltpu.VMEM((tm + WINDOW, LANES), BF16),
            pltpu.VMEM((tm + WINDOW, LANES), BF16),
            pltpu.VMEM((tm, D_CONV + D_ATTN), BF16),
        ],
    )
    return pl.pallas_call(
        kern,
        grid_spec=grid_spec,
        out_shape=[
            jax.ShapeDtypeStruct((bsz, seq, D_MODEL), F32),
            jax.ShapeDtypeStruct((bsz, WINDOW, D_KV), F32),
            jax.ShapeDtypeStruct((bsz, WINDOW, D_KV), F32),
            jax.ShapeDtypeStruct((bsz, CONV_W - 1, D_CONV), F32),
        ] + c_shapes,
        compiler_params=_params("arbitrary", "arbitrary"),
        name="mix_prompt",
    )(sinks, x, win, cw, wout, g, b, *c_args)


def _memkv_kernel(m_ref, wk_ref, wv_ref, *rest, layer):
    k_ref, v_ref, kb_ref, vb_ref = rest[-4:]
    mb = m_ref[...].astype(BF16)
    k = _dot(mb, wk_ref[0])
    v = _dot(mb, wv_ref[0])
    kb_ref[...] = k.astype(BF16)
    vb_ref[...] = v.astype(BF16)
    slab = layer if k_ref.shape[0] > 1 else 0
    for other in range(k_ref.shape[0]):
        if other != slab:
            k_ref[other] = jnp.zeros(k_ref.shape[1:], F32)
            v_ref[other] = jnp.zeros(v_ref.shape[1:], F32)
    for row, col in _head_piece_cols():
        k_ref[slab, 0, :, row, :] = k[:, col:col + LANES]
        v_ref[slab, 0, :, row, :] = v[:, col:col + LANES]


def _memkv(mem, wk, wv, layer, prev):
    m = mem.shape[0]
    nb = m // MEM_LEN
    blk = pl.BlockSpec((MEM_LEN, D_MODEL), lambda i: (i, 0))
    in_specs = [blk, _weight((D_MODEL, D_MODEL)), _weight((D_MODEL, D_MODEL))]
    args = [mem, wk, wv]
    aliases = {}
    if prev is None:
        tile_blk = pl.BlockSpec((DEPTH, 1, MEM_LEN, _MEM_ROWS, LANES), lambda i: (0, i, 0, 0, 0))
    else:
        tile_blk = pl.BlockSpec((1, 1, MEM_LEN, _MEM_ROWS, LANES), lambda i: (layer, i, 0, 0, 0))
        in_specs += [pl.BlockSpec(memory_space=pl.ANY)] * 2
        args += list(prev)
        aliases = {3: 0, 4: 1}
    tiles = jax.ShapeDtypeStruct((DEPTH, nb, MEM_LEN, _MEM_ROWS, LANES), F32)
    return pl.pallas_call(
        functools.partial(_memkv_kernel, layer=layer),
        grid=(nb,),
        in_specs=in_specs,
        out_specs=[tile_blk, tile_blk, blk, blk],
        out_shape=[tiles, tiles, jax.ShapeDtypeStruct((m, D_MODEL), BF16), jax.ShapeDtypeStruct((m, D_MODEL), BF16)],
        input_output_aliases=aliases,
        compiler_params=_params("arbitrary"),
        name="mem_kv",
    )(*args)


def _cross_prompt_kernel(x_ref, mk_ref, mv_ref, wq_ref, wo_ref, g_ref, b_ref, *rest, layer, ln, n_cast):
    cast_src, o_ref, cast_dst, obuf = rest[:n_cast], rest[n_cast], rest[n_cast + 1:2 * n_cast + 1], rest[-1]
    _run_casts(cast_src, cast_dst)
    q = _dot(x_ref[0].astype(BF16), wq_ref[0]) * (MEM_HEAD_DIM ** -0.5 * LOG2E)
    for h in range(MEM_HEADS):
        c0 = h * MEM_HEAD_DIM
        s = _dot_nt(q[:, c0:c0 + MEM_HEAD_DIM].astype(BF16), mk_ref[0, :, c0:c0 + MEM_HEAD_DIM])
        m = jnp.max(s, axis=-1, keepdims=True)
        p = jnp.exp2(s - m)
        denom = jnp.sum(p, axis=-1, keepdims=True)
        o = _dot(p.astype(BF16), mv_ref[0, :, c0:c0 + MEM_HEAD_DIM]) / denom
        obuf[:, c0:c0 + MEM_HEAD_DIM] = o.astype(BF16)

    def write_rows(r0, r1, y):
        o_ref[0, r0:r1, :] = y

    _project_and_norm(obuf, wo_ref[0], lambda r0, r1: x_ref[0, r0:r1, :], 1.0,
                      g_ref[layer, ln:ln + 1, :], b_ref[layer, ln:ln + 1, :], write_rows, x_ref.shape[1])


def _cross_prompt(x, mk, mv, wq, wo, g, b, layer, ln, tm, casts=()):
    bsz, seq, _ = x.shape
    tps = seq // tm
    xblk = pl.BlockSpec((1, tm, D_MODEL), lambda bi, i: (bi, i, 0))
    mblk = pl.BlockSpec((1, MEM_LEN, D_MODEL), lambda bi, i: (bi, 0, 0))
    c_in, c_out, c_shapes, c_args = _cast_specs(casts, bsz * tps, lambda bi, i: bi * tps + i)
    return pl.pallas_call(
        functools.partial(_cross_prompt_kernel, layer=layer, ln=ln, n_cast=len(casts)),
        grid=(bsz, tps),
        in_specs=[xblk, mblk, mblk, _weight((D_MODEL, D_MODEL)), _weight((D_MODEL, D_MODEL))] + _ln_specs() + c_in,
        out_specs=[xblk] + c_out,
        out_shape=[jax.ShapeDtypeStruct((bsz, seq, D_MODEL), F32)] + c_shapes,
        scratch_shapes=[pltpu.VMEM((tm, D_MODEL), BF16)],
        compiler_params=_params("arbitrary", "arbitrary"),
        name="cross_prompt",
    )(x, mk, mv, wq, wo, g, b, *c_args)


def _bdot(a, b, contract_b):
    return lax.dot_general(a, b, (((2,), (contract_b,)), ((0,), (0,))), preferred_element_type=F32)


def _mix_sample_kernel(sink_ref, x_ref, st_ref, kc_ref, vc_ref, win_ref, cw_ref, wout_ref, wq_ref, g_ref, b_ref,
                       *rest, bb, layer, ln):
    o_ref, nst_ref, nk_ref, nv_ref, q8_ref, zbuf, qm_scr, kn_scr, vn_scr, o_scr = rest[-10:]
    x = x_ref[...]
    xb = x.astype(BF16)

    def proj(col, width):
        return _dot(xb, win_ref[0, :, col:col + width])

    u = proj(_C_CG, D_CONV) * proj(_C_HC, D_CONV)
    st0 = st_ref[:, 0:D_CONV]
    st1 = st_ref[:, D_CONV:2 * D_CONV]
    conv = cw_ref[layer, 0:1, :] * st0 + cw_ref[layer, 1:2, :] * st1 + cw_ref[layer, 2:3, :] * u
    zbuf[:, 0:D_CONV] = (proj(_C_BG, D_CONV) * conv).astype(BF16)
    nst_ref[:, 0:D_CONV] = st1
    nst_ref[:, D_CONV:2 * D_CONV] = u

    q = proj(_C_Q, D_ATTN) * (HEAD_DIM ** -0.5)
    k = proj(_C_KV, D_KV)
    v = proj(_C_KV + D_KV, D_KV)

    pad = jnp.zeros((LANES - bb, D_KV), F32)
    kt = jnp.concatenate([k, pad], axis=0).T if bb < LANES else k.T
    vt = jnp.concatenate([v, pad], axis=0).T if bb < LANES else v.T
    last = lax.broadcasted_iota(jnp.int32, (D_KV, WINDOW), 1) == WINDOW - 1
    slab = layer if nk_ref.shape[0] > 1 else 0
    for other in range(nk_ref.shape[0]):
        if other != slab:
            nk_ref[other] = jnp.zeros(nk_ref.shape[1:], F32)
            nv_ref[other] = jnp.zeros(nv_ref.shape[1:], F32)
    for r in range(bb):
        nk_ref[slab, r] = jnp.where(last, kt[:, r:r + 1], pltpu.roll(kc_ref[0, r], WINDOW - 1, 1))
        nv_ref[slab, r] = jnp.where(last, vt[:, r:r + 1], pltpu.roll(vc_ref[0, r], WINDOW - 1, 1))

    lane = lax.broadcasted_iota(jnp.int32, (bb, LANES), 1)
    low = lane < HEAD_DIM
    for h in range(N_HEADS):
        kvh = h // GROUP
        qt = q[:, (h // 2) * LANES:(h // 2 + 1) * LANES]
        if (h % 2) != kvh:
            qt = pltpu.roll(qt, HEAD_DIM, 1)
        keep = low if kvh == 0 else jnp.logical_not(low)
        qm_scr[:, h, :] = jnp.where(keep, qt, 0.0)
        kn_scr[:, h, :] = k
        vn_scr[:, h, :] = v
    qm = qm_scr[...]

    s = _bdot(qm.astype(BF16), kc_ref[0].astype(BF16), 1)
    s_new = jnp.sum(qm * kn_scr[...], axis=-1, keepdims=True)
    hrow = lax.broadcasted_iota(jnp.int32, (1, N_HEADS, 1), 1)
    sink = jnp.full((1, N_HEADS, 1), sink_ref[layer, N_HEADS - 1], F32)
    for h in range(N_HEADS - 1):
        sink = jnp.where(hrow == h, sink_ref[layer, h], sink)
    m = jnp.maximum(jnp.maximum(jnp.max(s, axis=-1, keepdims=True), s_new), sink)
    p = jnp.exp(s - m)
    p_new = jnp.exp(s_new - m)
    denom = jnp.sum(p, axis=-1, keepdims=True) + p_new + jnp.exp(sink - m)
    o_scr[...] = (_bdot(p.astype(BF16), vc_ref[0].astype(BF16), 2) + p_new * vn_scr[...]) / denom

    for t in range(D_ATTN // LANES):
        oa = o_scr[:, 2 * t, :]
        ob = o_scr[:, 2 * t + 1, :]
        if (2 * t) // GROUP == 0:
            z = jnp.where(low, oa, pltpu.roll(ob, HEAD_DIM, 1))
        else:
            z = jnp.where(low, pltpu.roll(oa, HEAD_DIM, 1), ob)
        zbuf[:, D_CONV + t * LANES:D_CONV + (t + 1) * LANES] = z.astype(BF16)

    y = _dot(zbuf[...], wout_ref[0])
    out = _layer_norm(ALPHA * x + y, g_ref[layer, ln:ln + 1, :], b_ref[layer, ln:ln + 1, :])
    o_ref[...] = out
    q = _dot(out.astype(BF16), wq_ref[0]) * (MEM_HEAD_DIM ** -0.5)
    for row, col in _head_piece_cols():
        q8_ref[:, row, :] = q[:, col:col + LANES]


def _mix_sample(x, sinks, st, kc, vc, prev_windows, win, cw, wout, wq, g, b, layer, ln, bb):
    nb = x.shape[0]
    kern = functools.partial(_mix_sample_kernel, bb=bb, layer=layer, ln=ln)
    row = lambda i, s: (i, 0)
    win_blk = pl.BlockSpec((1, bb, D_KV, WINDOW), lambda i, s: (layer, i, 0, 0))
    in_specs = [
        pl.BlockSpec((bb, D_MODEL), row),
        pl.BlockSpec((bb, (CONV_W - 1) * D_CONV), row),
        win_blk,
        win_blk,
        _weight((D_MODEL, IN_COLS)),
        _resident((DEPTH, CONV_W, D_CONV)),
        _weight((D_CONV + D_ATTN, D_MODEL)),
        _weight((D_MODEL, D_MODEL)),
    ] + _ln_specs()
    args = [sinks, x, st, kc, vc, win, cw, wout, wq, g, b]
    aliases = {}
    out_win_blk = win_blk
    if prev_windows is None:
        out_win_blk = pl.BlockSpec((DEPTH, bb, D_KV, WINDOW), lambda i, s: (0, i, 0, 0))
    else:
        n_in = len(args)
        in_specs += [pl.BlockSpec(memory_space=pl.ANY)] * 2
        args += list(prev_windows)
        aliases = {n_in: 2, n_in + 1: 3}
    grid_spec = pltpu.PrefetchScalarGridSpec(
        num_scalar_prefetch=1,
        grid=(nb // bb,),
        in_specs=in_specs,
        out_specs=[pl.BlockSpec((bb, D_MODEL), row), pl.BlockSpec((bb, (CONV_W - 1) * D_CONV), row),
                   out_win_blk, out_win_blk, pl.BlockSpec((bb, _MEM_ROWS, LANES), lambda i, s: (i, 0, 0))],
        scratch_shapes=[pltpu.VMEM((bb, D_CONV + D_ATTN), BF16)]
        + [pltpu.VMEM((bb, N_HEADS, LANES), F32)] * 4,
    )
    return pl.pallas_call(
        kern,
        grid_spec=grid_spec,
        out_shape=[
            jax.ShapeDtypeStruct((nb, D_MODEL), F32),
            jax.ShapeDtypeStruct((nb, (CONV_W - 1) * D_CONV), F32),
            jax.ShapeDtypeStruct((DEPTH, nb, D_KV, WINDOW), F32),
            jax.ShapeDtypeStruct((DEPTH, nb, D_KV, WINDOW), F32),
            jax.ShapeDtypeStruct((nb, _MEM_ROWS, LANES), F32),
        ],
        input_output_aliases=aliases,
        compiler_params=_params("arbitrary"),
        name="mix_sample",
    )(*args)


_MEM_HALVES = MEM_HEAD_DIM // LANES
_MEM_ROWS = _MEM_HALVES * MEM_HEADS


def _head_piece_cols():
    return [(half * MEM_HEADS + h, h * MEM_HEAD_DIM + half * LANES)
            for half in range(_MEM_HALVES) for h in range(MEM_HEADS)]


SC_LANES = 16
SC_TOKENS = 16
_SC_ROWS = SC_TOKENS * _MEM_ROWS
_SC_CHUNKS = MEM_LEN // SC_TOKENS
_SC_PIECES = LANES // SC_LANES


def _sc_cross_attention(q8, mk_tiles, mv_tiles, layer, seq0, nseq):
    total_seq = mk_tiles.shape[1]
    kflat = mk_tiles.reshape(-1, LANES)
    vflat = mv_tiles.reshape(-1, LANES)
    chunk0 = (layer * total_seq + seq0) * _SC_CHUNKS
    mesh = plsc.VectorSubcoreMesh(core_axis_name="c", subcore_axis_name="s")

    @pl.kernel(out_type=jax.ShapeDtypeStruct((nseq * _MEM_ROWS, LANES), F32), mesh=mesh,
               scratch_types=[pltpu.VMEM((MEM_HEADS, SC_LANES), F32), pltpu.VMEM((MEM_HEADS, SC_LANES), F32)],
               compiler_params=pltpu.CompilerParams(needs_layout_passes=False))
    def attend(q_hbm, k_hbm, v_hbm, o_hbm, m_ref, l_ref):
        def body(idx, q_vmem, k_vmem, v_vmem, o_vmem):
            chunk = idx[1]
            lane = lax.iota(jnp.int32, SC_LANES)
            first = jnp.full((SC_LANES,), chunk, jnp.int32) == 0

            @pl.when(chunk == 0)
            def _():
                for h in range(MEM_HEADS):
                    m_ref[h, :] = jnp.zeros((SC_LANES,), F32)
                    l_ref[h, :] = jnp.zeros((SC_LANES,), F32)
                for r in range(_MEM_ROWS):
                    for j in range(_SC_PIECES):
                        o_vmem[r, pl.ds(j * SC_LANES, SC_LANES)] = jnp.zeros((SC_LANES,), F32)

            def one_head(h, carry):
                acc = [jnp.zeros((SC_LANES,), F32) for _ in range(SC_TOKENS)]
                for half in range(_MEM_HALVES):
                    r = half * MEM_HEADS + h
                    for j in range(_SC_PIECES):
                        sl = pl.ds(j * SC_LANES, SC_LANES)
                        qv = q_vmem[r, sl]
                        for t in range(SC_TOKENS):
                            acc[t] = acc[t] + k_vmem[t * _MEM_ROWS + r, sl] * qv
                s = jnp.zeros((SC_LANES,), F32)
                for t in range(SC_TOKENS):
                    s = jnp.where(lane == t, jnp.sum(acc[t]), s)

                m_old = m_ref[h, :]
                c_max = jnp.full((SC_LANES,), jnp.max(s), F32)
                m_new = jnp.where(first, c_max, jnp.maximum(m_old, c_max))
                alpha = jnp.where(first, 0.0, jnp.exp(m_old - m_new))
                p = jnp.exp(s - m_new)
                l_ref[h, :] = l_ref[h, :] * alpha + jnp.sum(p)
                m_ref[h, :] = m_new

                pt = [jnp.full((SC_LANES,), p[t], F32) for t in range(SC_TOKENS)]
                for half in range(_MEM_HALVES):
                    r = half * MEM_HEADS + h
                    for j in range(_SC_PIECES):
                        sl = pl.ds(j * SC_LANES, SC_LANES)
                        o = o_vmem[r, sl] * alpha
                        for t in range(SC_TOKENS):
                            o = o + pt[t] * v_vmem[t * _MEM_ROWS + r, sl]
                        o_vmem[r, sl] = o
                return carry

            lax.fori_loop(0, MEM_HEADS, one_head, 0)

            @pl.when(chunk == _SC_CHUNKS - 1)
            def _():
                for h in range(MEM_HEADS):
                    inv = 1.0 / l_ref[h, :]
                    for half in range(_MEM_HALVES):
                        r = half * MEM_HEADS + h
                        for j in range(_SC_PIECES):
                            sl = pl.ds(j * SC_LANES, SC_LANES)
                            o_vmem[r, sl] = o_vmem[r, sl] * inv

        pltpu.emit_pipeline(
            body,
            grid=(nseq, _SC_CHUNKS),
            in_specs=[pl.BlockSpec((_MEM_ROWS, LANES), lambda s, c: (seq0 + s, 0)),
                      pl.BlockSpec((_SC_ROWS, LANES), lambda s, c: (chunk0 + s * _SC_CHUNKS + c, 0)),
                      pl.BlockSpec((_SC_ROWS, LANES), lambda s, c: (chunk0 + s * _SC_CHUNKS + c, 0))],
            out_specs=[pl.BlockSpec((_MEM_ROWS, LANES), lambda s, c: (s, 0))],
            core_axis_name=("c", "s"),
            dimension_semantics=(pltpu.PARALLEL, pltpu.ARBITRARY),
            _explicit_indices=True,
        )(q_hbm, k_hbm, v_hbm, o_hbm)

    return attend(q8, kflat, vflat)


TM_PROMPT = 1024
BB_MIX_SAMPLE = 32


def kernel(x_prompt, x_sample, mem_prompt, cache_win_k, cache_win_v, state_conv, cache_mem_k, cache_mem_v,
           ln_g, ln_b, ffn1_w_gu, ffn1_w_down, w_in, conv_w, attn_sinks, w_out,
           w_cq, w_mk, w_mv, w_co, ffn2_w_gu, ffn2_w_down):
    bsz, seq, _ = x_prompt.shape
    nsmp = x_sample.shape[0]
    yp = x_prompt
    ys = x_sample.reshape(nsmp, D_MODEL)
    mem2 = mem_prompt.reshape(bsz * MEM_LEN, D_MODEL)

    stacked = dict(ffn1=(ffn1_w_gu, ffn1_w_down), ffn2=(ffn2_w_gu, ffn2_w_down),
                   mix=(w_in, w_out), cross=(w_cq, w_co, w_mk, w_mv))
    first_tile, wg0, wu0, wd0 = _ffn_first_tile(x_prompt.reshape(bsz * seq, D_MODEL), ffn1_w_gu, ffn1_w_down,
                                                ln_g, ln_b, 0, 0, TM_PROMPT)
    wb = {("ffn1", 0): ((wg0, wu0), wd0)}

    def jobs(*groups):
        return [(w, layer) for name, layer in groups for w in stacked[name]]

    def keep(groups, casts):
        casts = list(casts)
        for name, layer in groups:
            wb[(name, layer)] = tuple(casts[:len(stacked[name])])
            del casts[:len(stacked[name])]

    def token_tiles(c):
        c = c.reshape(DEPTH, nsmp, MEM_LEN, MEM_HEADS, _MEM_HALVES, LANES)
        return c.transpose(0, 1, 2, 4, 3, 5).reshape(DEPTH, nsmp, MEM_LEN * _MEM_ROWS, LANES)

    mem_k_tiles, mem_v_tiles = token_tiles(cache_mem_k), token_tiles(cache_mem_v)

    def window_lanes(c):
        return c.transpose(0, 1, 3, 4, 2).reshape(DEPTH, nsmp, D_KV, WINDOW)

    win_k_lanes, win_v_lanes = window_lanes(cache_win_k), window_lanes(cache_win_v)
    new_windows = None
    mem_tiles = None

    wkp, wvp, cvp, cvs = [], [], [], []
    for l in range(DEPTH):
        more = l + 1 < DEPTH
        side = [("mix", 0), ("cross", 0)] if l == 0 else [("ffn2", l)]
        yp, ys, *casts = _ffn(yp.reshape(bsz * seq, D_MODEL), ys, *wb[("ffn1", l)], ln_g, ln_b, l, 0, TM_PROMPT,
                              jobs(*side), first_tile=first_tile if l == 0 else None)
        keep(side, casts)
        yp = yp.reshape(bsz, seq, D_MODEL)

        win, wout = wb[("mix", l)]
        side = [("ffn2", 0)] if l == 0 else []
        yp, nk, nv, ncv, *casts = _mix_prompt(yp, attn_sinks, win, conv_w, wout, ln_g, ln_b, l, 1, TM_PROMPT,
                                              jobs(*side))
        keep(side, casts)
        wkp.append(nk.reshape(bsz, WINDOW, N_KV_HEADS, HEAD_DIM))
        wvp.append(nv.reshape(bsz, WINDOW, N_KV_HEADS, HEAD_DIM))
        cvp.append(ncv)
        wcq, wco, wmk, wmv = wb[("cross", l)]
        ys, nst, nks, nvs, q8 = _mix_sample(
            ys, attn_sinks, state_conv[l].reshape(nsmp, (CONV_W - 1) * D_CONV),
            win_k_lanes, win_v_lanes, new_windows, win, conv_w, wout, wcq, ln_g, ln_b, l, 1, BB_MIX_SAMPLE)
        new_windows = (nks, nvs)
        cvs.append(nst.reshape(nsmp, CONV_W - 1, D_CONV))

        o8 = _sc_cross_attention(q8.reshape(nsmp * _MEM_ROWS, LANES), mem_k_tiles, mem_v_tiles, l, 0, nsmp)
        mk_all, mv_all, mkb, mvb = _memkv(mem2, wmk, wmv, l, mem_tiles)
        mem_tiles = (mk_all, mv_all)
        side = [("ffn1", l + 1)] if more else []
        yp, *casts = _cross_prompt(yp, mkb.reshape(bsz, MEM_LEN, D_MODEL), mvb.reshape(bsz, MEM_LEN, D_MODEL),
                                   wcq, wco, ln_g, ln_b, l, 2, TM_PROMPT, jobs(*side))
        keep(side, casts)

        side = [("mix", l + 1), ("cross", l + 1)] if more else []
        yp, ys, *casts = _ffn(yp.reshape(bsz * seq, D_MODEL), ys, *wb[("ffn2", l)], ln_g, ln_b, l, 3, TM_PROMPT,
                              jobs(*side), sample_cross=(o8.reshape(nsmp, _MEM_ROWS, LANES), wco, 2))
        keep(side, casts)
        yp = yp.reshape(bsz, seq, D_MODEL)

    def window_rows(c):
        return c.reshape(DEPTH, nsmp, N_KV_HEADS, HEAD_DIM, WINDOW).transpose(0, 1, 4, 2, 3)

    def token_rows(c):
        c = c.reshape(DEPTH, bsz, MEM_LEN, _MEM_HALVES, MEM_HEADS, LANES)
        return c.transpose(0, 1, 2, 4, 3, 5).reshape(DEPTH, bsz, MEM_LEN, MEM_HEADS, MEM_HEAD_DIM)

    return (yp, ys.reshape(nsmp, 1, D_MODEL),
            jnp.stack(wkp), jnp.stack(wvp), jnp.stack(cvp), token_rows(mem_tiles[0]), token_rows(mem_tiles[1]),
            window_rows(new_windows[0]), window_rows(new_windows[1]), jnp.stack(cvs))
```

```python
import functools

import jax
import jax.numpy as jnp
from jax import lax
from jax.experimental import pallas as pl
from jax.experimental.pallas import tpu as pltpu
from jax.experimental.pallas import tpu_sc as plsc

D_MODEL = 1024
DEPTH = 2
D_CONV = 512
CONV_W = 3
HEAD_DIM = 64
N_HEADS = 8
N_KV_HEADS = 2
GROUP = N_HEADS // N_KV_HEADS
WINDOW = 128
D_ATTN = N_HEADS * HEAD_DIM
D_KV = N_KV_HEADS * HEAD_DIM
IN_COLS = 3 * D_CONV + D_ATTN + 2 * D_KV
MEM_LEN = 256
MEM_HEADS = 4
MEM_HEAD_DIM = D_MODEL // MEM_HEADS
D_FF = 2816
N_LN = 4
ALPHA = (2.0 * DEPTH) ** 0.25
LN_EPS = 1e-5
LOG2E = 1.4426950408889634

_C_BG, _C_CG, _C_HC, _C_Q, _C_KV = 0, D_CONV, 2 * D_CONV, 3 * D_CONV, 3 * D_CONV + D_ATTN

LANES = 128
SUBLANES = 8
MXU_COLS = 256
VMEM_LIMIT_BYTES = 56 * 1024 * 1024

BF16 = jnp.bfloat16
F32 = jnp.float32


def _dot(a, b):
    return jnp.dot(a, b, preferred_element_type=F32)


def _dot_nt(a, b):
    return lax.dot_general(a, b, (((1,), (1,)), ((), ())), preferred_element_type=F32)


def _layer_norm(z, g, b):
    mu = jnp.mean(z, axis=-1, keepdims=True)
    d = z - mu
    var = jnp.mean(d * d, axis=-1, keepdims=True)
    return d * lax.rsqrt(var + LN_EPS) * g + b


def _resident(shape):
    zeros = (0,) * len(shape)
    return pl.BlockSpec(shape, lambda *_: zeros, pipeline_mode=pl.Buffered(1))


def _weight(shape):
    return _resident((1,) + shape)


def _cast_specs(jobs, n_steps, step_of):
    in_specs, out_specs, out_shapes, args = [], [], [], []
    for w, layer in jobs:
        _, k, n = w.shape
        rows = k // n_steps
        assert rows * n_steps == k and rows % (2 * SUBLANES) == 0, (w.shape, n_steps)
        in_specs.append(pl.BlockSpec((1, rows, n), lambda *idx, layer=layer: (layer, step_of(*idx), 0)))
        out_specs.append(pl.BlockSpec((1, rows, n), lambda *idx: (0, step_of(*idx), 0)))
        out_shapes.append(jax.ShapeDtypeStruct((1, k, n), BF16))
        args.append(w)
    return in_specs, out_specs, out_shapes, args


def _run_casts(src_refs, dst_refs):
    for src, dst in zip(src_refs, dst_refs, strict=True):
        dst[...] = src[...].astype(BF16)


def _ln_specs():
    return [_resident((DEPTH, N_LN, D_MODEL)), _resident((DEPTH, N_LN, D_MODEL))]


def _params(*sem):
    return pltpu.CompilerParams(dimension_semantics=sem, vmem_limit_bytes=VMEM_LIMIT_BYTES)


NORM_ROWS = 256


def _project_and_norm(lhs_ref, w, x_rows, scale, g, b, write_rows, rows):
    for r0 in range(0, rows, min(NORM_ROWS, rows)):
        r1 = r0 + min(NORM_ROWS, rows)
        y = _dot(lhs_ref[r0:r1, :], w)
        write_rows(r0, r1, _layer_norm(ALPHA * x_rows(r0, r1) + scale * y, g, b))


def _swiglu(gate, up):
    return (gate * jax.nn.sigmoid(gate) * up).astype(BF16)


def _ffn_rows(x_ref, o_ref, wg_ref, wu_ref, wd_ref, h_ref, g, b):
    rows = x_ref.shape[0]
    xb = x_ref[...].astype(BF16)
    for c in range(D_FF // MXU_COLS):
        lo = c * MXU_COLS
        h_ref[0:rows, lo:lo + MXU_COLS] = _swiglu(_dot(xb, wg_ref[0, :, lo:lo + MXU_COLS]),
                                                  _dot(xb, wu_ref[0, :, lo:lo + MXU_COLS]))

    def write_rows(r0, r1, y):
        o_ref[r0:r1, :] = y

    _project_and_norm(h_ref, wd_ref[0], lambda r0, r1: x_ref[r0:r1, :], 0.5, g, b, write_rows, rows)


def _ffn_kernel(x_ref, xs_ref, wg_ref, wu_ref, wd_ref, g_ref, b_ref, *rest,
                layer, ln, n_cast, n_tiles, cross_ln, has_first):
    n_pre = (0 if cross_ln is None else 2) + (1 if has_first else 0)
    pre, rest = rest[:n_pre], rest[n_pre:]
    cast_src, (o_ref, os_ref), rest = rest[:n_cast], rest[n_cast:n_cast + 2], rest[n_cast + 2:]
    cast_dst, scratch = rest[:n_cast], rest[n_cast:]
    h_ref = scratch[0]
    g = g_ref[layer, ln:ln + 1, :]
    b = b_ref[layer, ln:ln + 1, :]
    step = pl.program_id(0)

    if has_first:
        @pl.when(step == 0)
        def _():
            _run_casts(cast_src, cast_dst)
            o_ref[...] = pre[-1][...]

    @pl.when((step < n_tiles) & (step > 0) if has_first else step < n_tiles)
    def _():
        _run_casts(cast_src, cast_dst)
        _ffn_rows(x_ref, o_ref, wg_ref, wu_ref, wd_ref, h_ref, g, b)

    @pl.when(step == n_tiles)
    def _():
        if cross_ln is None:
            _ffn_rows(xs_ref, os_ref, wg_ref, wu_ref, wd_ref, h_ref, g, b)
        else:
            o8_ref, wo_ref = pre[:2]
            obuf, xs_scr = scratch[1:]
            for row, col in _head_piece_cols():
                obuf[:, col:col + LANES] = o8_ref[:, row, :].astype(BF16)
            xs_scr[...] = _layer_norm(ALPHA * xs_ref[...] + _dot(obuf[...], wo_ref[0]),
                                      g_ref[layer, cross_ln:cross_ln + 1, :], b_ref[layer, cross_ln:cross_ln + 1, :])
            _ffn_rows(xs_scr, os_ref, wg_ref, wu_ref, wd_ref, h_ref, g, b)


def _ffn(x, xs, wgu, wd, g, b, layer, ln, tm, casts=(), sample_cross=None, first_tile=None):
    m = x.shape[0]
    ns = xs.shape[0]
    n_tiles = m // tm
    tile = lambda i: (jnp.minimum(i, n_tiles - 1), 0)
    c_in, c_out, c_shapes, c_args = _cast_specs(casts, n_tiles, lambda i: jnp.minimum(i, n_tiles - 1))
    half = lambda j: pl.BlockSpec((1, D_MODEL, D_FF), lambda i: (0, 0, j), pipeline_mode=pl.Buffered(1))
    if isinstance(wgu, tuple):
        w_args, w_specs = list(wgu), [half(0), half(0)]
    else:
        w_args, w_specs = [wgu, wgu], [half(0), half(1)]
    pre_specs, pre_args, pre_scratch, cross_ln = [], [], [], None
    if sample_cross is not None:
        o8, wo, cross_ln = sample_cross
        pre_specs = [_resident((ns, _MEM_ROWS, LANES)), _weight((D_MODEL, D_MODEL))]
        pre_args = [o8, wo]
        pre_scratch = [pltpu.VMEM((ns, D_MODEL), BF16), pltpu.VMEM((ns, D_MODEL), F32)]
    if first_tile is not None:
        pre_specs.append(_resident((tm, D_MODEL)))
        pre_args.append(first_tile)
    return pl.pallas_call(
        functools.partial(_ffn_kernel, layer=layer, ln=ln, n_cast=len(casts), n_tiles=n_tiles, cross_ln=cross_ln,
                          has_first=first_tile is not None),
        grid=(n_tiles + 1,),
        in_specs=[pl.BlockSpec((tm, D_MODEL), tile), _resident((ns, D_MODEL))] + w_specs
        + [_weight((D_FF, D_MODEL))] + _ln_specs() + pre_specs + c_in,
        out_specs=[pl.BlockSpec((tm, D_MODEL), tile), pl.BlockSpec((ns, D_MODEL), lambda i: (0, 0))] + c_out,
        out_shape=[jax.ShapeDtypeStruct((m, D_MODEL), F32), jax.ShapeDtypeStruct((ns, D_MODEL), F32)] + c_shapes,
        scratch_shapes=[pltpu.VMEM((tm, D_FF), BF16)] + pre_scratch,
        compiler_params=_params("arbitrary"),
        name="ffn_ln",
    )(x, xs, *w_args, wd, g, b, *pre_args, *c_args)


def _ffn_first_tile_kernel(x_ref, wg_ref, wu_ref, wd_ref, g_ref, b_ref, o_ref, wgb_ref, wub_ref, wdb_ref,
                           xb_scr, h_scr, wd_scr, *, layer, ln):
    c = pl.program_id(0)
    n_chunks = h_scr.shape[0]

    @pl.when(c == 0)
    def _():
        xb_scr[...] = x_ref[...].astype(BF16)

    wg = wg_ref[0].astype(BF16)
    wu = wu_ref[0].astype(BF16)
    wd = wd_ref[0].astype(BF16)
    wgb_ref[0] = wg
    wub_ref[0] = wu
    wdb_ref[0] = wd
    wd_scr[c] = wd
    xb = xb_scr[...]
    h_scr[c] = _swiglu(_dot(xb, wg), _dot(xb, wu))

    @pl.when(c == n_chunks - 1)
    def _():
        rows = x_ref.shape[0]
        g = g_ref[layer, ln:ln + 1, :]
        b = b_ref[layer, ln:ln + 1, :]
        for r0 in range(0, rows, NORM_ROWS):
            r1 = r0 + NORM_ROWS
            y = _dot(h_scr[0, r0:r1, :], wd_scr[0])
            for k in range(1, n_chunks):
                y = y + _dot(h_scr[k, r0:r1, :], wd_scr[k])
            o_ref[r0:r1, :] = _layer_norm(ALPHA * x_ref[r0:r1, :] + 0.5 * y, g, b)


def _ffn_first_tile(x, wgu32, wd32, g, b, layer, ln, tm):
    n_chunks = D_FF // MXU_COLS
    return pl.pallas_call(
        functools.partial(_ffn_first_tile_kernel, layer=layer, ln=ln),
        grid=(n_chunks,),
        in_specs=[_resident((tm, D_MODEL)),
                  pl.BlockSpec((1, D_MODEL, MXU_COLS), lambda c: (layer, 0, c)),
                  pl.BlockSpec((1, D_MODEL, MXU_COLS), lambda c: (layer, 0, n_chunks + c)),
                  pl.BlockSpec((1, MXU_COLS, D_MODEL), lambda c: (layer, c, 0))] + _ln_specs(),
        out_specs=[pl.BlockSpec((tm, D_MODEL), lambda c: (0, 0)),
                   pl.BlockSpec((1, D_MODEL, MXU_COLS), lambda c: (0, 0, c)),
                   pl.BlockSpec((1, D_MODEL, MXU_COLS), lambda c: (0, 0, c)),
                   pl.BlockSpec((1, MXU_COLS, D_MODEL), lambda c: (0, c, 0))],
        out_shape=[jax.ShapeDtypeStruct((tm, D_MODEL), F32),
                   jax.ShapeDtypeStruct((1, D_MODEL, D_FF), BF16),
                   jax.ShapeDtypeStruct((1, D_MODEL, D_FF), BF16),
                   jax.ShapeDtypeStruct((1, D_FF, D_MODEL), BF16)],
        scratch_shapes=[pltpu.VMEM((tm, D_MODEL), BF16),
                        pltpu.VMEM((n_chunks, tm, MXU_COLS), BF16),
                        pltpu.VMEM((n_chunks, MXU_COLS, D_MODEL), BF16)],
        compiler_params=_params("arbitrary"),
        name="ffn_first_tile",
    )(x, wgu32, wgu32, wd32, g, b)


def _dup_halves(a):
    lane = lax.broadcasted_iota(jnp.int32, a.shape, 1)
    low = lane < HEAD_DIM
    rolled = pltpu.roll(a, HEAD_DIM, 1)
    return jnp.where(low, a, rolled), jnp.where(low, rolled, a)


def _sink_column(sink_ref, layer, first_head, rows_per_head, n_heads):
    rows = n_heads * rows_per_head
    r = lax.broadcasted_iota(jnp.int32, (rows, 1), 0)
    col = jnp.full((rows, 1), sink_ref[layer, first_head + n_heads - 1], F32)
    for g in range(n_heads - 2, -1, -1):
        col = jnp.where(r < (g + 1) * rows_per_head, sink_ref[layer, first_head + g], col)
    return col


def _mix_prompt_kernel(sink_ref, x_ref, win_ref, cw_ref, wout_ref, g_ref, b_ref, *rest, tm, layer, ln, n_cast):
    cast_src, rest = rest[:n_cast], rest[n_cast:]
    (o_ref, nk_ref, nv_ref, nc_ref), rest = rest[:4], rest[4:]
    cast_dst, (ubuf, qbuf, kd0, kd1, vd0, vd1, zbuf) = rest[:n_cast], rest[n_cast:]
    seq_start = pl.program_id(1) == 0

    @pl.when(seq_start)
    def _():
        ubuf[0:SUBLANES, :] = jnp.zeros((SUBLANES, D_CONV), F32)
        for r in (kd0, kd1, vd0, vd1):
            r[0:WINDOW, :] = jnp.zeros((WINDOW, r.shape[1]), BF16)

    _run_casts(cast_src, cast_dst)
    xb = x_ref[0].astype(BF16)

    def proj(col, width):
        return _dot(xb, win_ref[0, :, col:col + width])

    u = proj(_C_CG, D_CONV) * proj(_C_HC, D_CONV)
    ubuf[SUBLANES:SUBLANES + tm, :] = u
    conv = (cw_ref[layer, 0:1, :] * ubuf[SUBLANES - 2:SUBLANES - 2 + tm, :]
            + cw_ref[layer, 1:2, :] * ubuf[SUBLANES - 1:SUBLANES - 1 + tm, :]
            + cw_ref[layer, 2:3, :] * u)
    zbuf[:, 0:D_CONV] = (proj(_C_BG, D_CONV) * conv).astype(BF16)
    nc_ref[0] = u[tm - (CONV_W - 1):tm, :]
    ubuf[0:SUBLANES, :] = u[tm - SUBLANES:tm, :]

    qbuf[...] = (proj(_C_Q, D_ATTN) * (HEAD_DIM ** -0.5 * LOG2E)).astype(BF16)
    k = proj(_C_KV, D_KV)
    v = proj(_C_KV + D_KV, D_KV)
    nk_ref[0] = k[tm - WINDOW:tm, :]
    nv_ref[0] = v[tm - WINDOW:tm, :]
    ka, kb = _dup_halves(k)
    va, vb = _dup_halves(v)
    kd0[WINDOW:WINDOW + tm, :] = ka.astype(BF16)
    kd1[WINDOW:WINDOW + tm, :] = kb.astype(BF16)
    ones = jnp.ones((tm, LANES), BF16)
    vd0[WINDOW:WINDOW + tm, :] = jnp.concatenate([va.astype(BF16), ones], axis=1)
    vd1[WINDOW:WINDOW + tm, :] = jnp.concatenate([vb.astype(BF16), ones], axis=1)

    rows = GROUP * WINDOW
    a_idx = lax.broadcasted_iota(jnp.int32, (rows, 2 * WINDOW), 0) % WINDOW
    c_idx = lax.broadcasted_iota(jnp.int32, (rows, 2 * WINDOW), 1)
    band = (c_idx >= a_idx) & (c_idx <= a_idx + WINDOW)
    first_lb = jnp.where(seq_start, WINDOW, 0)
    low = lax.broadcasted_iota(jnp.int32, (WINDOW, LANES), 1) < HEAD_DIM
    zero = jnp.zeros((WINDOW, LANES), BF16)

    for n in range(tm // WINDOW):
        r0 = n * WINDOW
        mask = (band & (c_idx >= first_lb)) if n == 0 else band
        for j, (kd, vd) in enumerate(((kd0, vd0), (kd1, vd1))):
            qa = qbuf[r0:r0 + WINDOW, (2 * j) * LANES:(2 * j + 1) * LANES]
            qb = qbuf[r0:r0 + WINDOW, (2 * j + 1) * LANES:(2 * j + 2) * LANES]
            qs = jnp.concatenate([jnp.where(low, qa, zero), jnp.where(low, zero, qa),
                                  jnp.where(low, qb, zero), jnp.where(low, zero, qb)], axis=0)
            s = _dot_nt(qs, kd[r0:r0 + 2 * WINDOW, :])
            s = jnp.where(mask, s, -jnp.inf)
            sink = _sink_column(sink_ref, layer, GROUP * j, WINDOW, GROUP) * LOG2E
            m = jnp.maximum(jnp.max(s, axis=-1, keepdims=True), sink)
            p = jnp.exp2(s - m)
            ov = _dot(p.astype(BF16), vd[r0:r0 + 2 * WINDOW, :])
            o = ov[:, 0:LANES] / (ov[:, LANES:2 * LANES] + jnp.exp2(sink - m))
            za = jnp.where(low, o[0:WINDOW], o[WINDOW:2 * WINDOW])
            zb = jnp.where(low, o[2 * WINDOW:3 * WINDOW], o[3 * WINDOW:4 * WINDOW])
            c0 = D_CONV + (2 * j) * LANES
            zbuf[r0:r0 + WINDOW, c0:c0 + LANES] = za.astype(BF16)
            zbuf[r0:r0 + WINDOW, c0 + LANES:c0 + 2 * LANES] = zb.astype(BF16)

    for r in (kd0, kd1, vd0, vd1):
        r[0:WINDOW, :] = r[tm:tm + WINDOW, :]

    def write_rows(r0, r1, y):
        o_ref[0, r0:r1, :] = y

    _project_and_norm(zbuf, wout_ref[0], lambda r0, r1: x_ref[0, r0:r1, :], 1.0,
                      g_ref[layer, ln:ln + 1, :], b_ref[layer, ln:ln + 1, :], write_rows, tm)


def _mix_prompt(x, sinks, win, cw, wout, g, b, layer, ln, tm, casts=()):
    bsz, seq, _ = x.shape
    tps = seq // tm
    kern = functools.partial(_mix_prompt_kernel, tm=tm, layer=layer, ln=ln, n_cast=len(casts))
    per_seq = lambda bi, i, s: (bi, 0, 0)
    c_in, c_out, c_shapes, c_args = _cast_specs(casts, bsz * tps, lambda bi, i, *_: bi * tps + i)
    grid_spec = pltpu.PrefetchScalarGridSpec(
        num_scalar_prefetch=1,
        grid=(bsz, tps),
        in_specs=[
            pl.BlockSpec((1, tm, D_MODEL), lambda bi, i, s: (bi, i, 0)),
            _weight((D_MODEL, IN_COLS)),
            _resident((DEPTH, CONV_W, D_CONV)),
            _weight((D_CONV + D_ATTN, D_MODEL)),
        ] + _ln_specs() + c_in,
        out_specs=[
            pl.BlockSpec((1, tm, D_MODEL), lambda bi, i, s: (bi, i, 0)),
            pl.BlockSpec((1, WINDOW, D_KV), per_seq),
            pl.BlockSpec((1, WINDOW, D_KV), per_seq),
            pl.BlockSpec((1, CONV_W - 1, D_CONV), per_seq),
        ] + c_out,
        scratch_shapes=[
            pltpu.VMEM((tm + SUBLANES, D_CONV), F32),
            pltpu.VMEM((tm, D_ATTN), BF16),
            pltpu.VMEM((tm + WINDOW, LANES), BF16),
            pltpu.VMEM((tm + WINDOW, LANES), BF16),
            pltpu.VMEM((tm + WINDOW, 2 * LANES), BF16),
            pltpu.VMEM((tm + WINDOW, 2 * LANES), BF16),
            pltpu.VMEM((tm, D_CONV + D_ATTN), BF16),
        ],
    )
    return pl.pallas_call(
        kern,
        grid_spec=grid_spec,
        out_shape=[
            jax.ShapeDtypeStruct((bsz, seq, D_MODEL), F32),
            jax.ShapeDtypeStruct((bsz, WINDOW, D_KV), F32),
            jax.ShapeDtypeStruct((bsz, WINDOW, D_KV), F32),
            jax.ShapeDtypeStruct((bsz, CONV_W - 1, D_CONV), F32),
        ] + c_shapes,
        compiler_params=_params("arbitrary", "arbitrary"),
        name="mix_prompt",
    )(sinks, x, win, cw, wout, g, b, *c_args)


def _memkv_kernel(m_ref, wk_ref, wv_ref, *rest, layer):
    k_ref, v_ref, kb_ref, vb_ref = rest[-4:]
    mb = m_ref[...].astype(BF16)
    k = _dot(mb, wk_ref[0])
    v = _dot(mb, wv_ref[0])
    kb_ref[...] = k.astype(BF16)
    vb_ref[...] = v.astype(BF16)
    slab = layer if k_ref.shape[0] > 1 else 0
    for other in range(k_ref.shape[0]):
        if other != slab:
            k_ref[other] = jnp.zeros(k_ref.shape[1:], F32)
            v_ref[other] = jnp.zeros(v_ref.shape[1:], F32)
    for row, col in _head_piece_cols():
        k_ref[slab, 0, :, row, :] = k[:, col:col + LANES]
        v_ref[slab, 0, :, row, :] = v[:, col:col + LANES]


def _memkv(mem, wk, wv, layer, prev):
    m = mem.shape[0]
    nb = m // MEM_LEN
    blk = pl.BlockSpec((MEM_LEN, D_MODEL), lambda i: (i, 0))
    in_specs = [blk, _weight((D_MODEL, D_MODEL)), _weight((D_MODEL, D_MODEL))]
    args = [mem, wk, wv]
    aliases = {}
    if prev is None:
        tile_blk = pl.BlockSpec((DEPTH, 1, MEM_LEN, _MEM_ROWS, LANES), lambda i: (0, i, 0, 0, 0))
    else:
        tile_blk = pl.BlockSpec((1, 1, MEM_LEN, _MEM_ROWS, LANES), lambda i: (layer, i, 0, 0, 0))
        in_specs += [pl.BlockSpec(memory_space=pl.ANY)] * 2
        args += list(prev)
        aliases = {3: 0, 4: 1}
    tiles = jax.ShapeDtypeStruct((DEPTH, nb, MEM_LEN, _MEM_ROWS, LANES), F32)
    return pl.pallas_call(
        functools.partial(_memkv_kernel, layer=layer),
        grid=(nb,),
        in_specs=in_specs,
        out_specs=[tile_blk, tile_blk, blk, blk],
        out_shape=[tiles, tiles, jax.ShapeDtypeStruct((m, D_MODEL), BF16), jax.ShapeDtypeStruct((m, D_MODEL), BF16)],
        input_output_aliases=aliases,
        compiler_params=_params("arbitrary"),
        name="mem_kv",
    )(*args)


def _cross_prompt_kernel(x_ref, mk_ref, mv_ref, wq_ref, wo_ref, g_ref, b_ref, *rest, layer, ln, n_cast):
    cast_src, o_ref, cast_dst, obuf = rest[:n_cast], rest[n_cast], rest[n_cast + 1:2 * n_cast + 1], rest[-1]
    _run_casts(cast_src, cast_dst)
    q = _dot(x_ref[0].astype(BF16), wq_ref[0]) * (MEM_HEAD_DIM ** -0.5 * LOG2E)
    for h in range(MEM_HEADS):
        c0 = h * MEM_HEAD_DIM
        s = _dot_nt(q[:, c0:c0 + MEM_HEAD_DIM].astype(BF16), mk_ref[0, :, c0:c0 + MEM_HEAD_DIM])
        m = jnp.max(s, axis=-1, keepdims=True)
        p = jnp.exp2(s - m)
        denom = jnp.sum(p, axis=-1, keepdims=True)
        o = _dot(p.astype(BF16), mv_ref[0, :, c0:c0 + MEM_HEAD_DIM]) / denom
        obuf[:, c0:c0 + MEM_HEAD_DIM] = o.astype(BF16)

    def write_rows(r0, r1, y):
        o_ref[0, r0:r1, :] = y

    _project_and_norm(obuf, wo_ref[0], lambda r0, r1: x_ref[0, r0:r1, :], 1.0,
                      g_ref[layer, ln:ln + 1, :], b_ref[layer, ln:ln + 1, :], write_rows, x_ref.shape[1])


def _cross_prompt(x, mk, mv, wq, wo, g, b, layer, ln, tm, casts=()):
    bsz, seq, _ = x.shape
    tps = seq // tm
    xblk = pl.BlockSpec((1, tm, D_MODEL), lambda bi, i: (bi, i, 0))
    mblk = pl.BlockSpec((1, MEM_LEN, D_MODEL), lambda bi, i: (bi, 0, 0))
    c_in, c_out, c_shapes, c_args = _cast_specs(casts, bsz * tps, lambda bi, i: bi * tps + i)
    return pl.pallas_call(
        functools.partial(_cross_prompt_kernel, layer=layer, ln=ln, n_cast=len(casts)),
        grid=(bsz, tps),
        in_specs=[xblk, mblk, mblk, _weight((D_MODEL, D_MODEL)), _weight((D_MODEL, D_MODEL))] + _ln_specs() + c_in,
        out_specs=[xblk] + c_out,
        out_shape=[jax.ShapeDtypeStruct((bsz, seq, D_MODEL), F32)] + c_shapes,
        scratch_shapes=[pltpu.VMEM((tm, D_MODEL), BF16)],
        compiler_params=_params("arbitrary", "arbitrary"),
        name="cross_prompt",
    )(x, mk, mv, wq, wo, g, b, *c_args)


def _bdot(a, b, contract_b):
    return lax.dot_general(a, b, (((2,), (contract_b,)), ((0,), (0,))), preferred_element_type=F32)


def _mix_sample_kernel(sink_ref, x_ref, st_ref, kc_ref, vc_ref, win_ref, cw_ref, wout_ref, wq_ref, g_ref, b_ref,
                       *rest, bb, layer, ln):
    o_ref, nst_ref, nk_ref, nv_ref, q8_ref, zbuf, qm_scr, kn_scr, vn_scr, o_scr = rest[-10:]
    x = x_ref[...]
    xb = x.astype(BF16)

    def proj(col, width):
        return _dot(xb, win_ref[0, :, col:col + width])

    u = proj(_C_CG, D_CONV) * proj(_C_HC, D_CONV)
    st0 = st_ref[:, 0:D_CONV]
    st1 = st_ref[:, D_CONV:2 * D_CONV]
    conv = cw_ref[layer, 0:1, :] * st0 + cw_ref[layer, 1:2, :] * st1 + cw_ref[layer, 2:3, :] * u
    zbuf[:, 0:D_CONV] = (proj(_C_BG, D_CONV) * conv).astype(BF16)
    nst_ref[:, 0:D_CONV] = st1
    nst_ref[:, D_CONV:2 * D_CONV] = u

    q = proj(_C_Q, D_ATTN) * (HEAD_DIM ** -0.5)
    k = proj(_C_KV, D_KV)
    v = proj(_C_KV + D_KV, D_KV)

    pad = jnp.zeros((LANES - bb, D_KV), F32)
    kt = jnp.concatenate([k, pad], axis=0).T if bb < LANES else k.T
    vt = jnp.concatenate([v, pad], axis=0).T if bb < LANES else v.T
    last = lax.broadcasted_iota(jnp.int32, (D_KV, WINDOW), 1) == WINDOW - 1
    slab = layer if nk_ref.shape[0] > 1 else 0
    for other in range(nk_ref.shape[0]):
        if other != slab:
            nk_ref[other] = jnp.zeros(nk_ref.shape[1:], F32)
            nv_ref[other] = jnp.zeros(nv_ref.shape[1:], F32)
    for r in range(bb):
        nk_ref[slab, r] = jnp.where(last, kt[:, r:r + 1], pltpu.roll(kc_ref[0, r], WINDOW - 1, 1))
        nv_ref[slab, r] = jnp.where(last, vt[:, r:r + 1], pltpu.roll(vc_ref[0, r], WINDOW - 1, 1))

    lane = lax.broadcasted_iota(jnp.int32, (bb, LANES), 1)
    low = lane < HEAD_DIM
    for h in range(N_HEADS):
        kvh = h // GROUP
        qt = q[:, (h // 2) * LANES:(h // 2 + 1) * LANES]
        if (h % 2) != kvh:
            qt = pltpu.roll(qt, HEAD_DIM, 1)
        keep = low if kvh == 0 else jnp.logical_not(low)
        qm_scr[:, h, :] = jnp.where(keep, qt, 0.0)
        kn_scr[:, h, :] = k
        vn_scr[:, h, :] = v
    qm = qm_scr[...]

    s = _bdot(qm.astype(BF16), kc_ref[0].astype(BF16), 1)
    s_new = jnp.sum(qm * kn_scr[...], axis=-1, keepdims=True)
    hrow = lax.broadcasted_iota(jnp.int32, (1, N_HEADS, 1), 1)
    sink = jnp.full((1, N_HEADS, 1), sink_ref[layer, N_HEADS - 1], F32)
    for h in range(N_HEADS - 1):
        sink = jnp.where(hrow == h, sink_ref[layer, h], sink)
    m = jnp.maximum(jnp.maximum(jnp.max(s, axis=-1, keepdims=True), s_new), sink)
    p = jnp.exp(s - m)
    p_new = jnp.exp(s_new - m)
    denom = jnp.sum(p, axis=-1, keepdims=True) + p_new + jnp.exp(sink - m)
    o_scr[...] = (_bdot(p.astype(BF16), vc_ref[0].astype(BF16), 2) + p_new * vn_scr[...]) / denom

    for t in range(D_ATTN // LANES):
        oa = o_scr[:, 2 * t, :]
        ob = o_scr[:, 2 * t + 1, :]
        if (2 * t) // GROUP == 0:
            z = jnp.where(low, oa, pltpu.roll(ob, HEAD_DIM, 1))
        else:
            z = jnp.where(low, pltpu.roll(oa, HEAD_DIM, 1), ob)
        zbuf[:, D_CONV + t * LANES:D_CONV + (t + 1) * LANES] = z.astype(BF16)

    y = _dot(zbuf[...], wout_ref[0])
    out = _layer_norm(ALPHA * x + y, g_ref[layer, ln:ln + 1, :], b_ref[layer, ln:ln + 1, :])
    o_ref[...] = out
    q = _dot(out.astype(BF16), wq_ref[0]) * (MEM_HEAD_DIM ** -0.5)
    for row, col in _head_piece_cols():
        q8_ref[:, row, :] = q[:, col:col + LANES]


def _mix_sample(x, sinks, st, kc, vc, prev_windows, win, cw, wout, wq, g, b, layer, ln, bb):
    nb = x.shape[0]
    kern = functools.partial(_mix_sample_kernel, bb=bb, layer=layer, ln=ln)
    row = lambda i, s: (i, 0)
    win_blk = pl.BlockSpec((1, bb, D_KV, WINDOW), lambda i, s: (layer, i, 0, 0))
    in_specs = [
        pl.BlockSpec((bb, D_MODEL), row),
        pl.BlockSpec((bb, (CONV_W - 1) * D_CONV), row),
        win_blk,
        win_blk,
        _weight((D_MODEL, IN_COLS)),
        _resident((DEPTH, CONV_W, D_CONV)),
        _weight((D_CONV + D_ATTN, D_MODEL)),
        _weight((D_MODEL, D_MODEL)),
    ] + _ln_specs()
    args = [sinks, x, st, kc, vc, win, cw, wout, wq, g, b]
    aliases = {}
    out_win_blk = win_blk
    if prev_windows is None:
        out_win_blk = pl.BlockSpec((DEPTH, bb, D_KV, WINDOW), lambda i, s: (0, i, 0, 0))
    else:
        n_in = len(args)
        in_specs += [pl.BlockSpec(memory_space=pl.ANY)] * 2
        args += list(prev_windows)
        aliases = {n_in: 2, n_in + 1: 3}
    grid_spec = pltpu.PrefetchScalarGridSpec(
        num_scalar_prefetch=1,
        grid=(nb // bb,),
        in_specs=in_specs,
        out_specs=[pl.BlockSpec((bb, D_MODEL), row), pl.BlockSpec((bb, (CONV_W - 1) * D_CONV), row),
                   out_win_blk, out_win_blk, pl.BlockSpec((bb, _MEM_ROWS, LANES), lambda i, s: (i, 0, 0))],
        scratch_shapes=[pltpu.VMEM((bb, D_CONV + D_ATTN), BF16)]
        + [pltpu.VMEM((bb, N_HEADS, LANES), F32)] * 4,
    )
    return pl.pallas_call(
        kern,
        grid_spec=grid_spec,
        out_shape=[
            jax.ShapeDtypeStruct((nb, D_MODEL), F32),
            jax.ShapeDtypeStruct((nb, (CONV_W - 1) * D_CONV), F32),
            jax.ShapeDtypeStruct((DEPTH, nb, D_KV, WINDOW), F32),
            jax.ShapeDtypeStruct((DEPTH, nb, D_KV, WINDOW), F32),
            jax.ShapeDtypeStruct((nb, _MEM_ROWS, LANES), F32),
        ],
        input_output_aliases=aliases,
        compiler_params=_params("arbitrary"),
        name="mix_sample",
    )(*args)


_MEM_HALVES = MEM_HEAD_DIM // LANES
_MEM_ROWS = _MEM_HALVES * MEM_HEADS


def _head_piece_cols():
    return [(half * MEM_HEADS + h, h * MEM_HEAD_DIM + half * LANES)
            for half in range(_MEM_HALVES) for h in range(MEM_HEADS)]


SC_LANES = 16
SC_TOKENS = 16
_SC_ROWS = SC_TOKENS * _MEM_ROWS
_SC_CHUNKS = MEM_LEN // SC_TOKENS
_SC_PIECES = LANES // SC_LANES


def _sc_cross_attention(q8, mk_tiles, mv_tiles, layer, seq0, nseq):
    total_seq = mk_tiles.shape[1]
    kflat = mk_tiles.reshape(-1, LANES)
    vflat = mv_tiles.reshape(-1, LANES)
    chunk0 = (layer * total_seq + seq0) * _SC_CHUNKS
    mesh = plsc.VectorSubcoreMesh(core_axis_name="c", subcore_axis_name="s")

    @pl.kernel(out_type=jax.ShapeDtypeStruct((nseq * _MEM_ROWS, LANES), F32), mesh=mesh,
               scratch_types=[pltpu.VMEM((MEM_HEADS, SC_LANES), F32), pltpu.VMEM((MEM_HEADS, SC_LANES), F32)],
               compiler_params=pltpu.CompilerParams(needs_layout_passes=False))
    def attend(q_hbm, k_hbm, v_hbm, o_hbm, m_ref, l_ref):
        def body(idx, q_vmem, k_vmem, v_vmem, o_vmem):
            chunk = idx[1]
            lane = lax.iota(jnp.int32, SC_LANES)
            first = jnp.full((SC_LANES,), chunk, jnp.int32) == 0

            @pl.when(chunk == 0)
            def _():
                for h in range(MEM_HEADS):
                    m_ref[h, :] = jnp.zeros((SC_LANES,), F32)
                    l_ref[h, :] = jnp.zeros((SC_LANES,), F32)
                for r in range(_MEM_ROWS):
                    for j in range(_SC_PIECES):
                        o_vmem[r, pl.ds(j * SC_LANES, SC_LANES)] = jnp.zeros((SC_LANES,), F32)

            def one_head(h, carry):
                acc = [jnp.zeros((SC_LANES,), F32) for _ in range(SC_TOKENS)]
                for half in range(_MEM_HALVES):
                    r = half * MEM_HEADS + h
                    for j in range(_SC_PIECES):
                        sl = pl.ds(j * SC_LANES, SC_LANES)
                        qv = q_vmem[r, sl]
                        for t in range(SC_TOKENS):
                            acc[t] = acc[t] + k_vmem[t * _MEM_ROWS + r, sl] * qv
                s = jnp.zeros((SC_LANES,), F32)
                for t in range(SC_TOKENS):
                    s = jnp.where(lane == t, jnp.sum(acc[t]), s)

                m_old = m_ref[h, :]
                c_max = jnp.full((SC_LANES,), jnp.max(s), F32)
                m_new = jnp.where(first, c_max, jnp.maximum(m_old, c_max))
                alpha = jnp.where(first, 0.0, jnp.exp(m_old - m_new))
                p = jnp.exp(s - m_new)
                l_ref[h, :] = l_ref[h, :] * alpha + jnp.sum(p)
                m_ref[h, :] = m_new

                pt = [jnp.full((SC_LANES,), p[t], F32) for t in range(SC_TOKENS)]
                for half in range(_MEM_HALVES):
                    r = half * MEM_HEADS + h
                    for j in range(_SC_PIECES):
                        sl = pl.ds(j * SC_LANES, SC_LANES)
                        o = o_vmem[r, sl] * alpha
                        for t in range(SC_TOKENS):
                            o = o + pt[t] * v_vmem[t * _MEM_ROWS + r, sl]
                        o_vmem[r, sl] = o
                return carry

            lax.fori_loop(0, MEM_HEADS, one_head, 0)

            @pl.when(chunk == _SC_CHUNKS - 1)
            def _():
                for h in range(MEM_HEADS):
                    inv = 1.0 / l_ref[h, :]
                    for half in range(_MEM_HALVES):
                        r = half * MEM_HEADS + h
                        for j in range(_SC_PIECES):
                            sl = pl.ds(j * SC_LANES, SC_LANES)
                            o_vmem[r, sl] = o_vmem[r, sl] * inv

        pltpu.emit_pipeline(
            body,
            grid=(nseq, _SC_CHUNKS),
            in_specs=[pl.BlockSpec((_MEM_ROWS, LANES), lambda s, c: (seq0 + s, 0)),
                      pl.BlockSpec((_SC_ROWS, LANES), lambda s, c: (chunk0 + s * _SC_CHUNKS + c, 0)),
                      pl.BlockSpec((_SC_ROWS, LANES), lambda s, c: (chunk0 + s * _SC_CHUNKS + c, 0))],
            out_specs=[pl.BlockSpec((_MEM_ROWS, LANES), lambda s, c: (s, 0))],
            core_axis_name=("c", "s"),
            dimension_semantics=(pltpu.PARALLEL, pltpu.ARBITRARY),
            _explicit_indices=True,
        )(q_hbm, k_hbm, v_hbm, o_hbm)

    return attend(q8, kflat, vflat)


TM_PROMPT = 1024
BB_MIX_SAMPLE = 32


def kernel(x_prompt, x_sample, mem_prompt, cache_win_k, cache_win_v, state_conv, cache_mem_k, cache_mem_v,
           ln_g, ln_b, ffn1_w_gu, ffn1_w_down, w_in, conv_w, attn_sinks, w_out,
           w_cq, w_mk, w_mv, w_co, ffn2_w_gu, ffn2_w_down):
    bsz, seq, _ = x_prompt.shape
    nsmp = x_sample.shape[0]
    yp = x_prompt
    ys = x_sample.reshape(nsmp, D_MODEL)
    mem2 = mem_prompt.reshape(bsz * MEM_LEN, D_MODEL)

    stacked = dict(ffn1=(ffn1_w_gu, ffn1_w_down), ffn2=(ffn2_w_gu, ffn2_w_down),
                   mix=(w_in, w_out), cross=(w_cq, w_co, w_mk, w_mv))
    first_tile, wg0, wu0, wd0 = _ffn_first_tile(x_prompt.reshape(bsz * seq, D_MODEL), ffn1_w_gu, ffn1_w_down,
                                                ln_g, ln_b, 0, 0, TM_PROMPT)
    wb = {("ffn1", 0): ((wg0, wu0), wd0)}

    def jobs(*groups):
        return [(w, layer) for name, layer in groups for w in stacked[name]]

    def keep(groups, casts):
        casts = list(casts)
        for name, layer in groups:
            wb[(name, layer)] = tuple(casts[:len(stacked[name])])
            del casts[:len(stacked[name])]

    def token_tiles(c):
        c = c.reshape(DEPTH, nsmp, MEM_LEN, MEM_HEADS, _MEM_HALVES, LANES)
        return c.transpose(0, 1, 2, 4, 3, 5).reshape(DEPTH, nsmp, MEM_LEN * _MEM_ROWS, LANES)

    mem_k_tiles, mem_v_tiles = token_tiles(cache_mem_k), token_tiles(cache_mem_v)

    def window_lanes(c):
        return c.transpose(0, 1, 3, 4, 2).reshape(DEPTH, nsmp, D_KV, WINDOW)

    win_k_lanes, win_v_lanes = window_lanes(cache_win_k), window_lanes(cache_win_v)
    new_windows = None
    mem_tiles = None

    wkp, wvp, cvp, cvs = [], [], [], []
    for l in range(DEPTH):
        more = l + 1 < DEPTH
        side = [("mix", 0), ("cross", 0)] if l == 0 else [("ffn2", l)]
        yp, ys, *casts = _ffn(yp.reshape(bsz * seq, D_MODEL), ys, *wb[("ffn1", l)], ln_g, ln_b, l, 0, TM_PROMPT,
                              jobs(*side), first_tile=first_tile if l == 0 else None)
        keep(side, casts)
        yp = yp.reshape(bsz, seq, D_MODEL)

        win, wout = wb[("mix", l)]
        side = [("ffn2", 0)] if l == 0 else []
        yp, nk, nv, ncv, *casts = _mix_prompt(yp, attn_sinks, win, conv_w, wout, ln_g, ln_b, l, 1, TM_PROMPT,
                                              jobs(*side))
        keep(side, casts)
        wkp.append(nk.reshape(bsz, WINDOW, N_KV_HEADS, HEAD_DIM))
        wvp.append(nv.reshape(bsz, WINDOW, N_KV_HEADS, HEAD_DIM))
        cvp.append(ncv)
        wcq, wco, wmk, wmv = wb[("cross", l)]
        ys, nst, nks, nvs, q8 = _mix_sample(
            ys, attn_sinks, state_conv[l].reshape(nsmp, (CONV_W - 1) * D_CONV),
            win_k_lanes, win_v_lanes, new_windows, win, conv_w, wout, wcq, ln_g, ln_b, l, 1, BB_MIX_SAMPLE)
        new_windows = (nks, nvs)
        cvs.append(nst.reshape(nsmp, CONV_W - 1, D_CONV))

        o8 = _sc_cross_attention(q8.reshape(nsmp * _MEM_ROWS, LANES), mem_k_tiles, mem_v_tiles, l, 0, nsmp)
        mk_all, mv_all, mkb, mvb = _memkv(mem2, wmk, wmv, l, mem_tiles)
        mem_tiles = (mk_all, mv_all)
        side = [("ffn1", l + 1)] if more else []
        yp, *casts = _cross_prompt(yp, mkb.reshape(bsz, MEM_LEN, D_MODEL), mvb.reshape(bsz, MEM_LEN, D_MODEL),
                                   wcq, wco, ln_g, ln_b, l, 2, TM_PROMPT, jobs(*side))
        keep(side, casts)

        side = [("mix", l + 1), ("cross", l + 1)] if more else []
        yp, ys, *casts = _ffn(yp.reshape(bsz * seq, D_MODEL), ys, *wb[("ffn2", l)], ln_g, ln_b, l, 3, TM_PROMPT,
                              jobs(*side), sample_cross=(o8.reshape(nsmp, _MEM_ROWS, LANES), wco, 2))
        keep(side, casts)
        yp = yp.reshape(bsz, seq, D_MODEL)

    def window_rows(c):
        return c.reshape(DEPTH, nsmp, N_KV_HEADS, HEAD_DIM, WINDOW).transpose(0, 1, 4, 2, 3)

    def token_rows(c):
        c = c.reshape(DEPTH, bsz, MEM_LEN, _MEM_HALVES, MEM_HEADS, LANES)
        return c.transpose(0, 1, 2, 4, 3, 5).reshape(DEPTH, bsz, MEM_LEN, MEM_HEADS, MEM_HEAD_DIM)

    return (yp, ys.reshape(nsmp, 1, D_MODEL),
            jnp.stack(wkp), jnp.stack(wvp), jnp.stack(cvp), token_rows(mem_tiles[0]), token_rows(mem_tiles[1]),
            window_rows(new_windows[0]), window_rows(new_windows[1]), jnp.stack(cvs))
```

```python
import functools

import jax
import jax.numpy as jnp
from jax import lax
from jax.experimental import pallas as pl
from jax.experimental.pallas import tpu as pltpu
from jax.experimental.pallas import tpu_sc as plsc

D_MODEL = 1024
DEPTH = 2
D_CONV = 512
CONV_W = 3
HEAD_DIM = 64
N_HEADS = 8
N_KV_HEADS = 2
GROUP = N_HEADS // N_KV_HEADS
WINDOW = 128
D_ATTN = N_HEADS * HEAD_DIM
D_KV = N_KV_HEADS * HEAD_DIM
IN_COLS = 3 * D_CONV + D_ATTN + 2 * D_KV
MEM_LEN = 256
MEM_HEADS = 4
MEM_HEAD_DIM = D_MODEL // MEM_HEADS
D_FF = 2816
N_LN = 4
ALPHA = (2.0 * DEPTH) ** 0.25
LN_EPS = 1e-5
LOG2E = 1.4426950408889634

_C_BG, _C_CG, _C_HC, _C_Q, _C_KV = 0, D_CONV, 2 * D_CONV, 3 * D_CONV, 3 * D_CONV + D_ATTN

LANES = 128
SUBLANES = 8
MXU_COLS = 256
VMEM_LIMIT_BYTES = 56 * 1024 * 1024

BF16 = jnp.bfloat16
F32 = jnp.float32


def _dot(a, b):
    return jnp.dot(a, b, preferred_element_type=F32)


def _dot_nt(a, b):
    return lax.dot_general(a, b, (((1,), (1,)), ((), ())), preferred_element_type=F32)


def _layer_norm(z, g, b):
    mu = jnp.mean(z, axis=-1, keepdims=True)
    d = z - mu
    var = jnp.mean(d * d, axis=-1, keepdims=True)
    return d * lax.rsqrt(var + LN_EPS) * g + b


def _resident(shape):
    zeros = (0,) * len(shape)
    return pl.BlockSpec(shape, lambda *_: zeros, pipeline_mode=pl.Buffered(1))


def _weight(shape):
    return _resident((1,) + shape)


def _cast_specs(jobs, n_steps, step_of):
    in_specs, out_specs, out_shapes, args = [], [], [], []
    for w, layer in jobs:
        _, k, n = w.shape
        rows = k // n_steps
        assert rows * n_steps == k and rows % (2 * SUBLANES) == 0, (w.shape, n_steps)
        in_specs.append(pl.BlockSpec((1, rows, n), lambda *idx, layer=layer: (layer, step_of(*idx), 0)))
        out_specs.append(pl.BlockSpec((1, rows, n), lambda *idx: (0, step_of(*idx), 0)))
        out_shapes.append(jax.ShapeDtypeStruct((1, k, n), BF16))
        args.append(w)
    return in_specs, out_specs, out_shapes, args


def _run_casts(src_refs, dst_refs):
    for src, dst in zip(src_refs, dst_refs, strict=True):
        dst[...] = src[...].astype(BF16)


def _ln_specs():
    return [_resident((DEPTH, N_LN, D_MODEL)), _resident((DEPTH, N_LN, D_MODEL))]


def _params(*sem):
    return pltpu.CompilerParams(dimension_semantics=sem, vmem_limit_bytes=VMEM_LIMIT_BYTES)


NORM_ROWS = 256


def _project_and_norm(lhs_ref, w, x_rows, scale, g, b, write_rows, rows):
    for r0 in range(0, rows, min(NORM_ROWS, rows)):
        r1 = r0 + min(NORM_ROWS, rows)
        y = _dot(lhs_ref[r0:r1, :], w)
        write_rows(r0, r1, _layer_norm(ALPHA * x_rows(r0, r1) + scale * y, g, b))


def _swiglu(gate, up):
    return (gate * jax.nn.sigmoid(gate) * up).astype(BF16)


def _ffn_rows(x_ref, o_ref, wg_ref, wu_ref, wd_ref, h_ref, g, b):
    rows = x_ref.shape[0]
    xb = x_ref[...].astype(BF16)
    for c in range(D_FF // MXU_COLS):
        lo = c * MXU_COLS
        h_ref[0:rows, lo:lo + MXU_COLS] = _swiglu(_dot(xb, wg_ref[0, :, lo:lo + MXU_COLS]),
                                                  _dot(xb, wu_ref[0, :, lo:lo + MXU_COLS]))

    def write_rows(r0, r1, y):
        o_ref[r0:r1, :] = y

    _project_and_norm(h_ref, wd_ref[0], lambda r0, r1: x_ref[r0:r1, :], 0.5, g, b, write_rows, rows)


def _ffn_kernel(x_ref, xs_ref, wg_ref, wu_ref, wd_ref, g_ref, b_ref, *rest,
                layer, ln, n_cast, n_tiles, cross_ln, has_first):
    n_pre = (0 if cross_ln is None else 2) + (1 if has_first else 0)
    pre, rest = rest[:n_pre], rest[n_pre:]
    cast_src, (o_ref, os_ref), rest = rest[:n_cast], rest[n_cast:n_cast + 2], rest[n_cast + 2:]
    cast_dst, scratch = rest[:n_cast], rest[n_cast:]
    h_ref = scratch[0]
    g = g_ref[layer, ln:ln + 1, :]
    b = b_ref[layer, ln:ln + 1, :]
    step = pl.program_id(0)

    if has_first:
        @pl.when(step == 0)
        def _():
            _run_casts(cast_src, cast_dst)
            o_ref[...] = pre[-1][...]

    @pl.when((step < n_tiles) & (step > 0) if has_first else step < n_tiles)
    def _():
        _run_casts(cast_src, cast_dst)
        _ffn_rows(x_ref, o_ref, wg_ref, wu_ref, wd_ref, h_ref, g, b)

    @pl.when(step == n_tiles)
    def _():
        if cross_ln is None:
            _ffn_rows(xs_ref, os_ref, wg_ref, wu_ref, wd_ref, h_ref, g, b)
        else:
            o8_ref, wo_ref = pre[:2]
            obuf, xs_scr = scratch[1:]
            for row, col in _head_piece_cols():
                obuf[:, col:col + LANES] = o8_ref[:, row, :].astype(BF16)
            xs_scr[...] = _layer_norm(ALPHA * xs_ref[...] + _dot(obuf[...], wo_ref[0]),
                                      g_ref[layer, cross_ln:cross_ln + 1, :], b_ref[layer, cross_ln:cross_ln + 1, :])
            _ffn_rows(xs_scr, os_ref, wg_ref, wu_ref, wd_ref, h_ref, g, b)


def _ffn(x, xs, wgu, wd, g, b, layer, ln, tm, casts=(), sample_cross=None, first_tile=None):
    m = x.shape[0]
    ns = xs.shape[0]
    n_tiles = m // tm
    tile = lambda i: (jnp.minimum(i, n_tiles - 1), 0)
    c_in, c_out, c_shapes, c_args = _cast_specs(casts, n_tiles, lambda i: jnp.minimum(i, n_tiles - 1))
    half = lambda j: pl.BlockSpec((1, D_MODEL, D_FF), lambda i: (0, 0, j), pipeline_mode=pl.Buffered(1))
    if isinstance(wgu, tuple):
        w_args, w_specs = list(wgu), [half(0), half(0)]
    else:
        w_args, w_specs = [wgu, wgu], [half(0), half(1)]
    pre_specs, pre_args, pre_scratch, cross_ln = [], [], [], None
    if sample_cross is not None:
        o8, wo, cross_ln = sample_cross
        pre_specs = [_resident((ns, _MEM_ROWS, LANES)), _weight((D_MODEL, D_MODEL))]
        pre_args = [o8, wo]
        pre_scratch = [pltpu.VMEM((ns, D_MODEL), BF16), pltpu.VMEM((ns, D_MODEL), F32)]
    if first_tile is not None:
        pre_specs.append(_resident((tm, D_MODEL)))
        pre_args.append(first_tile)
    return pl.pallas_call(
        functools.partial(_ffn_kernel, layer=layer, ln=ln, n_cast=len(casts), n_tiles=n_tiles, cross_ln=cross_ln,
                          has_first=first_tile is not None),
        grid=(n_tiles + 1,),
        in_specs=[pl.BlockSpec((tm, D_MODEL), tile), _resident((ns, D_MODEL))] + w_specs
        + [_weight((D_FF, D_MODEL))] + _ln_specs() + pre_specs + c_in,
        out_specs=[pl.BlockSpec((tm, D_MODEL), tile), pl.BlockSpec((ns, D_MODEL), lambda i: (0, 0))] + c_out,
        out_shape=[jax.ShapeDtypeStruct((m, D_MODEL), F32), jax.ShapeDtypeStruct((ns, D_MODEL), F32)] + c_shapes,
        scratch_shapes=[pltpu.VMEM((tm, D_FF), BF16)] + pre_scratch,
        compiler_params=_params("arbitrary"),
        name="ffn_ln",
    )(x, xs, *w_args, wd, g, b, *pre_args, *c_args)


def _ffn_first_tile_kernel(x_ref, wg_ref, wu_ref, wd_ref, g_ref, b_ref, o_ref, wgb_ref, wub_ref, wdb_ref,
                           xb_scr, h_scr, wd_scr, *, layer, ln):
    c = pl.program_id(0)
    n_chunks = h_scr.shape[0]

    @pl.when(c == 0)
    def _():
        xb_scr[...] = x_ref[...].astype(BF16)

    wg = wg_ref[0].astype(BF16)
    wu = wu_ref[0].astype(BF16)
    wd = wd_ref[0].astype(BF16)
    wgb_ref[0] = wg
    wub_ref[0] = wu
    wdb_ref[0] = wd
    wd_scr[c] = wd
    xb = xb_scr[...]
    h_scr[c] = _swiglu(_dot(xb, wg), _dot(xb, wu))

    @pl.when(c == n_chunks - 1)
    def _():
        rows = x_ref.shape[0]
        g = g_ref[layer, ln:ln + 1, :]
        b = b_ref[layer, ln:ln + 1, :]
        for r0 in range(0, rows, NORM_ROWS):
            r1 = r0 + NORM_ROWS
            y = _dot(h_scr[0, r0:r1, :], wd_scr[0])
            for k in range(1, n_chunks):
                y = y + _dot(h_scr[k, r0:r1, :], wd_scr[k])
            o_ref[r0:r1, :] = _layer_norm(ALPHA * x_ref[r0:r1, :] + 0.5 * y, g, b)


def _ffn_first_tile(x, wgu32, wd32, g, b, layer, ln, tm):
    n_chunks = D_FF // MXU_COLS
    return pl.pallas_call(
        functools.partial(_ffn_first_tile_kernel, layer=layer, ln=ln),
        grid=(n_chunks,),
        in_specs=[_resident((tm, D_MODEL)),
                  pl.BlockSpec((1, D_MODEL, MXU_COLS), lambda c: (layer, 0, c)),
                  pl.BlockSpec((1, D_MODEL, MXU_COLS), lambda c: (layer, 0, n_chunks + c)),
                  pl.BlockSpec((1, MXU_COLS, D_MODEL), lambda c: (layer, c, 0))] + _ln_specs(),
        out_specs=[pl.BlockSpec((tm, D_MODEL), lambda c: (0, 0)),
                   pl.BlockSpec((1, D_MODEL, MXU_COLS), lambda c: (0, 0, c)),
                   pl.BlockSpec((1, D_MODEL, MXU_COLS), lambda c: (0, 0, c)),
                   pl.BlockSpec((1, MXU_COLS, D_MODEL), lambda c: (0, c, 0))],
        out_shape=[jax.ShapeDtypeStruct((tm, D_MODEL), F32),
                   jax.ShapeDtypeStruct((1, D_MODEL, D_FF), BF16),
                   jax.ShapeDtypeStruct((1, D_MODEL, D_FF), BF16),
                   jax.ShapeDtypeStruct((1, D_FF, D_MODEL), BF16)],
        scratch_shapes=[pltpu.VMEM((tm, D_MODEL), BF16),
                        pltpu.VMEM((n_chunks, tm, MXU_COLS), BF16),
                        pltpu.VMEM((n_chunks, MXU_COLS, D_MODEL), BF16)],
        compiler_params=_params("arbitrary"),
        name="ffn_first_tile",
    )(x, wgu32, wgu32, wd32, g, b)


def _dup_halves(a):
    lane = lax.broadcasted_iota(jnp.int32, a.shape, 1)
    low = lane < HEAD_DIM
    rolled = pltpu.roll(a, HEAD_DIM, 1)
    return jnp.where(low, a, rolled), jnp.where(low, rolled, a)


def _sink_column(sink_ref, layer, first_head, rows_per_head, n_heads):
    rows = n_heads * rows_per_head
    r = lax.broadcasted_iota(jnp.int32, (rows, 1), 0)
    col = jnp.full((rows, 1), sink_ref[layer, first_head + n_heads - 1], F32)
    for g in range(n_heads - 2, -1, -1):
        col = jnp.where(r < (g + 1) * rows_per_head, sink_ref[layer, first_head + g], col)
    return col


def _mix_prompt_kernel(sink_ref, x_ref, win_ref, cw_ref, wout_ref, g_ref, b_ref, *rest, tm, layer, ln, n_cast):
    cast_src, rest = rest[:n_cast], rest[n_cast:]
    (o_ref, nk_ref, nv_ref, nc_ref), rest = rest[:4], rest[4:]
    cast_dst, (ubuf, qbuf, kd0, kd1, vd0, vd1, zbuf) = rest[:n_cast], rest[n_cast:]
    seq_start = pl.program_id(1) == 0

    @pl.when(seq_start)
    def _():
        ubuf[0:SUBLANES, :] = jnp.zeros((SUBLANES, D_CONV), F32)
        for r in (kd0, kd1, vd0, vd1):
            r[0:WINDOW, :] = jnp.zeros((WINDOW, r.shape[1]), BF16)

    _run_casts(cast_src, cast_dst)
    xb = x_ref[0].astype(BF16)

    def proj(col, width):
        return _dot(xb, win_ref[0, :, col:col + width])

    u = proj(_C_CG, D_CONV) * proj(_C_HC, D_CONV)
    ubuf[SUBLANES:SUBLANES + tm, :] = u
    conv = (cw_ref[layer, 0:1, :] * ubuf[SUBLANES - 2:SUBLANES - 2 + tm, :]
            + cw_ref[layer, 1:2, :] * ubuf[SUBLANES - 1:SUBLANES - 1 + tm, :]
            + cw_ref[layer, 2:3, :] * u)
    zbuf[:, 0:D_CONV] = (proj(_C_BG, D_CONV) * conv).astype(BF16)
    nc_ref[0] = u[tm - (CONV_W - 1):tm, :]
    ubuf[0:SUBLANES, :] = u[tm - SUBLANES:tm, :]

    qbuf[...] = (proj(_C_Q, D_ATTN) * (HEAD_DIM ** -0.5 * LOG2E)).astype(BF16)
    k = proj(_C_KV, D_KV)
    v = proj(_C_KV + D_KV, D_KV)
    nk_ref[0] = k[tm - WINDOW:tm, :]
    nv_ref[0] = v[tm - WINDOW:tm, :]
    ka, kb = _dup_halves(k)
    va, vb = _dup_halves(v)
    kd0[WINDOW:WINDOW + tm, :] = ka.astype(BF16)
    kd1[WINDOW:WINDOW + tm, :] = kb.astype(BF16)
    ones = jnp.ones((tm, LANES), BF16)
    vd0[WINDOW:WINDOW + tm, :] = jnp.concatenate([va.astype(BF16), ones], axis=1)
    vd1[WINDOW:WINDOW + tm, :] = jnp.concatenate([vb.astype(BF16), ones], axis=1)

    rows = GROUP * WINDOW
    a_idx = lax.broadcasted_iota(jnp.int32, (rows, 2 * WINDOW), 0) % WINDOW
    c_idx = lax.broadcasted_iota(jnp.int32, (rows, 2 * WINDOW), 1)
    band = (c_idx >= a_idx) & (c_idx <= a_idx + WINDOW)
    first_lb = jnp.where(seq_start, WINDOW, 0)
    low = lax.broadcasted_iota(jnp.int32, (WINDOW, LANES), 1) < HEAD_DIM
    zero = jnp.zeros((WINDOW, LANES), BF16)

    for n in range(tm // WINDOW):
        r0 = n * WINDOW
        mask = (band & (c_idx >= first_lb)) if n == 0 else band
        for j, (kd, vd) in enumerate(((kd0, vd0), (kd1, vd1))):
            qa = qbuf[r0:r0 + WINDOW, (2 * j) * LANES:(2 * j + 1) * LANES]
            qb = qbuf[r0:r0 + WINDOW, (2 * j + 1) * LANES:(2 * j + 2) * LANES]
            qs = jnp.concatenate([jnp.where(low, qa, zero), jnp.where(low, zero, qa),
                                  jnp.where(low, qb, zero), jnp.where(low, zero, qb)], axis=0)
            s = _dot_nt(qs, kd[r0:r0 + 2 * WINDOW, :])
            s = jnp.where(mask, s, -jnp.inf)
            sink = _sink_column(sink_ref, layer, GROUP * j, WINDOW, GROUP) * LOG2E
            m = jnp.max(s, axis=-1, keepdims=True)
            p = jnp.exp2(s - m)
            ov = _dot(p.astype(BF16), vd[r0:r0 + 2 * WINDOW, :])
            o = ov[:, 0:LANES] / (ov[:, LANES:2 * LANES] + jnp.exp2(sink - m))
            za = jnp.where(low, o[0:WINDOW], o[WINDOW:2 * WINDOW])
            zb = jnp.where(low, o[2 * WINDOW:3 * WINDOW], o[3 * WINDOW:4 * WINDOW])
            c0 = D_CONV + (2 * j) * LANES
            zbuf[r0:r0 + WINDOW, c0:c0 + LANES] = za.astype(BF16)
            zbuf[r0:r0 + WINDOW, c0 + LANES:c0 + 2 * LANES] = zb.astype(BF16)

    for r in (kd0, kd1, vd0, vd1):
        r[0:WINDOW, :] = r[tm:tm + WINDOW, :]

    def write_rows(r0, r1, y):
        o_ref[0, r0:r1, :] = y

    _project_and_norm(zbuf, wout_ref[0], lambda r0, r1: x_ref[0, r0:r1, :], 1.0,
                      g_ref[layer, ln:ln + 1, :], b_ref[layer, ln:ln + 1, :], write_rows, tm)


def _mix_prompt(x, sinks, win, cw, wout, g, b, layer, ln, tm, casts=()):
    bsz, seq, _ = x.shape
    tps = seq // tm
    kern = functools.partial(_mix_prompt_kernel, tm=tm, layer=layer, ln=ln, n_cast=len(casts))
    per_seq = lambda bi, i, s: (bi, 0, 0)
    c_in, c_out, c_shapes, c_args = _cast_specs(casts, bsz * tps, lambda bi, i, *_: bi * tps + i)
    grid_spec = pltpu.PrefetchScalarGridSpec(
        num_scalar_prefetch=1,
        grid=(bsz, tps),
        in_specs=[
            pl.BlockSpec((1, tm, D_MODEL), lambda bi, i, s: (bi, i, 0)),
            _weight((D_MODEL, IN_COLS)),
            _resident((DEPTH, CONV_W, D_CONV)),
            _weight((D_CONV + D_ATTN, D_MODEL)),
        ] + _ln_specs() + c_in,
        out_specs=[
            pl.BlockSpec((1, tm, D_MODEL), lambda bi, i, s: (bi, i, 0)),
            pl.BlockSpec((1, WINDOW, D_KV), per_seq),
            pl.BlockSpec((1, WINDOW, D_KV), per_seq),
            pl.BlockSpec((1, CONV_W - 1, D_CONV), per_seq),
        ] + c_out,
        scratch_shapes=[
            pltpu.VMEM((tm + SUBLANES, D_CONV), F32),
            pltpu.VMEM((tm, D_ATTN), BF16),
            pltpu.VMEM((tm + WINDOW, LANES), BF16),
            pltpu.VMEM((tm + WINDOW, LANES), BF16),
            pltpu.VMEM((tm + WINDOW, 2 * LANES), BF16),
            pltpu.VMEM((tm + WINDOW, 2 * LANES), BF16),
            pltpu.VMEM((tm, D_CONV + D_ATTN), BF16),
        ],
    )
    return pl.pallas_call(
        kern,
        grid_spec=grid_spec,
        out_shape=[
            jax.ShapeDtypeStruct((bsz, seq, D_MODEL), F32),
            jax.ShapeDtypeStruct((bsz, WINDOW, D_KV), F32),
            jax.ShapeDtypeStruct((bsz, WINDOW, D_KV), F32),
            jax.ShapeDtypeStruct((bsz, CONV_W - 1, D_CONV), F32),
        ] + c_shapes,
        compiler_params=_params("arbitrary", "arbitrary"),
        name="mix_prompt",
    )(sinks, x, win, cw, wout, g, b, *c_args)


def _memkv_kernel(m_ref, wk_ref, wv_ref, *rest, layer):
    k_ref, v_ref, kb_ref, vb_ref = rest[-4:]
    mb = m_ref[...].astype(BF16)
    k = _dot(mb, wk_ref[0])
    v = _dot(mb, wv_ref[0])
    kb_ref[...] = k.astype(BF16)
    vb_ref[...] = v.astype(BF16)
    slab = layer if k_ref.shape[0] > 1 else 0
    for other in range(k_ref.shape[0]):
        if other != slab:
            k_ref[other] = jnp.zeros(k_ref.shape[1:], F32)
            v_ref[other] = jnp.zeros(v_ref.shape[1:], F32)
    for row, col in _head_piece_cols():
        k_ref[slab, 0, :, row, :] = k[:, col:col + LANES]
        v_ref[slab, 0, :, row, :] = v[:, col:col + LANES]


def _memkv(mem, wk, wv, layer, prev):
    m = mem.shape[0]
    nb = m // MEM_LEN
    blk = pl.BlockSpec((MEM_LEN, D_MODEL), lambda i: (i, 0))
    in_specs = [blk, _weight((D_MODEL, D_MODEL)), _weight((D_MODEL, D_MODEL))]
    args = [mem, wk, wv]
    aliases = {}
    if prev is None:
        tile_blk = pl.BlockSpec((DEPTH, 1, MEM_LEN, _MEM_ROWS, LANES), lambda i: (0, i, 0, 0, 0))
    else:
        tile_blk = pl.BlockSpec((1, 1, MEM_LEN, _MEM_ROWS, LANES), lambda i: (layer, i, 0, 0, 0))
        in_specs += [pl.BlockSpec(memory_space=pl.ANY)] * 2
        args += list(prev)
        aliases = {3: 0, 4: 1}
    tiles = jax.ShapeDtypeStruct((DEPTH, nb, MEM_LEN, _MEM_ROWS, LANES), F32)
    return pl.pallas_call(
        functools.partial(_memkv_kernel, layer=layer),
        grid=(nb,),
        in_specs=in_specs,
        out_specs=[tile_blk, tile_blk, blk, blk],
        out_shape=[tiles, tiles, jax.ShapeDtypeStruct((m, D_MODEL), BF16), jax.ShapeDtypeStruct((m, D_MODEL), BF16)],
        input_output_aliases=aliases,
        compiler_params=_params("arbitrary"),
        name="mem_kv",
    )(*args)


def _cross_prompt_kernel(x_ref, mk_ref, mv_ref, wq_ref, wo_ref, g_ref, b_ref, *rest, layer, ln, n_cast):
    cast_src, o_ref, cast_dst, obuf = rest[:n_cast], rest[n_cast], rest[n_cast + 1:2 * n_cast + 1], rest[-1]
    _run_casts(cast_src, cast_dst)
    q = _dot(x_ref[0].astype(BF16), wq_ref[0]) * (MEM_HEAD_DIM ** -0.5 * LOG2E)
    for h in range(MEM_HEADS):
        c0 = h * MEM_HEAD_DIM
        s = _dot_nt(q[:, c0:c0 + MEM_HEAD_DIM].astype(BF16), mk_ref[0, :, c0:c0 + MEM_HEAD_DIM])
        m = jnp.max(s, axis=-1, keepdims=True)
        p = jnp.exp2(s - m)
        denom = jnp.sum(p, axis=-1, keepdims=True)
        o = _dot(p.astype(BF16), mv_ref[0, :, c0:c0 + MEM_HEAD_DIM]) / denom
        obuf[:, c0:c0 + MEM_HEAD_DIM] = o.astype(BF16)

    def write_rows(r0, r1, y):
        o_ref[0, r0:r1, :] = y

    _project_and_norm(obuf, wo_ref[0], lambda r0, r1: x_ref[0, r0:r1, :], 1.0,
                      g_ref[layer, ln:ln + 1, :], b_ref[layer, ln:ln + 1, :], write_rows, x_ref.shape[1])


def _cross_prompt(x, mk, mv, wq, wo, g, b, layer, ln, tm, casts=()):
    bsz, seq, _ = x.shape
    tps = seq // tm
    xblk = pl.BlockSpec((1, tm, D_MODEL), lambda bi, i: (bi, i, 0))
    mblk = pl.BlockSpec((1, MEM_LEN, D_MODEL), lambda bi, i: (bi, 0, 0))
    c_in, c_out, c_shapes, c_args = _cast_specs(casts, bsz * tps, lambda bi, i: bi * tps + i)
    return pl.pallas_call(
        functools.partial(_cross_prompt_kernel, layer=layer, ln=ln, n_cast=len(casts)),
        grid=(bsz, tps),
        in_specs=[xblk, mblk, mblk, _weight((D_MODEL, D_MODEL)), _weight((D_MODEL, D_MODEL))] + _ln_specs() + c_in,
        out_specs=[xblk] + c_out,
        out_shape=[jax.ShapeDtypeStruct((bsz, seq, D_MODEL), F32)] + c_shapes,
        scratch_shapes=[pltpu.VMEM((tm, D_MODEL), BF16)],
        compiler_params=_params("arbitrary", "arbitrary"),
        name="cross_prompt",
    )(x, mk, mv, wq, wo, g, b, *c_args)


def _bdot(a, b, contract_b):
    return lax.dot_general(a, b, (((2,), (contract_b,)), ((0,), (0,))), preferred_element_type=F32)


def _mix_sample_kernel(sink_ref, x_ref, st_ref, kc_ref, vc_ref, win_ref, cw_ref, wout_ref, wq_ref, g_ref, b_ref,
                       *rest, bb, layer, ln):
    o_ref, nst_ref, nk_ref, nv_ref, q8_ref, zbuf, qm_scr, kn_scr, vn_scr, o_scr = rest[-10:]
    x = x_ref[...]
    xb = x.astype(BF16)

    def proj(col, width):
        return _dot(xb, win_ref[0, :, col:col + width])

    u = proj(_C_CG, D_CONV) * proj(_C_HC, D_CONV)
    st0 = st_ref[:, 0:D_CONV]
    st1 = st_ref[:, D_CONV:2 * D_CONV]
    conv = cw_ref[layer, 0:1, :] * st0 + cw_ref[layer, 1:2, :] * st1 + cw_ref[layer, 2:3, :] * u
    zbuf[:, 0:D_CONV] = (proj(_C_BG, D_CONV) * conv).astype(BF16)
    nst_ref[:, 0:D_CONV] = st1
    nst_ref[:, D_CONV:2 * D_CONV] = u

    q = proj(_C_Q, D_ATTN) * (HEAD_DIM ** -0.5)
    k = proj(_C_KV, D_KV)
    v = proj(_C_KV + D_KV, D_KV)

    pad = jnp.zeros((LANES - bb, D_KV), F32)
    kt = jnp.concatenate([k, pad], axis=0).T if bb < LANES else k.T
    vt = jnp.concatenate([v, pad], axis=0).T if bb < LANES else v.T
    last = lax.broadcasted_iota(jnp.int32, (D_KV, WINDOW), 1) == WINDOW - 1
    slab = layer if nk_ref.shape[0] > 1 else 0
    for other in range(nk_ref.shape[0]):
        if other != slab:
            nk_ref[other] = jnp.zeros(nk_ref.shape[1:], F32)
            nv_ref[other] = jnp.zeros(nv_ref.shape[1:], F32)
    for r in range(bb):
        nk_ref[slab, r] = jnp.where(last, kt[:, r:r + 1], pltpu.roll(kc_ref[0, r], WINDOW - 1, 1))
        nv_ref[slab, r] = jnp.where(last, vt[:, r:r + 1], pltpu.roll(vc_ref[0, r], WINDOW - 1, 1))

    lane = lax.broadcasted_iota(jnp.int32, (bb, LANES), 1)
    low = lane < HEAD_DIM
    for h in range(N_HEADS):
        kvh = h // GROUP
        qt = q[:, (h // 2) * LANES:(h // 2 + 1) * LANES]
        if (h % 2) != kvh:
            qt = pltpu.roll(qt, HEAD_DIM, 1)
        keep = low if kvh == 0 else jnp.logical_not(low)
        qm_scr[:, h, :] = jnp.where(keep, qt, 0.0)
        kn_scr[:, h, :] = k
        vn_scr[:, h, :] = v
    qm = qm_scr[...]

    s = _bdot(qm.astype(BF16), kc_ref[0].astype(BF16), 1)
    s_new = jnp.sum(qm * kn_scr[...], axis=-1, keepdims=True)
    hrow = lax.broadcasted_iota(jnp.int32, (1, N_HEADS, 1), 1)
    sink = jnp.full((1, N_HEADS, 1), sink_ref[layer, N_HEADS - 1], F32)
    for h in range(N_HEADS - 1):
        sink = jnp.where(hrow == h, sink_ref[layer, h], sink)
    m = jnp.maximum(jnp.maximum(jnp.max(s, axis=-1, keepdims=True), s_new), sink)
    p = jnp.exp(s - m)
    p_new = jnp.exp(s_new - m)
    denom = jnp.sum(p, axis=-1, keepdims=True) + p_new + jnp.exp(sink - m)
    o_scr[...] = (_bdot(p.astype(BF16), vc_ref[0].astype(BF16), 2) + p_new * vn_scr[...]) / denom

    for t in range(D_ATTN // LANES):
        oa = o_scr[:, 2 * t, :]
        ob = o_scr[:, 2 * t + 1, :]
        if (2 * t) // GROUP == 0:
            z = jnp.where(low, oa, pltpu.roll(ob, HEAD_DIM, 1))
        else:
            z = jnp.where(low, pltpu.roll(oa, HEAD_DIM, 1), ob)
        zbuf[:, D_CONV + t * LANES:D_CONV + (t + 1) * LANES] = z.astype(BF16)

    y = _dot(zbuf[...], wout_ref[0])
    out = _layer_norm(ALPHA * x + y, g_ref[layer, ln:ln + 1, :], b_ref[layer, ln:ln + 1, :])
    o_ref[...] = out
    q = _dot(out.astype(BF16), wq_ref[0]) * (MEM_HEAD_DIM ** -0.5)
    for row, col in _head_piece_cols():
        q8_ref[:, row, :] = q[:, col:col + LANES]


def _mix_sample(x, sinks, st, kc, vc, prev_windows, win, cw, wout, wq, g, b, layer, ln, bb):
    nb = x.shape[0]
    kern = functools.partial(_mix_sample_kernel, bb=bb, layer=layer, ln=ln)
    row = lambda i, s: (i, 0)
    win_blk = pl.BlockSpec((1, bb, D_KV, WINDOW), lambda i, s: (layer, i, 0, 0))
    in_specs = [
        pl.BlockSpec((bb, D_MODEL), row),
        pl.BlockSpec((bb, (CONV_W - 1) * D_CONV), row),
        win_blk,
        win_blk,
        _weight((D_MODEL, IN_COLS)),
        _resident((DEPTH, CONV_W, D_CONV)),
        _weight((D_CONV + D_ATTN, D_MODEL)),
        _weight((D_MODEL, D_MODEL)),
    ] + _ln_specs()
    args = [sinks, x, st, kc, vc, win, cw, wout, wq, g, b]
    aliases = {}
    out_win_blk = win_blk
    if prev_windows is None:
        out_win_blk = pl.BlockSpec((DEPTH, bb, D_KV, WINDOW), lambda i, s: (0, i, 0, 0))
    else:
        n_in = len(args)
        in_specs += [pl.BlockSpec(memory_space=pl.ANY)] * 2
        args += list(prev_windows)
        aliases = {n_in: 2, n_in + 1: 3}
    grid_spec = pltpu.PrefetchScalarGridSpec(
        num_scalar_prefetch=1,
        grid=(nb // bb,),
        in_specs=in_specs,
        out_specs=[pl.BlockSpec((bb, D_MODEL), row), pl.BlockSpec((bb, (CONV_W - 1) * D_CONV), row),
                   out_win_blk, out_win_blk, pl.BlockSpec((bb, _MEM_ROWS, LANES), lambda i, s: (i, 0, 0))],
        scratch_shapes=[pltpu.VMEM((bb, D_CONV + D_ATTN), BF16)]
        + [pltpu.VMEM((bb, N_HEADS, LANES), F32)] * 4,
    )
    return pl.pallas_call(
        kern,
        grid_spec=grid_spec,
        out_shape=[
            jax.ShapeDtypeStruct((nb, D_MODEL), F32),
            jax.ShapeDtypeStruct((nb, (CONV_W - 1) * D_CONV), F32),
            jax.ShapeDtypeStruct((DEPTH, nb, D_KV, WINDOW), F32),
            jax.ShapeDtypeStruct((DEPTH, nb, D_KV, WINDOW), F32),
            jax.ShapeDtypeStruct((nb, _MEM_ROWS, LANES), F32),
        ],
        input_output_aliases=aliases,
        compiler_params=_params("arbitrary"),
        name="mix_sample",
    )(*args)


_MEM_HALVES = MEM_HEAD_DIM // LANES
_MEM_ROWS = _MEM_HALVES * MEM_HEADS


def _head_piece_cols():
    return [(half * MEM_HEADS + h, h * MEM_HEAD_DIM + half * LANES)
            for half in range(_MEM_HALVES) for h in range(MEM_HEADS)]


SC_LANES = 16
SC_TOKENS = 16
_SC_ROWS = SC_TOKENS * _MEM_ROWS
_SC_CHUNKS = MEM_LEN // SC_TOKENS
_SC_PIECES = LANES // SC_LANES


def _sc_cross_attention(q8, mk_tiles, mv_tiles, layer, seq0, nseq):
    total_seq = mk_tiles.shape[1]
    kflat = mk_tiles.reshape(-1, LANES)
    vflat = mv_tiles.reshape(-1, LANES)
    chunk0 = (layer * total_seq + seq0) * _SC_CHUNKS
    mesh = plsc.VectorSubcoreMesh(core_axis_name="c", subcore_axis_name="s")

    @pl.kernel(out_type=jax.ShapeDtypeStruct((nseq * _MEM_ROWS, LANES), F32), mesh=mesh,
               scratch_types=[pltpu.VMEM((MEM_HEADS, SC_LANES), F32), pltpu.VMEM((MEM_HEADS, SC_LANES), F32)],
               compiler_params=pltpu.CompilerParams(needs_layout_passes=False))
    def attend(q_hbm, k_hbm, v_hbm, o_hbm, m_ref, l_ref):
        def body(idx, q_vmem, k_vmem, v_vmem, o_vmem):
            chunk = idx[1]
            lane = lax.iota(jnp.int32, SC_LANES)
            first = jnp.full((SC_LANES,), chunk, jnp.int32) == 0

            @pl.when(chunk == 0)
            def _():
                for h in range(MEM_HEADS):
                    m_ref[h, :] = jnp.zeros((SC_LANES,), F32)
                    l_ref[h, :] = jnp.zeros((SC_LANES,), F32)
                for r in range(_MEM_ROWS):
                    for j in range(_SC_PIECES):
                        o_vmem[r, pl.ds(j * SC_LANES, SC_LANES)] = jnp.zeros((SC_LANES,), F32)

            def one_head(h, carry):
                acc = [jnp.zeros((SC_LANES,), F32) for _ in range(SC_TOKENS)]
                for half in range(_MEM_HALVES):
                    r = half * MEM_HEADS + h
                    for j in range(_SC_PIECES):
                        sl = pl.ds(j * SC_LANES, SC_LANES)
                        qv = q_vmem[r, sl]
                        for t in range(SC_TOKENS):
                            acc[t] = acc[t] + k_vmem[t * _MEM_ROWS + r, sl] * qv
                s = jnp.zeros((SC_LANES,), F32)
                for t in range(SC_TOKENS):
                    s = jnp.where(lane == t, jnp.sum(acc[t]), s)

                m_old = m_ref[h, :]
                c_max = jnp.full((SC_LANES,), jnp.max(s), F32)
                m_new = jnp.where(first, c_max, jnp.maximum(m_old, c_max))
                alpha = jnp.where(first, 0.0, jnp.exp(m_old - m_new))
                p = jnp.exp(s - m_new)
                l_ref[h, :] = l_ref[h, :] * alpha + jnp.sum(p)
                m_ref[h, :] = m_new

                pt = [jnp.full((SC_LANES,), p[t], F32) for t in range(SC_TOKENS)]
                for half in range(_MEM_HALVES):
                    r = half * MEM_HEADS + h
                    for j in range(_SC_PIECES):
                        sl = pl.ds(j * SC_LANES, SC_LANES)
                        o = o_vmem[r, sl] * alpha
                        for t in range(SC_TOKENS):
                            o = o + pt[t] * v_vmem[t * _MEM_ROWS + r, sl]
                        o_vmem[r, sl] = o
                return carry

            lax.fori_loop(0, MEM_HEADS, one_head, 0)

            @pl.when(chunk == _SC_CHUNKS - 1)
            def _():
                for h in range(MEM_HEADS):
                    inv = 1.0 / l_ref[h, :]
                    for half in range(_MEM_HALVES):
                        r = half * MEM_HEADS + h
                        for j in range(_SC_PIECES):
                            sl = pl.ds(j * SC_LANES, SC_LANES)
                            o_vmem[r, sl] = o_vmem[r, sl] * inv

        pltpu.emit_pipeline(
            body,
            grid=(nseq, _SC_CHUNKS),
            in_specs=[pl.BlockSpec((_MEM_ROWS, LANES), lambda s, c: (seq0 + s, 0)),
                      pl.BlockSpec((_SC_ROWS, LANES), lambda s, c: (chunk0 + s * _SC_CHUNKS + c, 0)),
                      pl.BlockSpec((_SC_ROWS, LANES), lambda s, c: (chunk0 + s * _SC_CHUNKS + c, 0))],
            out_specs=[pl.BlockSpec((_MEM_ROWS, LANES), lambda s, c: (s, 0))],
            core_axis_name=("c", "s"),
            dimension_semantics=(pltpu.PARALLEL, pltpu.ARBITRARY),
            _explicit_indices=True,
        )(q_hbm, k_hbm, v_hbm, o_hbm)

    return attend(q8, kflat, vflat)


TM_PROMPT = 1024
BB_MIX_SAMPLE = 32


def kernel(x_prompt, x_sample, mem_prompt, cache_win_k, cache_win_v, state_conv, cache_mem_k, cache_mem_v,
           ln_g, ln_b, ffn1_w_gu, ffn1_w_down, w_in, conv_w, attn_sinks, w_out,
           w_cq, w_mk, w_mv, w_co, ffn2_w_gu, ffn2_w_down):
    bsz, seq, _ = x_prompt.shape
    nsmp = x_sample.shape[0]
    yp = x_prompt
    ys = x_sample.reshape(nsmp, D_MODEL)
    mem2 = mem_prompt.reshape(bsz * MEM_LEN, D_MODEL)

    stacked = dict(ffn1=(ffn1_w_gu, ffn1_w_down), ffn2=(ffn2_w_gu, ffn2_w_down),
                   mix=(w_in, w_out), cross=(w_cq, w_co, w_mk, w_mv))
    first_tile, wg0, wu0, wd0 = _ffn_first_tile(x_prompt.reshape(bsz * seq, D_MODEL), ffn1_w_gu, ffn1_w_down,
                                                ln_g, ln_b, 0, 0, TM_PROMPT)
    wb = {("ffn1", 0): ((wg0, wu0), wd0)}

    def jobs(*groups):
        return [(w, layer) for name, layer in groups for w in stacked[name]]

    def keep(groups, casts):
        casts = list(casts)
        for name, layer in groups:
            wb[(name, layer)] = tuple(casts[:len(stacked[name])])
            del casts[:len(stacked[name])]

    def token_tiles(c):
        c = c.reshape(DEPTH, nsmp, MEM_LEN, MEM_HEADS, _MEM_HALVES, LANES)
        return c.transpose(0, 1, 2, 4, 3, 5).reshape(DEPTH, nsmp, MEM_LEN * _MEM_ROWS, LANES)

    mem_k_tiles, mem_v_tiles = token_tiles(cache_mem_k), token_tiles(cache_mem_v)

    def window_lanes(c):
        return c.transpose(0, 1, 3, 4, 2).reshape(DEPTH, nsmp, D_KV, WINDOW)

    win_k_lanes, win_v_lanes = window_lanes(cache_win_k), window_lanes(cache_win_v)
    new_windows = None
    mem_tiles = None

    wkp, wvp, cvp, cvs = [], [], [], []
    for l in range(DEPTH):
        more = l + 1 < DEPTH
        side = [("mix", 0), ("cross", 0)] if l == 0 else [("ffn2", l)]
        yp, ys, *casts = _ffn(yp.reshape(bsz * seq, D_MODEL), ys, *wb[("ffn1", l)], ln_g, ln_b, l, 0, TM_PROMPT,
                              jobs(*side), first_tile=first_tile if l == 0 else None)
        keep(side, casts)
        yp = yp.reshape(bsz, seq, D_MODEL)

        win, wout = wb[("mix", l)]
        side = [("ffn2", 0)] if l == 0 else []
        yp, nk, nv, ncv, *casts = _mix_prompt(yp, attn_sinks, win, conv_w, wout, ln_g, ln_b, l, 1, TM_PROMPT,
                                              jobs(*side))
        keep(side, casts)
        wkp.append(nk.reshape(bsz, WINDOW, N_KV_HEADS, HEAD_DIM))
        wvp.append(nv.reshape(bsz, WINDOW, N_KV_HEADS, HEAD_DIM))
        cvp.append(ncv)
        wcq, wco, wmk, wmv = wb[("cross", l)]
        ys, nst, nks, nvs, q8 = _mix_sample(
            ys, attn_sinks, state_conv[l].reshape(nsmp, (CONV_W - 1) * D_CONV),
            win_k_lanes, win_v_lanes, new_windows, win, conv_w, wout, wcq, ln_g, ln_b, l, 1, BB_MIX_SAMPLE)
        new_windows = (nks, nvs)
        cvs.append(nst.reshape(nsmp, CONV_W - 1, D_CONV))

        o8 = _sc_cross_attention(q8.reshape(nsmp * _MEM_ROWS, LANES), mem_k_tiles, mem_v_tiles, l, 0, nsmp)
        mk_all, mv_all, mkb, mvb = _memkv(mem2, wmk, wmv, l, mem_tiles)
        mem_tiles = (mk_all, mv_all)
        side = [("ffn1", l + 1)] if more else []
        yp, *casts = _cross_prompt(yp, mkb.reshape(bsz, MEM_LEN, D_MODEL), mvb.reshape(bsz, MEM_LEN, D_MODEL),
                                   wcq, wco, ln_g, ln_b, l, 2, TM_PROMPT, jobs(*side))
        keep(side, casts)

        side = [("mix", l + 1), ("cross", l + 1)] if more else []
        yp, ys, *casts = _ffn(yp.reshape(bsz * seq, D_MODEL), ys, *wb[("ffn2", l)], ln_g, ln_b, l, 3, TM_PROMPT,
                              jobs(*side), sample_cross=(o8.reshape(nsmp, _MEM_ROWS, LANES), wco, 2))
        keep(side, casts)
        yp = yp.reshape(bsz, seq, D_MODEL)

    def window_rows(c):
        return c.reshape(DEPTH, nsmp, N_KV_HEADS, HEAD_DIM, WINDOW).transpose(0, 1, 4, 2, 3)

    def token_rows(c):
        c = c.reshape(DEPTH, bsz, MEM_LEN, _MEM_HALVES, MEM_HEADS, LANES)
        return c.transpose(0, 1, 2, 4, 3, 5).reshape(DEPTH, bsz, MEM_LEN, MEM_HEADS, MEM_HEAD_DIM)

    return (yp, ys.reshape(nsmp, 1, D_MODEL),
            jnp.stack(wkp), jnp.stack(wvp), jnp.stack(cvp), token_rows(mem_tiles[0]), token_rows(mem_tiles[1]),
            window_rows(new_windows[0]), window_rows(new_windows[1]), jnp.stack(cvs))
```

```python
import functools

import jax
import jax.numpy as jnp
from jax import lax
from jax.experimental import pallas as pl
from jax.experimental.pallas import tpu as pltpu
from jax.experimental.pallas import tpu_sc as plsc

D_MODEL = 1024
DEPTH = 2
D_CONV = 512
CONV_W = 3
HEAD_DIM = 64
N_HEADS = 8
N_KV_HEADS = 2
GROUP = N_HEADS // N_KV_HEADS
WINDOW = 128
D_ATTN = N_HEADS * HEAD_DIM
D_KV = N_KV_HEADS * HEAD_DIM
IN_COLS = 3 * D_CONV + D_ATTN + 2 * D_KV
MEM_LEN = 256
MEM_HEADS = 4
MEM_HEAD_DIM = D_MODEL // MEM_HEADS
D_FF = 2816
N_LN = 4
ALPHA = (2.0 * DEPTH) ** 0.25
LN_EPS = 1e-5
LOG2E = 1.4426950408889634

_C_BG, _C_CG, _C_HC, _C_Q, _C_KV = 0, D_CONV, 2 * D_CONV, 3 * D_CONV, 3 * D_CONV + D_ATTN

LANES = 128
SUBLANES = 8
MXU_COLS = 256
VMEM_LIMIT_BYTES = 56 * 1024 * 1024

BF16 = jnp.bfloat16
F32 = jnp.float32


def _dot(a, b):
    return jnp.dot(a, b, preferred_element_type=F32)


def _dot_nt(a, b):
    return lax.dot_general(a, b, (((1,), (1,)), ((), ())), preferred_element_type=F32)


def _layer_norm(z, g, b):
    mu = jnp.mean(z, axis=-1, keepdims=True)
    d = z - mu
    var = jnp.mean(d * d, axis=-1, keepdims=True)
    return d * lax.rsqrt(var + LN_EPS) * g + b


def _resident(shape):
    zeros = (0,) * len(shape)
    return pl.BlockSpec(shape, lambda *_: zeros, pipeline_mode=pl.Buffered(1))


def _weight(shape):
    return _resident((1,) + shape)


def _cast_specs(jobs, n_steps, step_of):
    in_specs, out_specs, out_shapes, args = [], [], [], []
    for w, layer in jobs:
        _, k, n = w.shape
        rows = k // n_steps
        assert rows * n_steps == k and rows % (2 * SUBLANES) == 0, (w.shape, n_steps)
        in_specs.append(pl.BlockSpec((1, rows, n), lambda *idx, layer=layer: (layer, step_of(*idx), 0)))
        out_specs.append(pl.BlockSpec((1, rows, n), lambda *idx: (0, step_of(*idx), 0)))
        out_shapes.append(jax.ShapeDtypeStruct((1, k, n), BF16))
        args.append(w)
    return in_specs, out_specs, out_shapes, args


def _run_casts(src_refs, dst_refs):
    for src, dst in zip(src_refs, dst_refs, strict=True):
        dst[...] = src[...].astype(BF16)


def _ln_specs():
    return [_resident((DEPTH, N_LN, D_MODEL)), _resident((DEPTH, N_LN, D_MODEL))]


def _params(*sem):
    return pltpu.CompilerParams(dimension_semantics=sem, vmem_limit_bytes=VMEM_LIMIT_BYTES)


NORM_ROWS = 256


def _project_and_norm(lhs_ref, w, x_rows, scale, g, b, write_rows, rows):
    for r0 in range(0, rows, min(NORM_ROWS, rows)):
        r1 = r0 + min(NORM_ROWS, rows)
        y = _dot(lhs_ref[r0:r1, :], w)
        write_rows(r0, r1, _layer_norm(ALPHA * x_rows(r0, r1) + scale * y, g, b))


def _swiglu(gate, up):
    return (gate * up / (1.0 + jnp.exp2(gate * -LOG2E))).astype(BF16)


def _ffn_rows(x_ref, o_ref, wg_ref, wu_ref, wd_ref, h_ref, g, b):
    rows = x_ref.shape[0]
    xb = x_ref[...].astype(BF16)
    for c in range(D_FF // MXU_COLS):
        lo = c * MXU_COLS
        h_ref[0:rows, lo:lo + MXU_COLS] = _swiglu(_dot(xb, wg_ref[0, :, lo:lo + MXU_COLS]),
                                                  _dot(xb, wu_ref[0, :, lo:lo + MXU_COLS]))

    def write_rows(r0, r1, y):
        o_ref[r0:r1, :] = y

    _project_and_norm(h_ref, wd_ref[0], lambda r0, r1: x_ref[r0:r1, :], 0.5, g, b, write_rows, rows)


def _ffn_kernel(x_ref, xs_ref, wg_ref, wu_ref, wd_ref, g_ref, b_ref, *rest,
                layer, ln, n_cast, n_tiles, cross_ln, has_first):
    n_pre = (0 if cross_ln is None else 2) + (1 if has_first else 0)
    pre, rest = rest[:n_pre], rest[n_pre:]
    cast_src, (o_ref, os_ref), rest = rest[:n_cast], rest[n_cast:n_cast + 2], rest[n_cast + 2:]
    cast_dst, scratch = rest[:n_cast], rest[n_cast:]
    h_ref = scratch[0]
    g = g_ref[layer, ln:ln + 1, :]
    b = b_ref[layer, ln:ln + 1, :]
    step = pl.program_id(0)

    if has_first:
        @pl.when(step == 0)
        def _():
            _run_casts(cast_src, cast_dst)
            o_ref[...] = pre[-1][...]

    @pl.when((step < n_tiles) & (step > 0) if has_first else step < n_tiles)
    def _():
        _run_casts(cast_src, cast_dst)
        _ffn_rows(x_ref, o_ref, wg_ref, wu_ref, wd_ref, h_ref, g, b)

    @pl.when(step == n_tiles)
    def _():
        if cross_ln is None:
            _ffn_rows(xs_ref, os_ref, wg_ref, wu_ref, wd_ref, h_ref, g, b)
        else:
            o8_ref, wo_ref = pre[:2]
            obuf, xs_scr = scratch[1:]
            for row, col in _head_piece_cols():
                obuf[:, col:col + LANES] = o8_ref[:, row, :].astype(BF16)
            xs_scr[...] = _layer_norm(ALPHA * xs_ref[...] + _dot(obuf[...], wo_ref[0]),
                                      g_ref[layer, cross_ln:cross_ln + 1, :], b_ref[layer, cross_ln:cross_ln + 1, :])
            _ffn_rows(xs_scr, os_ref, wg_ref, wu_ref, wd_ref, h_ref, g, b)


def _ffn(x, xs, wgu, wd, g, b, layer, ln, tm, casts=(), sample_cross=None, first_tile=None):
    m = x.shape[0]
    ns = xs.shape[0]
    n_tiles = m // tm
    tile = lambda i: (jnp.minimum(i, n_tiles - 1), 0)
    c_in, c_out, c_shapes, c_args = _cast_specs(casts, n_tiles, lambda i: jnp.minimum(i, n_tiles - 1))
    half = lambda j: pl.BlockSpec((1, D_MODEL, D_FF), lambda i: (0, 0, j), pipeline_mode=pl.Buffered(1))
    if isinstance(wgu, tuple):
        w_args, w_specs = list(wgu), [half(0), half(0)]
    else:
        w_args, w_specs = [wgu, wgu], [half(0), half(1)]
    pre_specs, pre_args, pre_scratch, cross_ln = [], [], [], None
    if sample_cross is not None:
        o8, wo, cross_ln = sample_cross
        pre_specs = [_resident((ns, _MEM_ROWS, LANES)), _weight((D_MODEL, D_MODEL))]
        pre_args = [o8, wo]
        pre_scratch = [pltpu.VMEM((ns, D_MODEL), BF16), pltpu.VMEM((ns, D_MODEL), F32)]
    if first_tile is not None:
        pre_specs.append(_resident((tm, D_MODEL)))
        pre_args.append(first_tile)
    return pl.pallas_call(
        functools.partial(_ffn_kernel, layer=layer, ln=ln, n_cast=len(casts), n_tiles=n_tiles, cross_ln=cross_ln,
                          has_first=first_tile is not None),
        grid=(n_tiles + 1,),
        in_specs=[pl.BlockSpec((tm, D_MODEL), tile), _resident((ns, D_MODEL))] + w_specs
        + [_weight((D_FF, D_MODEL))] + _ln_specs() + pre_specs + c_in,
        out_specs=[pl.BlockSpec((tm, D_MODEL), tile), pl.BlockSpec((ns, D_MODEL), lambda i: (0, 0))] + c_out,
        out_shape=[jax.ShapeDtypeStruct((m, D_MODEL), F32), jax.ShapeDtypeStruct((ns, D_MODEL), F32)] + c_shapes,
        scratch_shapes=[pltpu.VMEM((tm, D_FF), BF16)] + pre_scratch,
        compiler_params=_params("arbitrary"),
        name="ffn_ln",
    )(x, xs, *w_args, wd, g, b, *pre_args, *c_args)


def _ffn_first_tile_kernel(x_ref, wg_ref, wu_ref, wd_ref, g_ref, b_ref, o_ref, wgb_ref, wub_ref, wdb_ref,
                           xb_scr, h_scr, wd_scr, *, layer, ln):
    c = pl.program_id(0)
    n_chunks = h_scr.shape[0]

    @pl.when(c == 0)
    def _():
        xb_scr[...] = x_ref[...].astype(BF16)

    wg = wg_ref[0].astype(BF16)
    wu = wu_ref[0].astype(BF16)
    wd = wd_ref[0].astype(BF16)
    wgb_ref[0] = wg
    wub_ref[0] = wu
    wdb_ref[0] = wd
    wd_scr[c] = wd
    xb = xb_scr[...]
    h_scr[c] = _swiglu(_dot(xb, wg), _dot(xb, wu))

    @pl.when(c == n_chunks - 1)
    def _():
        rows = x_ref.shape[0]
        g = g_ref[layer, ln:ln + 1, :]
        b = b_ref[layer, ln:ln + 1, :]
        for r0 in range(0, rows, NORM_ROWS):
            r1 = r0 + NORM_ROWS
            y = _dot(h_scr[0, r0:r1, :], wd_scr[0])
            for k in range(1, n_chunks):
                y = y + _dot(h_scr[k, r0:r1, :], wd_scr[k])
            o_ref[r0:r1, :] = _layer_norm(ALPHA * x_ref[r0:r1, :] + 0.5 * y, g, b)


def _ffn_first_tile(x, wgu32, wd32, g, b, layer, ln, tm):
    n_chunks = D_FF // MXU_COLS
    return pl.pallas_call(
        functools.partial(_ffn_first_tile_kernel, layer=layer, ln=ln),
        grid=(n_chunks,),
        in_specs=[_resident((tm, D_MODEL)),
                  pl.BlockSpec((1, D_MODEL, MXU_COLS), lambda c: (layer, 0, c)),
                  pl.BlockSpec((1, D_MODEL, MXU_COLS), lambda c: (layer, 0, n_chunks + c)),
                  pl.BlockSpec((1, MXU_COLS, D_MODEL), lambda c: (layer, c, 0))] + _ln_specs(),
        out_specs=[pl.BlockSpec((tm, D_MODEL), lambda c: (0, 0)),
                   pl.BlockSpec((1, D_MODEL, MXU_COLS), lambda c: (0, 0, c)),
                   pl.BlockSpec((1, D_MODEL, MXU_COLS), lambda c: (0, 0, c)),
                   pl.BlockSpec((1, MXU_COLS, D_MODEL), lambda c: (0, c, 0))],
        out_shape=[jax.ShapeDtypeStruct((tm, D_MODEL), F32),
                   jax.ShapeDtypeStruct((1, D_MODEL, D_FF), BF16),
                   jax.ShapeDtypeStruct((1, D_MODEL, D_FF), BF16),
                   jax.ShapeDtypeStruct((1, D_FF, D_MODEL), BF16)],
        scratch_shapes=[pltpu.VMEM((tm, D_MODEL), BF16),
                        pltpu.VMEM((n_chunks, tm, MXU_COLS), BF16),
                        pltpu.VMEM((n_chunks, MXU_COLS, D_MODEL), BF16)],
        compiler_params=_params("arbitrary"),
        name="ffn_first_tile",
    )(x, wgu32, wgu32, wd32, g, b)


def _dup_halves(a):
    lane = lax.broadcasted_iota(jnp.int32, a.shape, 1)
    low = lane < HEAD_DIM
    rolled = pltpu.roll(a, HEAD_DIM, 1)
    return jnp.where(low, a, rolled), jnp.where(low, rolled, a)


def _sink_column(sink_ref, layer, first_head, rows_per_head, n_heads):
    rows = n_heads * rows_per_head
    r = lax.broadcasted_iota(jnp.int32, (rows, 1), 0)
    col = jnp.full((rows, 1), sink_ref[layer, first_head + n_heads - 1], F32)
    for g in range(n_heads - 2, -1, -1):
        col = jnp.where(r < (g + 1) * rows_per_head, sink_ref[layer, first_head + g], col)
    return col


def _mix_prompt_kernel(sink_ref, x_ref, win_ref, cw_ref, wout_ref, g_ref, b_ref, *rest, tm, layer, ln, n_cast):
    cast_src, rest = rest[:n_cast], rest[n_cast:]
    (o_ref, nk_ref, nv_ref, nc_ref), rest = rest[:4], rest[4:]
    cast_dst, (ubuf, qbuf, kd0, kd1, vd0, vd1, zbuf) = rest[:n_cast], rest[n_cast:]
    seq_start = pl.program_id(1) == 0

    @pl.when(seq_start)
    def _():
        ubuf[0:SUBLANES, :] = jnp.zeros((SUBLANES, D_CONV), F32)
        for r in (kd0, kd1, vd0, vd1):
            r[0:WINDOW, :] = jnp.zeros((WINDOW, r.shape[1]), BF16)

    _run_casts(cast_src, cast_dst)
    xb = x_ref[0].astype(BF16)

    def proj(col, width):
        return _dot(xb, win_ref[0, :, col:col + width])

    u = proj(_C_CG, D_CONV) * proj(_C_HC, D_CONV)
    ubuf[SUBLANES:SUBLANES + tm, :] = u
    conv = (cw_ref[layer, 0:1, :] * ubuf[SUBLANES - 2:SUBLANES - 2 + tm, :]
            + cw_ref[layer, 1:2, :] * ubuf[SUBLANES - 1:SUBLANES - 1 + tm, :]
            + cw_ref[layer, 2:3, :] * u)
    zbuf[:, 0:D_CONV] = (proj(_C_BG, D_CONV) * conv).astype(BF16)
    nc_ref[0] = u[tm - (CONV_W - 1):tm, :]
    ubuf[0:SUBLANES, :] = u[tm - SUBLANES:tm, :]

    qbuf[...] = (proj(_C_Q, D_ATTN) * (HEAD_DIM ** -0.5 * LOG2E)).astype(BF16)
    k = proj(_C_KV, D_KV)
    v = proj(_C_KV + D_KV, D_KV)
    nk_ref[0] = k[tm - WINDOW:tm, :]
    nv_ref[0] = v[tm - WINDOW:tm, :]
    ka, kb = _dup_halves(k)
    va, vb = _dup_halves(v)
    kd0[WINDOW:WINDOW + tm, :] = ka.astype(BF16)
    kd1[WINDOW:WINDOW + tm, :] = kb.astype(BF16)
    ones = jnp.ones((tm, LANES), BF16)
    vd0[WINDOW:WINDOW + tm, :] = jnp.concatenate([va.astype(BF16), ones], axis=1)
    vd1[WINDOW:WINDOW + tm, :] = jnp.concatenate([vb.astype(BF16), ones], axis=1)

    rows = GROUP * WINDOW
    a_idx = lax.broadcasted_iota(jnp.int32, (rows, 2 * WINDOW), 0) % WINDOW
    c_idx = lax.broadcasted_iota(jnp.int32, (rows, 2 * WINDOW), 1)
    band = (c_idx >= a_idx) & (c_idx <= a_idx + WINDOW)
    first_lb = jnp.where(seq_start, WINDOW, 0)
    low = lax.broadcasted_iota(jnp.int32, (WINDOW, LANES), 1) < HEAD_DIM
    zero = jnp.zeros((WINDOW, LANES), BF16)

    for n in range(tm // WINDOW):
        r0 = n * WINDOW
        mask = (band & (c_idx >= first_lb)) if n == 0 else band
        for j, (kd, vd) in enumerate(((kd0, vd0), (kd1, vd1))):
            qa = qbuf[r0:r0 + WINDOW, (2 * j) * LANES:(2 * j + 1) * LANES]
            qb = qbuf[r0:r0 + WINDOW, (2 * j + 1) * LANES:(2 * j + 2) * LANES]
            qs = jnp.concatenate([jnp.where(low, qa, zero), jnp.where(low, zero, qa),
                                  jnp.where(low, qb, zero), jnp.where(low, zero, qb)], axis=0)
            s = _dot_nt(qs, kd[r0:r0 + 2 * WINDOW, :])
            s = jnp.where(mask, s, -jnp.inf)
            sink = _sink_column(sink_ref, layer, GROUP * j, WINDOW, GROUP) * LOG2E
            m = jnp.max(s, axis=-1, keepdims=True)
            p = jnp.exp2(s - m)
            ov = _dot(p.astype(BF16), vd[r0:r0 + 2 * WINDOW, :])
            o = ov[:, 0:LANES] / (ov[:, LANES:2 * LANES] + jnp.exp2(sink - m))
            za = jnp.where(low, o[0:WINDOW], o[WINDOW:2 * WINDOW])
            zb = jnp.where(low, o[2 * WINDOW:3 * WINDOW], o[3 * WINDOW:4 * WINDOW])
            c0 = D_CONV + (2 * j) * LANES
            zbuf[r0:r0 + WINDOW, c0:c0 + LANES] = za.astype(BF16)
            zbuf[r0:r0 + WINDOW, c0 + LANES:c0 + 2 * LANES] = zb.astype(BF16)

    for r in (kd0, kd1, vd0, vd1):
        r[0:WINDOW, :] = r[tm:tm + WINDOW, :]

    def write_rows(r0, r1, y):
        o_ref[0, r0:r1, :] = y

    _project_and_norm(zbuf, wout_ref[0], lambda r0, r1: x_ref[0, r0:r1, :], 1.0,
                      g_ref[layer, ln:ln + 1, :], b_ref[layer, ln:ln + 1, :], write_rows, tm)


def _mix_prompt(x, sinks, win, cw, wout, g, b, layer, ln, tm, casts=()):
    bsz, seq, _ = x.shape
    tps = seq // tm
    kern = functools.partial(_mix_prompt_kernel, tm=tm, layer=layer, ln=ln, n_cast=len(casts))
    per_seq = lambda bi, i, s: (bi, 0, 0)
    c_in, c_out, c_shapes, c_args = _cast_specs(casts, bsz * tps, lambda bi, i, *_: bi * tps + i)
    grid_spec = pltpu.PrefetchScalarGridSpec(
        num_scalar_prefetch=1,
        grid=(bsz, tps),
        in_specs=[
            pl.BlockSpec((1, tm, D_MODEL), lambda bi, i, s: (bi, i, 0)),
            _weight((D_MODEL, IN_COLS)),
            _resident((DEPTH, CONV_W, D_CONV)),
            _weight((D_CONV + D_ATTN, D_MODEL)),
        ] + _ln_specs() + c_in,
        out_specs=[
            pl.BlockSpec((1, tm, D_MODEL), lambda bi, i, s: (bi, i, 0)),
            pl.BlockSpec((1, WINDOW, D_KV), per_seq),
            pl.BlockSpec((1, WINDOW, D_KV), per_seq),
            pl.BlockSpec((1, CONV_W - 1, D_CONV), per_seq),
        ] + c_out,
        scratch_shapes=[
            pltpu.VMEM((tm + SUBLANES, D_CONV), F32),
            pltpu.VMEM((tm, D_ATTN), BF16),
            pltpu.VMEM((tm + WINDOW, LANES), BF16),
            pltpu.VMEM((tm + WINDOW, LANES), BF16),
            pltpu.VMEM((tm + WINDOW, 2 * LANES), BF16),
            pltpu.VMEM((tm + WINDOW, 2 * LANES), BF16),
            pltpu.VMEM((tm, D_CONV + D_ATTN), BF16),
        ],
    )
    return pl.pallas_call(
        kern,
        grid_spec=grid_spec,
        out_shape=[
            jax.ShapeDtypeStruct((bsz, seq, D_MODEL), F32),
            jax.ShapeDtypeStruct((bsz, WINDOW, D_KV), F32),
            jax.ShapeDtypeStruct((bsz, WINDOW, D_KV), F32),
            jax.ShapeDtypeStruct((bsz, CONV_W - 1, D_CONV), F32),
        ] + c_shapes,
        compiler_params=_params("arbitrary", "arbitrary"),
        name="mix_prompt",
    )(sinks, x, win, cw, wout, g, b, *c_args)


def _memkv_kernel(m_ref, wk_ref, wv_ref, *rest, layer):
    k_ref, v_ref, kb_ref, vb_ref = rest[-4:]
    mb = m_ref[...].astype(BF16)
    k = _dot(mb, wk_ref[0])
    v = _dot(mb, wv_ref[0])
    kb_ref[...] = k.astype(BF16)
    vb_ref[...] = v.astype(BF16)
    slab = layer if k_ref.shape[0] > 1 else 0
    for other in range(k_ref.shape[0]):
        if other != slab:
            k_ref[other] = jnp.zeros(k_ref.shape[1:], F32)
            v_ref[other] = jnp.zeros(v_ref.shape[1:], F32)
    for row, col in _head_piece_cols():
        k_ref[slab, 0, :, row, :] = k[:, col:col + LANES]
        v_ref[slab, 0, :, row, :] = v[:, col:col + LANES]


def _memkv(mem, wk, wv, layer, prev):
    m = mem.shape[0]
    nb = m // MEM_LEN
    blk = pl.BlockSpec((MEM_LEN, D_MODEL), lambda i: (i, 0))
    in_specs = [blk, _weight((D_MODEL, D_MODEL)), _weight((D_MODEL, D_MODEL))]
    args = [mem, wk, wv]
    aliases = {}
    if prev is None:
        tile_blk = pl.BlockSpec((DEPTH, 1, MEM_LEN, _MEM_ROWS, LANES), lambda i: (0, i, 0, 0, 0))
    else:
        tile_blk = pl.BlockSpec((1, 1, MEM_LEN, _MEM_ROWS, LANES), lambda i: (layer, i, 0, 0, 0))
        in_specs += [pl.BlockSpec(memory_space=pl.ANY)] * 2
        args += list(prev)
        aliases = {3: 0, 4: 1}
    tiles = jax.ShapeDtypeStruct((DEPTH, nb, MEM_LEN, _MEM_ROWS, LANES), F32)
    return pl.pallas_call(
        functools.partial(_memkv_kernel, layer=layer),
        grid=(nb,),
        in_specs=in_specs,
        out_specs=[tile_blk, tile_blk, blk, blk],
        out_shape=[tiles, tiles, jax.ShapeDtypeStruct((m, D_MODEL), BF16), jax.ShapeDtypeStruct((m, D_MODEL), BF16)],
        input_output_aliases=aliases,
        compiler_params=_params("arbitrary"),
        name="mem_kv",
    )(*args)


def _cross_prompt_kernel(x_ref, mk_ref, mv_ref, wq_ref, wo_ref, g_ref, b_ref, *rest, layer, ln, n_cast):
    cast_src, o_ref, cast_dst, obuf = rest[:n_cast], rest[n_cast], rest[n_cast + 1:2 * n_cast + 1], rest[-1]
    _run_casts(cast_src, cast_dst)
    q = _dot(x_ref[0].astype(BF16), wq_ref[0]) * (MEM_HEAD_DIM ** -0.5 * LOG2E)
    for h in range(MEM_HEADS):
        c0 = h * MEM_HEAD_DIM
        s = _dot_nt(q[:, c0:c0 + MEM_HEAD_DIM].astype(BF16), mk_ref[0, :, c0:c0 + MEM_HEAD_DIM])
        m = jnp.max(s, axis=-1, keepdims=True)
        p = jnp.exp2(s - m)
        denom = jnp.sum(p, axis=-1, keepdims=True)
        o = _dot(p.astype(BF16), mv_ref[0, :, c0:c0 + MEM_HEAD_DIM]) / denom
        obuf[:, c0:c0 + MEM_HEAD_DIM] = o.astype(BF16)

    def write_rows(r0, r1, y):
        o_ref[0, r0:r1, :] = y

    _project_and_norm(obuf, wo_ref[0], lambda r0, r1: x_ref[0, r0:r1, :], 1.0,
                      g_ref[layer, ln:ln + 1, :], b_ref[layer, ln:ln + 1, :], write_rows, x_ref.shape[1])


def _cross_prompt(x, mk, mv, wq, wo, g, b, layer, ln, tm, casts=()):
    bsz, seq, _ = x.shape
    tps = seq // tm
    xblk = pl.BlockSpec((1, tm, D_MODEL), lambda bi, i: (bi, i, 0))
    mblk = pl.BlockSpec((1, MEM_LEN, D_MODEL), lambda bi, i: (bi, 0, 0))
    c_in, c_out, c_shapes, c_args = _cast_specs(casts, bsz * tps, lambda bi, i: bi * tps + i)
    return pl.pallas_call(
        functools.partial(_cross_prompt_kernel, layer=layer, ln=ln, n_cast=len(casts)),
        grid=(bsz, tps),
        in_specs=[xblk, mblk, mblk, _weight((D_MODEL, D_MODEL)), _weight((D_MODEL, D_MODEL))] + _ln_specs() + c_in,
        out_specs=[xblk] + c_out,
        out_shape=[jax.ShapeDtypeStruct((bsz, seq, D_MODEL), F32)] + c_shapes,
        scratch_shapes=[pltpu.VMEM((tm, D_MODEL), BF16)],
        compiler_params=_params("arbitrary", "arbitrary"),
        name="cross_prompt",
    )(x, mk, mv, wq, wo, g, b, *c_args)


def _bdot(a, b, contract_b):
    return lax.dot_general(a, b, (((2,), (contract_b,)), ((0,), (0,))), preferred_element_type=F32)


def _mix_sample_kernel(sink_ref, x_ref, st_ref, kc_ref, vc_ref, win_ref, cw_ref, wout_ref, wq_ref, g_ref, b_ref,
                       *rest, bb, layer, ln):
    o_ref, nst_ref, nk_ref, nv_ref, q8_ref, zbuf, qm_scr, kn_scr, vn_scr, o_scr = rest[-10:]
    x = x_ref[...]
    xb = x.astype(BF16)

    def proj(col, width):
        return _dot(xb, win_ref[0, :, col:col + width])

    u = proj(_C_CG, D_CONV) * proj(_C_HC, D_CONV)
    st0 = st_ref[:, 0:D_CONV]
    st1 = st_ref[:, D_CONV:2 * D_CONV]
    conv = cw_ref[layer, 0:1, :] * st0 + cw_ref[layer, 1:2, :] * st1 + cw_ref[layer, 2:3, :] * u
    zbuf[:, 0:D_CONV] = (proj(_C_BG, D_CONV) * conv).astype(BF16)
    nst_ref[:, 0:D_CONV] = st1
    nst_ref[:, D_CONV:2 * D_CONV] = u

    q = proj(_C_Q, D_ATTN) * (HEAD_DIM ** -0.5)
    k = proj(_C_KV, D_KV)
    v = proj(_C_KV + D_KV, D_KV)

    pad = jnp.zeros((LANES - bb, D_KV), F32)
    kt = jnp.concatenate([k, pad], axis=0).T if bb < LANES else k.T
    vt = jnp.concatenate([v, pad], axis=0).T if bb < LANES else v.T
    last = lax.broadcasted_iota(jnp.int32, (D_KV, WINDOW), 1) == WINDOW - 1
    slab = layer if nk_ref.shape[0] > 1 else 0
    for other in range(nk_ref.shape[0]):
        if other != slab:
            nk_ref[other] = jnp.zeros(nk_ref.shape[1:], F32)
            nv_ref[other] = jnp.zeros(nv_ref.shape[1:], F32)
    for r in range(bb):
        nk_ref[slab, r] = jnp.where(last, kt[:, r:r + 1], pltpu.roll(kc_ref[0, r], WINDOW - 1, 1))
        nv_ref[slab, r] = jnp.where(last, vt[:, r:r + 1], pltpu.roll(vc_ref[0, r], WINDOW - 1, 1))

    lane = lax.broadcasted_iota(jnp.int32, (bb, LANES), 1)
    low = lane < HEAD_DIM
    for h in range(N_HEADS):
        kvh = h // GROUP
        qt = q[:, (h // 2) * LANES:(h // 2 + 1) * LANES]
        if (h % 2) != kvh:
            qt = pltpu.roll(qt, HEAD_DIM, 1)
        keep = low if kvh == 0 else jnp.logical_not(low)
        qm_scr[:, h, :] = jnp.where(keep, qt, 0.0)
        kn_scr[:, h, :] = k
        vn_scr[:, h, :] = v
    qm = qm_scr[...]

    s = _bdot(qm.astype(BF16), kc_ref[0].astype(BF16), 1)
    s_new = jnp.sum(qm * kn_scr[...], axis=-1, keepdims=True)
    hrow = lax.broadcasted_iota(jnp.int32, (1, N_HEADS, 1), 1)
    sink = jnp.full((1, N_HEADS, 1), sink_ref[layer, N_HEADS - 1], F32)
    for h in range(N_HEADS - 1):
        sink = jnp.where(hrow == h, sink_ref[layer, h], sink)
    m = jnp.maximum(jnp.maximum(jnp.max(s, axis=-1, keepdims=True), s_new), sink)
    p = jnp.exp(s - m)
    p_new = jnp.exp(s_new - m)
    denom = jnp.sum(p, axis=-1, keepdims=True) + p_new + jnp.exp(sink - m)
    o_scr[...] = (_bdot(p.astype(BF16), vc_ref[0].astype(BF16), 2) + p_new * vn_scr[...]) / denom

    for t in range(D_ATTN // LANES):
        oa = o_scr[:, 2 * t, :]
        ob = o_scr[:, 2 * t + 1, :]
        if (2 * t) // GROUP == 0:
            z = jnp.where(low, oa, pltpu.roll(ob, HEAD_DIM, 1))
        else:
            z = jnp.where(low, pltpu.roll(oa, HEAD_DIM, 1), ob)
        zbuf[:, D_CONV + t * LANES:D_CONV + (t + 1) * LANES] = z.astype(BF16)

    y = _dot(zbuf[...], wout_ref[0])
    out = _layer_norm(ALPHA * x + y, g_ref[layer, ln:ln + 1, :], b_ref[layer, ln:ln + 1, :])
    o_ref[...] = out
    q = _dot(out.astype(BF16), wq_ref[0]) * (MEM_HEAD_DIM ** -0.5)
    for row, col in _head_piece_cols():
        q8_ref[:, row, :] = q[:, col:col + LANES]


def _mix_sample(x, sinks, st, kc, vc, prev_windows, win, cw, wout, wq, g, b, layer, ln, bb):
    nb = x.shape[0]
    kern = functools.partial(_mix_sample_kernel, bb=bb, layer=layer, ln=ln)
    row = lambda i, s: (i, 0)
    win_blk = pl.BlockSpec((1, bb, D_KV, WINDOW), lambda i, s: (layer, i, 0, 0))
    in_specs = [
        pl.BlockSpec((bb, D_MODEL), row),
        pl.BlockSpec((bb, (CONV_W - 1) * D_CONV), row),
        win_blk,
        win_blk,
        _weight((D_MODEL, IN_COLS)),
        _resident((DEPTH, CONV_W, D_CONV)),
        _weight((D_CONV + D_ATTN, D_MODEL)),
        _weight((D_MODEL, D_MODEL)),
    ] + _ln_specs()
    args = [sinks, x, st, kc, vc, win, cw, wout, wq, g, b]
    aliases = {}
    out_win_blk = win_blk
    if prev_windows is None:
        out_win_blk = pl.BlockSpec((DEPTH, bb, D_KV, WINDOW), lambda i, s: (0, i, 0, 0))
    else:
        n_in = len(args)
        in_specs += [pl.BlockSpec(memory_space=pl.ANY)] * 2
        args += list(prev_windows)
        aliases = {n_in: 2, n_in + 1: 3}
    grid_spec = pltpu.PrefetchScalarGridSpec(
        num_scalar_prefetch=1,
        grid=(nb // bb,),
        in_specs=in_specs,
        out_specs=[pl.BlockSpec((bb, D_MODEL), row), pl.BlockSpec((bb, (CONV_W - 1) * D_CONV), row),
                   out_win_blk, out_win_blk, pl.BlockSpec((bb, _MEM_ROWS, LANES), lambda i, s: (i, 0, 0))],
        scratch_shapes=[pltpu.VMEM((bb, D_CONV + D_ATTN), BF16)]
        + [pltpu.VMEM((bb, N_HEADS, LANES), F32)] * 4,
    )
    return pl.pallas_call(
        kern,
        grid_spec=grid_spec,
        out_shape=[
            jax.ShapeDtypeStruct((nb, D_MODEL), F32),
            jax.ShapeDtypeStruct((nb, (CONV_W - 1) * D_CONV), F32),
            jax.ShapeDtypeStruct((DEPTH, nb, D_KV, WINDOW), F32),
            jax.ShapeDtypeStruct((DEPTH, nb, D_KV, WINDOW), F32),
            jax.ShapeDtypeStruct((nb, _MEM_ROWS, LANES), F32),
        ],
        input_output_aliases=aliases,
        compiler_params=_params("arbitrary"),
        name="mix_sample",
    )(*args)


_MEM_HALVES = MEM_HEAD_DIM // LANES
_MEM_ROWS = _MEM_HALVES * MEM_HEADS


def _head_piece_cols():
    return [(half * MEM_HEADS + h, h * MEM_HEAD_DIM + half * LANES)
            for half in range(_MEM_HALVES) for h in range(MEM_HEADS)]


SC_LANES = 16
SC_TOKENS = 16
_SC_ROWS = SC_TOKENS * _MEM_ROWS
_SC_CHUNKS = MEM_LEN // SC_TOKENS
_SC_PIECES = LANES // SC_LANES


def _sc_cross_attention(q8, mk_tiles, mv_tiles, layer, seq0, nseq):
    total_seq = mk_tiles.shape[1]
    kflat = mk_tiles.reshape(-1, LANES)
    vflat = mv_tiles.reshape(-1, LANES)
    chunk0 = (layer * total_seq + seq0) * _SC_CHUNKS
    mesh = plsc.VectorSubcoreMesh(core_axis_name="c", subcore_axis_name="s")

    @pl.kernel(out_type=jax.ShapeDtypeStruct((nseq * _MEM_ROWS, LANES), F32), mesh=mesh,
               scratch_types=[pltpu.VMEM((MEM_HEADS, SC_LANES), F32), pltpu.VMEM((MEM_HEADS, SC_LANES), F32)],
               compiler_params=pltpu.CompilerParams(needs_layout_passes=False))
    def attend(q_hbm, k_hbm, v_hbm, o_hbm, m_ref, l_ref):
        def body(idx, q_vmem, k_vmem, v_vmem, o_vmem):
            chunk = idx[1]
            lane = lax.iota(jnp.int32, SC_LANES)
            first = jnp.full((SC_LANES,), chunk, jnp.int32) == 0

            @pl.when(chunk == 0)
            def _():
                for h in range(MEM_HEADS):
                    m_ref[h, :] = jnp.zeros((SC_LANES,), F32)
                    l_ref[h, :] = jnp.zeros((SC_LANES,), F32)
                for r in range(_MEM_ROWS):
                    for j in range(_SC_PIECES):
                        o_vmem[r, pl.ds(j * SC_LANES, SC_LANES)] = jnp.zeros((SC_LANES,), F32)

            def one_head(h, carry):
                acc = [jnp.zeros((SC_LANES,), F32) for _ in range(SC_TOKENS)]
                for half in range(_MEM_HALVES):
                    r = half * MEM_HEADS + h
                    for j in range(_SC_PIECES):
                        sl = pl.ds(j * SC_LANES, SC_LANES)
                        qv = q_vmem[r, sl]
                        for t in range(SC_TOKENS):
                            acc[t] = acc[t] + k_vmem[t * _MEM_ROWS + r, sl] * qv
                s = jnp.zeros((SC_LANES,), F32)
                for t in range(SC_TOKENS):
                    s = jnp.where(lane == t, jnp.sum(acc[t]), s)

                m_old = m_ref[h, :]
                c_max = jnp.full((SC_LANES,), jnp.max(s), F32)
                m_new = jnp.where(first, c_max, jnp.maximum(m_old, c_max))
                alpha = jnp.where(first, 0.0, jnp.exp(m_old - m_new))
                p = jnp.exp(s - m_new)
                l_ref[h, :] = l_ref[h, :] * alpha + jnp.sum(p)
                m_ref[h, :] = m_new

                pt = [jnp.full((SC_LANES,), p[t], F32) for t in range(SC_TOKENS)]
                for half in range(_MEM_HALVES):
                    r = half * MEM_HEADS + h
                    for j in range(_SC_PIECES):
                        sl = pl.ds(j * SC_LANES, SC_LANES)
                        o = o_vmem[r, sl] * alpha
                        for t in range(SC_TOKENS):
                            o = o + pt[t] * v_vmem[t * _MEM_ROWS + r, sl]
                        o_vmem[r, sl] = o
                return carry

            lax.fori_loop(0, MEM_HEADS, one_head, 0)

            @pl.when(chunk == _SC_CHUNKS - 1)
            def _():
                for h in range(MEM_HEADS):
                    inv = 1.0 / l_ref[h, :]
                    for half in range(_MEM_HALVES):
                        r = half * MEM_HEADS + h
                        for j in range(_SC_PIECES):
                            sl = pl.ds(j * SC_LANES, SC_LANES)
                            o_vmem[r, sl] = o_vmem[r, sl] * inv

        pltpu.emit_pipeline(
            body,
            grid=(nseq, _SC_CHUNKS),
            in_specs=[pl.BlockSpec((_MEM_ROWS, LANES), lambda s, c: (seq0 + s, 0)),
                      pl.BlockSpec((_SC_ROWS, LANES), lambda s, c: (chunk0 + s * _SC_CHUNKS + c, 0)),
                      pl.BlockSpec((_SC_ROWS, LANES), lambda s, c: (chunk0 + s * _SC_CHUNKS + c, 0))],
            out_specs=[pl.BlockSpec((_MEM_ROWS, LANES), lambda s, c: (s, 0))],
            core_axis_name=("c", "s"),
            dimension_semantics=(pltpu.PARALLEL, pltpu.ARBITRARY),
            _explicit_indices=True,
        )(q_hbm, k_hbm, v_hbm, o_hbm)

    return attend(q8, kflat, vflat)


TM_PROMPT = 1024
BB_MIX_SAMPLE = 32


def kernel(x_prompt, x_sample, mem_prompt, cache_win_k, cache_win_v, state_conv, cache_mem_k, cache_mem_v,
           ln_g, ln_b, ffn1_w_gu, ffn1_w_down, w_in, conv_w, attn_sinks, w_out,
           w_cq, w_mk, w_mv, w_co, ffn2_w_gu, ffn2_w_down):
    bsz, seq, _ = x_prompt.shape
    nsmp = x_sample.shape[0]
    yp = x_prompt
    ys = x_sample.reshape(nsmp, D_MODEL)
    mem2 = mem_prompt.reshape(bsz * MEM_LEN, D_MODEL)

    stacked = dict(ffn1=(ffn1_w_gu, ffn1_w_down), ffn2=(ffn2_w_gu, ffn2_w_down),
                   mix=(w_in, w_out), cross=(w_cq, w_co, w_mk, w_mv))
    first_tile, wg0, wu0, wd0 = _ffn_first_tile(x_prompt.reshape(bsz * seq, D_MODEL), ffn1_w_gu, ffn1_w_down,
                                                ln_g, ln_b, 0, 0, TM_PROMPT)
    wb = {("ffn1", 0): ((wg0, wu0), wd0)}

    def jobs(*groups):
        return [(w, layer) for name, layer in groups for w in stacked[name]]

    def keep(groups, casts):
        casts = list(casts)
        for name, layer in groups:
            wb[(name, layer)] = tuple(casts[:len(stacked[name])])
            del casts[:len(stacked[name])]

    def token_tiles(c):
        c = c.reshape(DEPTH, nsmp, MEM_LEN, MEM_HEADS, _MEM_HALVES, LANES)
        return c.transpose(0, 1, 2, 4, 3, 5).reshape(DEPTH, nsmp, MEM_LEN * _MEM_ROWS, LANES)

    mem_k_tiles, mem_v_tiles = token_tiles(cache_mem_k), token_tiles(cache_mem_v)

    def window_lanes(c):
        return c.transpose(0, 1, 3, 4, 2).reshape(DEPTH, nsmp, D_KV, WINDOW)

    win_k_lanes, win_v_lanes = window_lanes(cache_win_k), window_lanes(cache_win_v)
    new_windows = None
    mem_tiles = None

    wkp, wvp, cvp, cvs = [], [], [], []
    for l in range(DEPTH):
        more = l + 1 < DEPTH
        side = [("mix", 0), ("cross", 0)] if l == 0 else [("ffn2", l)]
        yp, ys, *casts = _ffn(yp.reshape(bsz * seq, D_MODEL), ys, *wb[("ffn1", l)], ln_g, ln_b, l, 0, TM_PROMPT,
                              jobs(*side), first_tile=first_tile if l == 0 else None)
        keep(side, casts)
        yp = yp.reshape(bsz, seq, D_MODEL)

        win, wout = wb[("mix", l)]
        side = [("ffn2", 0)] if l == 0 else []
        yp, nk, nv, ncv, *casts = _mix_prompt(yp, attn_sinks, win, conv_w, wout, ln_g, ln_b, l, 1, TM_PROMPT,
                                              jobs(*side))
        keep(side, casts)
        wkp.append(nk.reshape(bsz, WINDOW, N_KV_HEADS, HEAD_DIM))
        wvp.append(nv.reshape(bsz, WINDOW, N_KV_HEADS, HEAD_DIM))
        cvp.append(ncv)
        wcq, wco, wmk, wmv = wb[("cross", l)]
        ys, nst, nks, nvs, q8 = _mix_sample(
            ys, attn_sinks, state_conv[l].reshape(nsmp, (CONV_W - 1) * D_CONV),
            win_k_lanes, win_v_lanes, new_windows, win, conv_w, wout, wcq, ln_g, ln_b, l, 1, BB_MIX_SAMPLE)
        new_windows = (nks, nvs)
        cvs.append(nst.reshape(nsmp, CONV_W - 1, D_CONV))

        o8 = _sc_cross_attention(q8.reshape(nsmp * _MEM_ROWS, LANES), mem_k_tiles, mem_v_tiles, l, 0, nsmp)
        mk_all, mv_all, mkb, mvb = _memkv(mem2, wmk, wmv, l, mem_tiles)
        mem_tiles = (mk_all, mv_all)
        side = [("ffn1", l + 1)] if more else []
        yp, *casts = _cross_prompt(yp, mkb.reshape(bsz, MEM_LEN, D_MODEL), mvb.reshape(bsz, MEM_LEN, D_MODEL),
                                   wcq, wco, ln_g, ln_b, l, 2, TM_PROMPT, jobs(*side))
        keep(side, casts)

        side = [("mix", l + 1), ("cross", l + 1)] if more else []
        yp, ys, *casts = _ffn(yp.reshape(bsz * seq, D_MODEL), ys, *wb[("ffn2", l)], ln_g, ln_b, l, 3, TM_PROMPT,
                              jobs(*side), sample_cross=(o8.reshape(nsmp, _MEM_ROWS, LANES), wco, 2))
        keep(side, casts)
        yp = yp.reshape(bsz, seq, D_MODEL)

    def window_rows(c):
        return c.reshape(DEPTH, nsmp, N_KV_HEADS, HEAD_DIM, WINDOW).transpose(0, 1, 4, 2, 3)

    def token_rows(c):
        c = c.reshape(DEPTH, bsz, MEM_LEN, _MEM_HALVES, MEM_HEADS, LANES)
        return c.transpose(0, 1, 2, 4, 3, 5).reshape(DEPTH, bsz, MEM_LEN, MEM_HEADS, MEM_HEAD_DIM)

    return (yp, ys.reshape(nsmp, 1, D_MODEL),
            jnp.stack(wkp), jnp.stack(wvp), jnp.stack(cvp), token_rows(mem_tiles[0]), token_rows(mem_tiles[1]),
            window_rows(new_windows[0]), window_rows(new_windows[1]), jnp.stack(cvs))
```
